```python
import math
import jax, jax.numpy as jnp
from jax import lax
import numpy as np

D_MODEL = 2048
BATCH = 8
SEQ = 4096
DEPTH = 1
DEC_BATCH = 16
DEC_SEQ = 2048
PAST_LEN = 128

ATT_HEADS = 8
ATT_HEAD_DIM = 64
ATT_WIDTH = ATT_HEADS * 2 * ATT_HEAD_DIM
Q_BLOCK = 128
RW_HEAD = 64
RW_WIDTH = 1024
RW_HEADS = RW_WIDTH // RW_HEAD
DECAY_LORA = 96
ICLR_LORA = 96
GATE_LORA = 256
SHIFT_WIDTH = 3 * RW_WIDTH + DECAY_LORA + ICLR_LORA + GATE_LORA
N_IN = 3 * ATT_WIDTH + SHIFT_WIDTH + 2 * D_MODEL
N_EXPERTS = 16
CAPACITY_FACTOR = 2
EXPERT_FF = 1024
NORM_EPS = 1e-6
SUBLN_EPS = 1e-5
LNX_EPS = 64e-5

kernel_name = "hybrid_diffattn_birwkv7_ecmoe_encoder"


def rmsnorm(x, g, eps=NORM_EPS):
    x32 = x.astype(jnp.float32)
    y = x32 * lax.rsqrt(jnp.mean(x32 * x32, axis=-1, keepdims=True) + eps)
    return (y * g.astype(jnp.float32)).astype(x.dtype)


def alibi_slopes(n_heads):
    return jnp.asarray(2.0 ** (-8.0 * (np.arange(n_heads) + 1) / n_heads), jnp.float32)


def diff_attention(q, k, v, lam, slopes):
    B, S = q.shape[0], q.shape[1]
    nb = S // Q_BLOCK
    scale = ATT_HEAD_DIM ** -0.5
    qb = q.reshape(B, nb, Q_BLOCK, ATT_HEADS, 2, ATT_HEAD_DIM).transpose(1, 0, 2, 3, 4, 5)
    starts = jnp.arange(nb, dtype=jnp.int32) * Q_BLOCK
    kpos = jnp.arange(S, dtype=jnp.int32)

    def block(args):
        qi, s0 = args
        qpos = s0 + jnp.arange(Q_BLOCK, dtype=jnp.int32)
        dist = jnp.abs(qpos[:, None] - kpos[None, :]).astype(jnp.float32)
        bias = -slopes[:, None, None] * dist[None]
        sc = jnp.einsum('bqhcd,bkhcd->bhcqk', qi, k,
                        preferred_element_type=jnp.float32) * scale + bias[None, :, None]
        p = jax.nn.softmax(sc, axis=-1)
        pd = p[:, :, 0] - lam * p[:, :, 1]
        return jnp.einsum('bhqk,bkhe->bqhe', pd.astype(v.dtype), v)

    o = lax.map(block, (qb, starts))
    return o.transpose(1, 0, 2, 3, 4).reshape(B, S, ATT_HEADS, 2 * ATT_HEAD_DIM)


def centred_shift(z, mu_prev, mu_next):
    zp = jnp.pad(z[:, :-1], ((0, 0), (1, 0), (0, 0)))
    zn = jnp.pad(z[:, 1:], ((0, 0), (0, 1), (0, 0)))
    return z + mu_prev * (zp - z) + mu_next * (zn - z)


def rwkv7_scan(r, w, k, v, a, b, reverse):
    B, S, H, N = r.shape
    xs = tuple(t.transpose(1, 0, 2, 3) for t in (r, w, k, v, a, b))
    s0 = jnp.zeros((B, H, N, N), jnp.float32)

    def step(state, inp):
        r_t, w_t, k_t, v_t, a_t, b_t = inp
        sa = jnp.einsum('bhvk,bhk->bhv', state, a_t)
        state = (state * w_t[:, :, None, :] + sa[..., None] * b_t[:, :, None, :]
                 + v_t[..., None] * k_t[:, :, None, :])
        return state, jnp.einsum('bhvk,bhk->bhv', state, r_t)

    _, ys = lax.scan(step, s0, xs, reverse=reverse)
    return ys.transpose(1, 0, 2, 3)


def rwkv7_bidir(z, w0, w2, a0, a2, g2, k_k, k_a, r_k, lnx_g, lnx_b):
    B, S, _ = z.shape
    o1 = RW_WIDTH
    o2 = 2 * RW_WIDTH
    o3 = 3 * RW_WIDTH
    o4 = o3 + DECAY_LORA
    o5 = o4 + ICLR_LORA
    zf = z.astype(jnp.float32)
    hs = (B, S, RW_HEADS, RW_HEAD)
    r = zf[..., :o1].reshape(hs)
    k = zf[..., o1:o2].reshape(hs)
    v = zf[..., o2:o3].reshape(hs)
    xw = jnp.tanh(zf[..., o3:o4])
    xa = zf[..., o4:o5]
    xg = jax.nn.sigmoid(zf[..., o5:])
    g = xg @ g2.astype(jnp.float32)
    kk = (k * k_k.astype(jnp.float32).reshape(RW_HEADS, RW_HEAD))
    kk = kk / jnp.maximum(jnp.linalg.norm(kk, axis=-1, keepdims=True), 1e-12)
    out = jnp.zeros(hs, jnp.float32)
    for d in range(2):
        wl = -jax.nn.softplus(-(w0[d].astype(jnp.float32) + xw @ w2[d].astype(jnp.float32))) - 0.5
        decay = jnp.exp(-jnp.exp(wl)).reshape(hs)
        a = jax.nn.sigmoid(a0[d].astype(jnp.float32) + xa @ a2[d].astype(jnp.float32)).reshape(hs)
        kd = k * (1.0 + (a - 1.0) * k_a.astype(jnp.float32).reshape(RW_HEADS, RW_HEAD))
        y = rwkv7_scan(r, decay, kd, v, -kk, kk * a, reverse=(d == 1))
        bonus = jnp.sum(r * kd * r_k.astype(jnp.float32), axis=-1, keepdims=True) * v
        out = out + y + bonus
    mu = jnp.mean(out, axis=-1, keepdims=True)
    var = jnp.mean(jnp.square(out - mu), axis=-1, keepdims=True)
    yn = ((out - mu) * lax.rsqrt(var + LNX_EPS)).reshape(B, S, RW_WIDTH)
    yn = yn * lnx_g.astype(jnp.float32) + lnx_b.astype(jnp.float32)
    return (yn * g).astype(z.dtype)


def expert_choice_moe(x, w_router, w_gate, w_up, w_down):
    B, S, D = x.shape
    n_tok = B * S
    cap = max(1, CAPACITY_FACTOR * n_tok // N_EXPERTS)
    xt = x.reshape(n_tok, D)
    logits = jnp.einsum('nd,de->ne', xt, w_router, preferred_element_type=jnp.float32)
    affin = jax.nn.softmax(logits, axis=-1)
    gval, idx = lax.top_k(affin.T, cap)
    xe = xt[idx]
    h = jax.nn.silu(jnp.einsum('ecd,edf->ecf', xe, w_gate)) * jnp.einsum('ecd,edf->ecf', xe, w_up)
    ye = jnp.einsum('ecf,efd->ecd', h, w_down) * gval[..., None].astype(x.dtype)
    out = jnp.zeros((n_tok, D), x.dtype).at[idx.reshape(-1)].add(ye.reshape(-1, D))
    return out.reshape(B, S, D)


def trunk(x, p):
    slopes = alibi_slopes(ATT_HEADS)
    B, S, _ = x.shape
    for l in range(DEPTH):
        lam_init = 0.8 - 0.6 * math.exp(-0.3 * l)
        xn = rmsnorm(x, p['norm_mix_g'][l])
        proj = jnp.einsum('bsd,dn->bsn', xn, p['w_in'][l])
        q = proj[..., :ATT_WIDTH].reshape(B, S, ATT_HEADS, 2, ATT_HEAD_DIM)
        k = proj[..., ATT_WIDTH:2 * ATT_WIDTH].reshape(B, S, ATT_HEADS, 2, ATT_HEAD_DIM)
        v = proj[..., 2 * ATT_WIDTH:3 * ATT_WIDTH].reshape(B, S, ATT_HEADS, 2 * ATT_HEAD_DIM)
        zr = proj[..., 3 * ATT_WIDTH:3 * ATT_WIDTH + SHIFT_WIDTH]
        gates = proj[..., 3 * ATT_WIDTH + SHIFT_WIDTH:]
        lq1 = p['lambda_q1'][l].astype(jnp.float32)
        lk1 = p['lambda_k1'][l].astype(jnp.float32)
        lq2 = p['lambda_q2'][l].astype(jnp.float32)
        lk2 = p['lambda_k2'][l].astype(jnp.float32)
        lam = jnp.exp(jnp.sum(lq1 * lk1)) - jnp.exp(jnp.sum(lq2 * lk2)) + lam_init
        o_att = diff_attention(q, k, v, lam, slopes)
        o_att = (rmsnorm(o_att, p['subln_g'][l], SUBLN_EPS) * (1.0 - lam_init)).reshape(B, S, ATT_WIDTH)
        mu = p['shift_mu'][l]
        z = centred_shift(zr, mu[0], mu[1])
        o_rw = rwkv7_bidir(z, p['rw_w0'][l], p['rw_w2'][l], p['rw_a0'][l], p['rw_a2'][l], p['rw_g2'][l],
                           p['rw_k_k'][l], p['rw_k_a'][l], p['rw_r_k'][l], p['lnx_g'][l], p['lnx_b'][l])
        gt = jax.nn.sigmoid((gates + p['gate_b'][l].reshape(2 * D_MODEL)).astype(jnp.float32)).astype(x.dtype)
        merged = (gt[..., :D_MODEL] * jnp.einsum('bsc,cd->bsd', o_att, p['w_br_att'][l])
                  + gt[..., D_MODEL:] * jnp.einsum('bsc,cd->bsd', o_rw, p['w_br_rw'][l]))
        x = x + jnp.einsum('bsd,de->bse', merged, p['w_out'][l])
        hn = rmsnorm(x, p['norm_ffn_g'][l])
        x = x + expert_choice_moe(hn, p['w_router'][l], p['w_gate_e'][l], p['w_up_e'][l], p['w_down_e'][l])
    return rmsnorm(x, p['norm_final_g'])


def setup_inputs(seed: int = 0) -> dict:
    key = jax.random.key(seed)
    ks = jax.random.split(key, 32)
    L, D = DEPTH, D_MODEL

    def nrm(k, shape, scale):
        return jax.random.normal(k, shape, jnp.float32) * scale

    return {
        "x_prompt": nrm(ks[0], (BATCH, SEQ, D), 1.0),
        "x_sample": nrm(ks[1], (DEC_BATCH, DEC_SEQ, D), 1.0),
        "norm_mix_g": 1.0 + nrm(ks[2], (L, D), 0.02),
        "w_in": nrm(ks[3], (L, D, N_IN), D ** -0.5),
        "shift_mu": jax.random.uniform(ks[4], (L, 2, SHIFT_WIDTH), jnp.float32, 0.0, 0.5),
        "lambda_q1": nrm(ks[5], (L, ATT_HEAD_DIM), 0.1),
        "lambda_k1": nrm(ks[6], (L, ATT_HEAD_DIM), 0.1),
        "lambda_q2": nrm(ks[7], (L, ATT_HEAD_DIM), 0.1),
        "lambda_k2": nrm(ks[8], (L, ATT_HEAD_DIM), 0.1),
        "subln_g": 1.0 + nrm(ks[9], (L, 2 * ATT_HEAD_DIM), 0.02),
        "rw_w0": jax.random.uniform(ks[10], (L, 2, RW_WIDTH), jnp.float32, -4.0, 1.0),
        "rw_w2": nrm(ks[11], (L, 2, DECAY_LORA, RW_WIDTH), 0.5 * DECAY_LORA ** -0.5),
        "rw_a0": nrm(ks[12], (L, 2, RW_WIDTH), 0.1),
        "rw_a2": nrm(ks[13], (L, 2, ICLR_LORA, RW_WIDTH), 0.5 * ICLR_LORA ** -0.5),
        "rw_g2": nrm(ks[14], (L, GATE_LORA, RW_WIDTH), GATE_LORA ** -0.5),
        "rw_k_k": 0.85 + nrm(ks[15], (L, RW_WIDTH), 0.05),
        "rw_k_a": 1.0 + nrm(ks[16], (L, RW_WIDTH), 0.05),
        "rw_r_k": nrm(ks[17], (L, RW_HEADS, RW_HEAD), 0.3),
        "lnx_g": 1.0 + nrm(ks[18], (L, RW_WIDTH), 0.02),
        "lnx_b": nrm(ks[19], (L, RW_WIDTH), 0.02),
        "gate_b": nrm(ks[20], (L, 2, D), 0.02),
        "w_br_att": nrm(ks[21], (L, ATT_WIDTH, D), ATT_WIDTH ** -0.5),
        "w_br_rw": nrm(ks[22], (L, RW_WIDTH, D), RW_WIDTH ** -0.5),
        "w_out": nrm(ks[23], (L, D, D), D ** -0.5),
        "norm_ffn_g": 1.0 + nrm(ks[24], (L, D), 0.02),
        "w_router": nrm(ks[25], (L, D, N_EXPERTS), D ** -0.5),
        "w_gate_e": nrm(ks[26], (L, N_EXPERTS, D, EXPERT_FF), D ** -0.5),
        "w_up_e": nrm(ks[27], (L, N_EXPERTS, D, EXPERT_FF), D ** -0.5),
        "w_down_e": nrm(ks[28], (L, N_EXPERTS, EXPERT_FF, D), EXPERT_FF ** -0.5),
        "norm_final_g": 1.0 + nrm(ks[29], (D,), 0.02),
    }


def reference(x_prompt, x_sample, norm_mix_g, w_in, shift_mu, lambda_q1, lambda_k1, lambda_q2, lambda_k2,
              subln_g, rw_w0, rw_w2, rw_a0, rw_a2, rw_g2, rw_k_k, rw_k_a, rw_r_k, lnx_g, lnx_b, gate_b,
              w_br_att, w_br_rw, w_out, norm_ffn_g, w_router, w_gate_e, w_up_e, w_down_e, norm_final_g):
    params = dict(norm_mix_g=norm_mix_g, w_in=w_in, shift_mu=shift_mu, lambda_q1=lambda_q1,
                  lambda_k1=lambda_k1, lambda_q2=lambda_q2, lambda_k2=lambda_k2, subln_g=subln_g,
                  rw_w0=rw_w0, rw_w2=rw_w2, rw_a0=rw_a0, rw_a2=rw_a2, rw_g2=rw_g2, rw_k_k=rw_k_k,
                  rw_k_a=rw_k_a, rw_r_k=rw_r_k, lnx_g=lnx_g, lnx_b=lnx_b, gate_b=gate_b,
                  w_br_att=w_br_att, w_br_rw=w_br_rw, w_out=w_out, norm_ffn_g=norm_ffn_g,
                  w_router=w_router, w_gate_e=w_gate_e, w_up_e=w_up_e, w_down_e=w_down_e,
                  norm_final_g=norm_final_g)
    y_prompt = trunk(x_prompt, params)
    y_sample = trunk(x_sample, params)
    return (y_prompt, y_sample)
```

```python
import functools
import math

import jax
import jax.numpy as jnp
from jax import lax
from jax.experimental import pallas as pl
from jax.experimental.pallas import tpu as pltpu

F32 = jnp.float32
BF16 = jnp.bfloat16
I32 = jnp.int32

D_MODEL = 2048
ATT_HEADS = 8
ATT_HEAD_DIM = 64
ATT_WIDTH = ATT_HEADS * 2 * ATT_HEAD_DIM
RW_HEAD = 64
RW_WIDTH = 1024
DECAY_LORA = 96
ICLR_LORA = 96
GATE_LORA = 256
SHIFT_WIDTH = 3 * RW_WIDTH + DECAY_LORA + ICLR_LORA + GATE_LORA
N_EXPERTS = 16
CAPACITY_FACTOR = 2
EXPERT_FF = 1024
NORM_EPS = 1e-6
SUBLN_EPS = 1e-5
LNX_EPS = 64e-5
LAM_INIT = 0.8 - 0.6 * math.exp(-0.3 * 0)

LANES = 128
VMEM_LIMIT = 56 * 1024 * 1024

SL_GATE_ATT, SL_GATE_RW = 0, 16
SL_ATT_Q, SL_ATT_K, SL_ATT_V = 32, 40, 48
SL_RW = 56
N_RW_SLABS = 28
N_SLABS = 84
CHUNK = 64


def _cparams(sem):
    return pltpu.CompilerParams(dimension_semantics=sem, vmem_limit_bytes=VMEM_LIMIT)


def _sigmoid(x):
    return 1.0 / (1.0 + jnp.exp(-x))


def _split3(x):
    hi = x.astype(BF16)
    r1 = x - hi.astype(F32)
    mid = r1.astype(BF16)
    lo = (r1 - mid.astype(F32)).astype(BF16)
    return hi, mid, lo


def _dot(a, b):
    return jnp.dot(a, b, preferred_element_type=F32)


def _dot_nt(a, b):
    return lax.dot_general(a, b, (((1,), (1,)), ((), ())), preferred_element_type=F32)


def _dot_tn(a, b):
    return lax.dot_general(a, b, (((0,), (0,)), ((), ())), preferred_element_type=F32)


def _dot_f32(a_bf16_exact, x):
    hi, mid, lo = _split3(x)
    return _dot(a_bf16_exact, hi) + _dot(a_bf16_exact, mid) + _dot(a_bf16_exact, lo)


def _rearrange_in_cols(a):
    att = a[..., :3 * ATT_WIDTH]
    zr = a[..., 3 * ATT_WIDTH:3 * ATT_WIDTH + SHIFT_WIDTH]
    gates = a[..., 3 * ATT_WIDTH + SHIFT_WIDTH:]
    o3 = 3 * RW_WIDTH
    o4 = o3 + DECAY_LORA
    o5 = o4 + ICLR_LORA
    pad = [(0, 0)] * (a.ndim - 1)
    lw = jnp.pad(zr[..., o3:o4], pad + [(0, LANES - DECAY_LORA)])
    la = jnp.pad(zr[..., o4:o5], pad + [(0, LANES - ICLR_LORA)])
    return jnp.concatenate([gates, att, zr[..., :o3], lw, la, zr[..., o5:]], axis=-1)


def _prep_in_weights(w_in, shift_mu):
    w_slab = _rearrange_in_cols(w_in).astype(BF16)
    mu_full = jnp.pad(shift_mu, ((0, 0), (3 * ATT_WIDTH, 2 * D_MODEL)))
    mu_slab = _rearrange_in_cols(mu_full)[:, SL_RW * LANES:]
    return w_slab, mu_slab.reshape(2, N_RW_SLABS, 1, LANES)


def _inproj_kernel(x_ref, g_ref, w_ref, o_ref, xn_ref, *, n_out_slabs):
    @pl.when(pl.program_id(1) == 0)
    def _():
        x = x_ref[...]
        ms = jnp.mean(x * x, axis=-1, keepdims=True)
        xn_ref[...] = (x * lax.rsqrt(ms + NORM_EPS) * g_ref[...]).astype(BF16)

    acc = _dot(xn_ref[...], w_ref[...])
    for c in range(n_out_slabs):
        o_ref[c] = acc[:, c * LANES:(c + 1) * LANES].astype(BF16)


def _inproj(x2d, g, w_slab):
    n = x2d.shape[0]
    tm = min(1024, n)
    tn = 768
    n_out_slabs = tn // LANES
    grid = (n // tm, (N_SLABS * LANES) // tn)
    return pl.pallas_call(
        functools.partial(_inproj_kernel, n_out_slabs=n_out_slabs),
        grid=grid,
        in_specs=[
            pl.BlockSpec((tm, D_MODEL), lambda i, j: (i, 0)),
            pl.BlockSpec((1, D_MODEL), lambda i, j: (0, 0)),
            pl.BlockSpec((D_MODEL, tn), lambda i, j: (0, j)),
        ],
        out_specs=pl.BlockSpec((n_out_slabs, tm, LANES), lambda i, j: (j, i, 0)),
        out_shape=jax.ShapeDtypeStruct((N_SLABS, n, LANES), BF16),
        scratch_shapes=[pltpu.VMEM((tm, D_MODEL), BF16)],
        compiler_params=_cparams(("parallel", "arbitrary")),
        name="inproj",
    )(x2d, g, w_slab)


def _attn_kernel(slopes_ref, lq1_ref, lk1_ref, lq2_ref, lk2_ref, subg_ref, q_ref, k_ref, v_ref, o_ref,
                 kt1_ref, kt2_ref, vaug_ref, s_ref, *, seq, tq):
    h = pl.program_id(1)
    qi = pl.program_id(2)
    slope = slopes_ref[h]
    q0 = pl.multiple_of(qi * tq, tq)

    @pl.when(qi == 0)
    def _():
        kt = k_ref[0].astype(F32).T
        row = lax.broadcasted_iota(I32, kt.shape, 0)
        kt1_ref[...] = jnp.where(row < ATT_HEAD_DIM, kt, 0.0).astype(BF16)
        kt2_ref[...] = jnp.where(row >= ATT_HEAD_DIM, kt, 0.0).astype(BF16)
        lane = lax.broadcasted_iota(I32, (seq, LANES), 1)
        vaug_ref[:, :LANES] = v_ref[0]
        vaug_ref[:, LANES:] = jnp.where(lane == 0, 1.0, 0.0).astype(BF16)

    col = lax.broadcasted_iota(I32, (16, seq), 1)
    r16 = lax.broadcasted_iota(I32, (16, seq), 0)
    jp = col - q0
    sigma = jnp.where(jp < 0, -1.0, jnp.where(jp >= tq, 1.0, 0.0)).astype(F32)
    jh = (jp >> 8).astype(F32)
    jl = (jp & 255).astype(F32)
    feat = jnp.where(r16 == 0, sigma,
                     jnp.where(r16 == 1, -sigma * (slope * 256.0) * jh,
                               jnp.where(r16 == 2, -sigma * slope * jl, 0.0))).astype(BF16)
    kt1_ref[ATT_HEAD_DIM:ATT_HEAD_DIM + 16, :] = feat
    kt2_ref[0:16, :] = feat

    q = q_ref[0].astype(F32) * (ATT_HEAD_DIM ** -0.5)
    lane = lax.broadcasted_iota(I32, (tq, LANES), 1)
    ip = lax.broadcasted_iota(I32, (tq, LANES), 0).astype(F32)
    lhs1 = jnp.where(lane < ATT_HEAD_DIM, q,
                     jnp.where(lane == ATT_HEAD_DIM, slope * ip,
                               jnp.where(lane <= ATT_HEAD_DIM + 2, 1.0, 0.0))).astype(BF16)
    lhs2 = jnp.where(lane >= ATT_HEAD_DIM, q,
                     jnp.where(lane == 0, slope * ip,
                               jnp.where(lane <= 2, 1.0, 0.0))).astype(BF16)

    di = lax.broadcasted_iota(I32, (tq, tq), 0)
    dj = lax.broadcasted_iota(I32, (tq, tq), 1)
    diag_bias = -slope * jnp.abs(di - dj).astype(F32)

    def one_map(lhs, kt_ref):
        s_ref[...] = _dot(lhs, kt_ref[...])
        s_ref[:, pl.ds(q0, tq)] += diag_bias
        s = s_ref[...]
        m = jnp.max(s, axis=-1, keepdims=True)
        e = jnp.exp(s - m).astype(BF16)
        oa = _dot(e, vaug_ref[...])
        return oa[:, :LANES] / oa[:, LANES:LANES + 1]

    o1 = one_map(lhs1, kt1_ref)
    o2 = one_map(lhs2, kt2_ref)
    lam = (jnp.exp(jnp.sum(lq1_ref[...] * lk1_ref[...], keepdims=True))
           - jnp.exp(jnp.sum(lq2_ref[...] * lk2_ref[...], keepdims=True)) + LAM_INIT)
    out = o1 - lam * o2
    ms = jnp.mean(out * out, axis=-1, keepdims=True)
    y = out * lax.rsqrt(ms + SUBLN_EPS) * subg_ref[...]
    o_ref[0] = (y * (1.0 - LAM_INIT)).astype(BF16)


def _attention(slabs, slopes, lq1, lk1, lq2, lk2, subg, batch, seq):
    n = batch * seq
    tq = 256
    nq = seq // tq
    vec = lambda: pl.BlockSpec((1, ATT_HEAD_DIM), lambda b, h, i: (0, 0))
    return pl.pallas_call(
        functools.partial(_attn_kernel, seq=seq, tq=tq),
        grid=(batch, ATT_HEADS, nq),
        in_specs=[
            pl.BlockSpec(memory_space=pltpu.SMEM),
            vec(), vec(), vec(), vec(),
            pl.BlockSpec((1, LANES), lambda b, h, i: (0, 0)),
            pl.BlockSpec((1, tq, LANES), lambda b, h, i: (SL_ATT_Q + h, b * nq + i, 0)),
            pl.BlockSpec((1, seq, LANES), lambda b, h, i: (SL_ATT_K + h, b, 0)),
            pl.BlockSpec((1, seq, LANES), lambda b, h, i: (SL_ATT_V + h, b, 0)),
        ],
        out_specs=pl.BlockSpec((1, tq, LANES), lambda b, h, i: (h, b * nq + i, 0)),
        out_shape=jax.ShapeDtypeStruct((ATT_HEADS, n, LANES), BF16),
        scratch_shapes=[
            pltpu.VMEM((LANES, seq), BF16),
            pltpu.VMEM((LANES, seq), BF16),
            pltpu.VMEM((seq, 2 * LANES), BF16),
            pltpu.VMEM((tq, seq), F32),
        ],
        compiler_params=_cparams(("parallel", "parallel", "arbitrary")),
        name="diff_attn",
    )(slopes, lq1, lk1, lq2, lk2, subg, slabs, slabs, slabs)


def _head_segsum(x, bd):
    hi, mid, lo = _split3(x)
    return _dot(hi, bd) + _dot(mid, bd) + _dot(lo, bd)


def _block_ones():
    ri = lax.broadcasted_iota(I32, (LANES, LANES), 0)
    ci = lax.broadcasted_iota(I32, (LANES, LANES), 1)
    return jnp.where((ri >> 6) == (ci >> 6), 1.0, 0.0).astype(BF16)


def _softplus(x):
    return jnp.maximum(x, 0.0) + jnp.log1p(jnp.exp(-jnp.abs(x)))


def _rwprep_kernel(main_ref, prev_ref, next_ref, mu_ref, w0_ref, a0_ref, kk_ref, ka_ref, w2_ref, a2_ref,
                   g2_ref, r_ref, v_ref, na_ref, g_ref, kd_ref, b_ref, ld_ref, *, nt):
    i = pl.program_id(1)
    z = main_ref[...].astype(F32)
    t = z.shape[1]
    prev_row = jnp.where(i > 0, prev_ref[...].astype(F32)[:, 15:16, :], 0.0)
    next_row = jnp.where(i < nt - 1, next_ref[...].astype(F32)[:, 0:1, :], 0.0)
    row = lax.broadcasted_iota(I32, z.shape, 1)
    zp = jnp.where(row == 0, prev_row, pltpu.roll(z, 1, 1))
    zn = jnp.where(row == t - 1, next_row, pltpu.roll(z, t - 1, 1))
    z = z + mu_ref[0] * (zp - z) + mu_ref[1] * (zn - z)

    xw = jnp.tanh(z[24]).astype(BF16)
    xa = z[25].astype(BF16)
    xg = _sigmoid(jnp.concatenate([z[26], z[27]], axis=1)).astype(BF16)
    g_full = _dot(xg, g2_ref[...])
    lw = [_dot(xw, w2_ref[d]) for d in range(2)]
    la = [_dot(xa, a2_ref[d]) for d in range(2)]
    bd = _block_ones()
    for c in range(8):
        cs = slice(c * LANES, (c + 1) * LANES)
        kc = z[8 + c]
        kk = kc * kk_ref[c]
        nrm = jnp.sqrt(_head_segsum(kk * kk, bd))
        kk = kk / jnp.maximum(nrm, 1e-12)
        r_ref[c] = z[c]
        v_ref[c] = z[16 + c]
        na_ref[c] = -kk
        g_ref[c] = g_full[:, cs]
        for d in range(2):
            wl = -_softplus(-(w0_ref[d, c] + lw[d][:, cs])) - 0.5
            ld_ref[d, c] = -jnp.exp(wl)
            asig = _sigmoid(a0_ref[d, c] + la[d][:, cs])
            kd_ref[d, c] = kc * (1.0 + (asig - 1.0) * ka_ref[c])
            b_ref[d, c] = kk * asig


def _prep_rw_params(p):
    vec = lambda a: a.reshape(a.shape[:-1] + (8, 1, LANES))
    pad_rows = lambda a: jnp.pad(a, ((0, 0), (0, LANES - a.shape[1]), (0, 0))).astype(BF16)
    return (vec(p["rw_w0"][0]), vec(p["rw_a0"][0]), vec(p["rw_k_k"][0]), vec(p["rw_k_a"][0]),
            pad_rows(p["rw_w2"][0]), pad_rows(p["rw_a2"][0]), p["rw_g2"][0].astype(BF16),
            vec(p["rw_r_k"][0].reshape(RW_WIDTH)), vec(p["lnx_g"][0]), vec(p["lnx_b"][0]))


def _rwprep(slabs, mu_slab, w0, a0, k_k, k_a, w2, a2, g2, batch, seq):
    n = batch * seq
    t = 256
    nt = seq // t
    hb = 16
    full = lambda shape: pl.BlockSpec(shape, lambda b, i: (0,) * len(shape))
    rows = lambda b, i: b * nt + i
    o8 = pl.BlockSpec((8, t, LANES), lambda b, i: (0, rows(b, i), 0))
    o28 = pl.BlockSpec((2, 8, t, LANES), lambda b, i: (0, 0, rows(b, i), 0))
    s8 = jax.ShapeDtypeStruct((8, n, LANES), F32)
    s28 = jax.ShapeDtypeStruct((2, 8, n, LANES), F32)
    rw_blk = SL_RW // N_RW_SLABS
    return pl.pallas_call(
        functools.partial(_rwprep_kernel, nt=nt),
        grid=(batch, nt),
        in_specs=[
            pl.BlockSpec((N_RW_SLABS, t, LANES), lambda b, i: (rw_blk, rows(b, i), 0)),
            pl.BlockSpec((N_RW_SLABS, hb, LANES),
                         lambda b, i: (rw_blk, jnp.maximum((b * seq + i * t) // hb - 1, 0), 0)),
            pl.BlockSpec((N_RW_SLABS, hb, LANES),
                         lambda b, i: (rw_blk, jnp.minimum((b * seq + (i + 1) * t) // hb, n // hb - 1), 0)),
            full((2, N_RW_SLABS, 1, LANES)),
            full((2, 8, 1, LANES)), full((2, 8, 1, LANES)), full((8, 1, LANES)), full((8, 1, LANES)),
            full((2, LANES, RW_WIDTH)), full((2, LANES, RW_WIDTH)), full((GATE_LORA, RW_WIDTH)),
        ],
        out_specs=[o8, o8, o8, o8, o28, o28, o28],
        out_shape=[s8, s8, s8, s8, s28, s28, s28],
        compiler_params=_cparams(("parallel", "parallel")),
        name="rwkv_prep",
    )(slabs, slabs, slabs, mu_slab, w0, a0, k_k, k_a, w2, a2, g2)


def _rwscan_kernel(rf_ref, vf_ref, naf_ref, kdf_ref, bf_ref, ldf_ref,
                   rb_ref, vb_ref, nab_ref, kdb_ref, bb_ref, ldb_ref, rk_ref,
                   yf_ref, yb_ref, st_ref, *, nc):
    @pl.when(pl.program_id(2) == 0)
    def _():
        st_ref[...] = jnp.zeros_like(st_ref)

    lane = lax.broadcasted_iota(I32, (CHUNK, LANES), 1)
    head0 = lane < RW_HEAD
    ri = lax.broadcasted_iota(I32, (LANES, LANES), 0)
    ci = lax.broadcasted_iota(I32, (LANES, LANES), 1)
    same = (ri >> 6) == (ci >> 6)
    tt = ri & (CHUNK - 1)
    ss = ci & (CHUNK - 1)
    eye = jnp.where(ri == ci, 1.0, 0.0).astype(F32)
    tr = lax.broadcasted_iota(I32, (CHUNK, CHUNK), 0)
    tc = lax.broadcasted_iota(I32, (CHUNK, CHUNK), 1)
    bd = _block_ones()
    rk = rk_ref[0]

    def stack(x):
        return jnp.concatenate([jnp.where(head0, x, 0.0), jnp.where(head0, 0.0, x)], axis=0)

    dirs = (
        (rf_ref, vf_ref, naf_ref, kdf_ref, bf_ref, ldf_ref, yf_ref),
        (rb_ref, vb_ref, nab_ref, kdb_ref, bb_ref, ldb_ref, yb_ref),
    )
    for d, (r_ref, v_ref, na_ref, kd_ref, b_ref, ld_ref, y_ref) in enumerate(dirs):
        if d == 0:
            strict = same & (ss < tt)
            incl = same & (ss <= tt)
            tri = jnp.where(tc <= tr, 1.0, 0.0).astype(BF16)
            last = CHUNK - 1
        else:
            strict = same & (ss > tt)
            incl = same & (ss >= tt)
            tri = jnp.where(tc >= tr, 1.0, 0.0).astype(BF16)
            last = 0
        order = range(nc) if d == 0 else range(nc - 1, -1, -1)
        for ch in order:
            sl = slice(ch * CHUNK, (ch + 1) * CHUNK)
            r = r_ref[0, sl, :]
            v = v_ref[0, sl, :]
            na = na_ref[0, sl, :]
            kd = kd_ref[0, 0, sl, :]
            b = b_ref[0, 0, sl, :]
            ld = ld_ref[0, 0, sl, :]
            c = _dot_f32(tri, ld)
            total = c[last:last + 1]
            e_c = jnp.exp(c)
            e_nc = jnp.exp(-c)
            e_tc = jnp.exp(total - c)
            a_t = stack(na * jnp.exp(c - ld)).astype(BF16)
            r_t = stack(r * e_c).astype(BF16)
            v_s = stack(v).astype(BF16)
            lhs = jnp.concatenate([a_t, r_t], axis=0)
            rhs = jnp.concatenate([stack(b * e_nc), stack(kd * e_nc)], axis=0).astype(BF16)
            p = _dot_nt(lhs, rhs)
            n_ab = jnp.where(strict, p[:LANES, :LANES], 0.0)
            a_ak = jnp.where(strict, p[:LANES, LANES:], 0.0).astype(BF16)
            p_rb = jnp.where(incl, p[LANES:, :LANES], 0.0).astype(BF16)
            p_rk = jnp.where(incl, p[LANES:, LANES:], 0.0).astype(BF16)
            x = eye + n_ab
            nk = n_ab
            for _ in range(5):
                nkb = nk.astype(BF16)
                nk = _dot(nkb, nkb)
                x = x + _dot(x.astype(BF16), nk.astype(BF16))
            w = _dot(a_ak, v_s)
            au = _dot(x.astype(BF16), jnp.concatenate([a_t, w.astype(BF16)], axis=1))
            y0 = _dot(p_rk, v_s)
            st = st_ref[d]
            uy = _dot_nt(jnp.concatenate([au[:, :LANES].astype(BF16), r_t], axis=0), st.astype(BF16))
            u = (uy[:LANES] + au[:, LANES:]).astype(BF16)
            y = uy[LANES:] + _dot(p_rb, u) + y0
            bk = jnp.concatenate([stack(b * e_tc), stack(kd * e_tc)], axis=0).astype(BF16)
            st_ref[d] = st * jnp.exp(total) + _dot_tn(jnp.concatenate([u, v_s], axis=0), bk)
            bonus = _head_segsum(r * kd * rk, bd) * v
            y_ref[0, sl, :] = y[:CHUNK] + y[CHUNK:] + bonus


def _rwscan(r, v, na, kd, b, ld, r_k, batch, seq):
    n = batch * seq
    nc = 2
    tcs = CHUNK * nc
    nt = seq // tcs
    fwd = lambda bi, c, t: bi * nt + t
    bwd = lambda bi, c, t: bi * nt + nt - 1 - t
    s3 = lambda rows: pl.BlockSpec((1, tcs, LANES), lambda bi, c, t: (c, rows(bi, c, t), 0))
    s4 = lambda d, rows: pl.BlockSpec((1, 1, tcs, LANES), lambda bi, c, t: (d, c, rows(bi, c, t), 0))
    out = jax.ShapeDtypeStruct((8, n, LANES), F32)
    return pl.pallas_call(
        functools.partial(_rwscan_kernel, nc=nc),
        grid=(batch, 8, nt),
        in_specs=[s3(fwd), s3(fwd), s3(fwd), s4(0, fwd), s4(0, fwd), s4(0, fwd),
                  s3(bwd), s3(bwd), s3(bwd), s4(1, bwd), s4(1, bwd), s4(1, bwd),
                  pl.BlockSpec((1, 1, LANES), lambda bi, c, t: (c, 0, 0))],
        out_specs=[s3(fwd), s3(bwd)],
        out_shape=[out, out],
        scratch_shapes=[pltpu.VMEM((2, LANES, LANES), F32)],
        compiler_params=_cparams(("parallel", "parallel", "arbitrary")),
        name="rwkv_scan",
    )(r, v, na, kd, b, ld, r, v, na, kd, b, ld, r_k)


def _merge_kernel(x_ref, oatt_ref, yf_ref, yb_ref, g_ref, gates_ref, gb_ref, lng_ref, lnb_ref,
                  wba_ref, wbr_ref, wout_ref, h_ref):
    bd = _block_ones()
    orw = []
    for c in range(8):
        y = yf_ref[c] + yb_ref[c]
        mu = _head_segsum(y, bd) * (1.0 / RW_HEAD)
        yc = y - mu
        var = _head_segsum(yc * yc, bd) * (1.0 / RW_HEAD)
        yn = yc * lax.rsqrt(var + LNX_EPS) * lng_ref[c] + lnb_ref[c]
        orw.append((yn * g_ref[c]).astype(BF16))
    orw = jnp.concatenate(orw, axis=1)
    oatt = jnp.concatenate([oatt_ref[c] for c in range(8)], axis=1)
    ga = jnp.concatenate([gates_ref[c] for c in range(16)], axis=1).astype(F32) + gb_ref[0]
    gr = jnp.concatenate([gates_ref[16 + c] for c in range(16)], axis=1).astype(F32) + gb_ref[1]
    merged = _sigmoid(ga) * _dot(oatt, wba_ref[...]) + _sigmoid(gr) * _dot(orw, wbr_ref[...])
    h_ref[...] = x_ref[...] + _dot(merged.astype(BF16), wout_ref[...])


def _merge(x2d, oatt, yf, yb, g, slabs, gate_b, lng, lnb, wba, wbr, wout):
    n = x2d.shape[0]
    tm = min(256, n)
    const = lambda shape: pl.BlockSpec(shape, lambda i: (0,) * len(shape), pipeline_mode=pl.Buffered(1))
    s8 = pl.BlockSpec((8, tm, LANES), lambda i: (0, i, 0))
    return pl.pallas_call(
        _merge_kernel,
        grid=(n // tm,),
        in_specs=[
            pl.BlockSpec((tm, D_MODEL), lambda i: (i, 0)),
            s8, s8, s8, s8,
            pl.BlockSpec((32, tm, LANES), lambda i: (0, i, 0)),
            const((2, 1, D_MODEL)), const((8, 1, LANES)), const((8, 1, LANES)),
            const((ATT_WIDTH, D_MODEL)), const((RW_WIDTH, D_MODEL)), const((D_MODEL, D_MODEL)),
        ],
        out_specs=pl.BlockSpec((tm, D_MODEL), lambda i: (i, 0)),
        out_shape=jax.ShapeDtypeStruct((n, D_MODEL), F32),
        compiler_params=_cparams(("parallel",)),
        name="merge_outproj",
    )(x2d, oatt, yf, yb, g, slabs, gate_b, lng, lnb, wba, wbr, wout)


def _router_kernel(h_ref, g_ref, wr_ref, hn_ref, aff_ref):
    x = h_ref[...]
    ms = jnp.mean(x * x, axis=-1, keepdims=True)
    hn = x * lax.rsqrt(ms + NORM_EPS) * g_ref[...]
    hn_ref[...] = hn
    xh, xm, xl = _split3(hn)
    wh, wm, wl = _split3(wr_ref[...])
    logits = (_dot(xh, wh) + _dot(xh, wm) + _dot(xm, wh)
              + _dot(xh, wl) + _dot(xl, wh) + _dot(xm, wm))
    lt = logits.T[:N_EXPERTS]
    m = jnp.max(lt, axis=0, keepdims=True)
    e = jnp.exp(lt - m)
    aff_ref[...] = e / jnp.sum(e, axis=0, keepdims=True)


def _router(h2d, g, wr_pad):
    n = h2d.shape[0]
    tm = min(256, n)
    return pl.pallas_call(
        _router_kernel,
        grid=(n // tm,),
        in_specs=[
            pl.BlockSpec((tm, D_MODEL), lambda i: (i, 0)),
            pl.BlockSpec((1, D_MODEL), lambda i: (0, 0)),
            pl.BlockSpec((D_MODEL, LANES), lambda i: (0, 0)),
        ],
        out_specs=[
            pl.BlockSpec((tm, D_MODEL), lambda i: (i, 0)),
            pl.BlockSpec((N_EXPERTS, tm), lambda i: (0, i)),
        ],
        out_shape=[
            jax.ShapeDtypeStruct((n, D_MODEL), F32),
            jax.ShapeDtypeStruct((N_EXPERTS, n), F32),
        ],
        compiler_params=_cparams(("parallel",)),
        name="router",
    )(h2d, g, wr_pad)


def _select_kernel(aff_ref, pos_ref, *, cap):
    bits = pltpu.bitcast(aff_ref[...], I32)
    nrow = bits.shape[1]

    def count(mask):
        c = jnp.sum(jnp.where(mask, 1, 0), axis=2, keepdims=True)
        return jnp.sum(c, axis=1, keepdims=True)

    def body(_, carry):
        lo, hi = carry
        mid = lo + ((hi - lo) >> 1)
        ok = count(bits >= mid) >= cap
        return jnp.where(ok, mid, lo), jnp.where(ok, hi, mid)

    lo0 = jnp.zeros((N_EXPERTS, 1, 1), I32)
    hi0 = jnp.full((N_EXPERTS, 1, 1), 0x7F800000, I32)
    thr, _ = lax.fori_loop(0, 31, body, (lo0, hi0))
    gt = bits > thr
    eq = bits == thr
    need = cap - count(gt)

    ri = lax.broadcasted_iota(I32, (LANES, LANES), 0)
    ci = lax.broadcasted_iota(I32, (LANES, LANES), 1)
    upper = jnp.where(ri <= ci, 1.0, 0.0).astype(BF16)
    rr = lax.broadcasted_iota(I32, (nrow, nrow), 0)
    rc = lax.broadcasted_iota(I32, (nrow, nrow), 1)
    lower_strict = jnp.where(rc < rr, 1.0, 0.0).astype(BF16)

    def excl_prefix(mask):
        x = jnp.where(mask, 1.0, 0.0).astype(BF16)
        incl = _dot(x.reshape(N_EXPERTS * nrow, LANES), upper).reshape(N_EXPERTS, nrow, LANES)
        tot = jnp.broadcast_to(incl[:, :, LANES - 1:LANES], incl.shape).astype(BF16)
        before = jnp.stack([_dot(lower_strict, tot[e]) for e in range(N_EXPERTS)], axis=0)
        return (incl - x.astype(F32) + before).astype(I32)

    sel = gt | (eq & (excl_prefix(eq) < need))
    pos_ref[...] = jnp.where(sel, excl_prefix(sel), -1)


def _select(aff3, cap):
    return pl.pallas_call(
        functools.partial(_select_kernel, cap=cap),
        out_shape=jax.ShapeDtypeStruct(aff3.shape, I32),
        compiler_params=pltpu.CompilerParams(vmem_limit_bytes=VMEM_LIMIT),
        name="expert_select",
    )(aff3)


def _ffn_kernel(idx_ref, hn_hbm, gval_ref, wg_ref, wu_ref, wd_ref, out_ref, xbuf, sem, *, tc, nt):
    base = (pl.program_id(0) * nt + pl.program_id(1)) * tc

    def issue(i, carry):
        tok = idx_ref[base + i]
        pltpu.make_async_copy(hn_hbm.at[pl.ds(tok, 1)], xbuf.at[pl.ds(i, 1)], sem).start()
        return carry

    lax.fori_loop(0, tc, issue, 0)
    pltpu.make_async_copy(hn_hbm.at[pl.ds(0, tc)], xbuf, sem).wait()
    xe = xbuf[...].astype(BF16)
    a = _dot(xe, wg_ref[0])
    u = _dot(xe, wu_ref[0])
    hmid = (a * _sigmoid(a) * u).astype(BF16)
    out_ref[...] = _dot(hmid, wd_ref[0]) * gval_ref[0]


def _expert_ffn(idx_flat, hn, gval, wg, wu, wd, cap):
    tc = min(256, cap)
    nt = cap // tc
    grid_spec = pltpu.PrefetchScalarGridSpec(
        num_scalar_prefetch=1,
        grid=(N_EXPERTS, nt),
        in_specs=[
            pl.BlockSpec(memory_space=pl.ANY),
            pl.BlockSpec((1, tc, 1), lambda e, j, idx: (e * nt + j, 0, 0)),
            pl.BlockSpec((1, D_MODEL, EXPERT_FF), lambda e, j, idx: (e, 0, 0)),
            pl.BlockSpec((1, D_MODEL, EXPERT_FF), lambda e, j, idx: (e, 0, 0)),
            pl.BlockSpec((1, EXPERT_FF, D_MODEL), lambda e, j, idx: (e, 0, 0)),
        ],
        out_specs=pl.BlockSpec((tc, D_MODEL), lambda e, j, idx: (e * nt + j, 0)),
        scratch_shapes=[pltpu.VMEM((tc, D_MODEL), F32), pltpu.SemaphoreType.DMA],
    )
    return pl.pallas_call(
        functools.partial(_ffn_kernel, tc=tc, nt=nt),
        grid_spec=grid_spec,
        out_shape=jax.ShapeDtypeStruct((N_EXPERTS * cap, D_MODEL), F32),
        compiler_params=_cparams(("arbitrary", "arbitrary")),
        name="expert_ffn",
    )(idx_flat, hn, gval.reshape(N_EXPERTS * nt, tc, 1), wg, wu, wd)


def _combine_kernel(idx_ref, p0_ref, h_ref, post_ref, g_ref, ye_hbm, out_ref, stage, sem, *, tt, cap, ntile):
    tile = pl.program_id(0)
    tok0 = tile * tt
    total = 0
    for e in range(N_EXPERTS):
        p0 = p0_ref[e * (ntile + 1) + tile]
        cnt = p0_ref[e * (ntile + 1) + tile + 1] - p0

        def issue(q, carry, e=e, p0=p0):
            row = e * cap + p0 + q
            i = idx_ref[row] - tok0
            pltpu.make_async_copy(ye_hbm.at[pl.ds(row, 1)], stage.at[pl.ds(e * tt + i, 1)], sem).start()
            return carry

        lax.fori_loop(0, cnt, issue, 0)
        total = total + cnt

    def wait_row(q, carry):
        pltpu.make_async_copy(ye_hbm.at[pl.ds(0, 1)], stage.at[pl.ds(0, 1)], sem).wait()
        return carry

    lax.fori_loop(0, total, wait_row, 0)

    acc = h_ref[...]
    post = post_ref[...]
    for e in range(N_EXPERTS):
        acc = acc + jnp.where(post[:, e:e + 1] >= 0, stage[e * tt:(e + 1) * tt, :], 0.0)
    ms = jnp.mean(acc * acc, axis=-1, keepdims=True)
    out_ref[...] = acc * lax.rsqrt(ms + NORM_EPS) * g_ref[...]


def _combine(idx_flat, p0_flat, h2d, post, g, yexp, cap):
    n = h2d.shape[0]
    tt = LANES
    ntile = n // tt
    grid_spec = pltpu.PrefetchScalarGridSpec(
        num_scalar_prefetch=2,
        grid=(ntile,),
        in_specs=[
            pl.BlockSpec((tt, D_MODEL), lambda i, a, b: (i, 0)),
            pl.BlockSpec((tt, N_EXPERTS), lambda i, a, b: (i, 0)),
            pl.BlockSpec((1, D_MODEL), lambda i, a, b: (0, 0)),
            pl.BlockSpec(memory_space=pl.ANY),
        ],
        out_specs=pl.BlockSpec((tt, D_MODEL), lambda i, a, b: (i, 0)),
        scratch_shapes=[pltpu.VMEM((N_EXPERTS * tt, D_MODEL), F32), pltpu.SemaphoreType.DMA],
    )
    return pl.pallas_call(
        functools.partial(_combine_kernel, tt=tt, cap=cap, ntile=ntile),
        grid_spec=grid_spec,
        out_shape=jax.ShapeDtypeStruct((n, D_MODEL), F32),
        compiler_params=_cparams(("arbitrary",)),
        name="moe_combine",
    )(idx_flat, p0_flat, h2d, post, g, yexp)


def _trunk(x, p, w_slab, mu_slab, rwp, moe_w, slopes):
    batch, seq, _ = x.shape
    n = batch * seq
    x2d = x.reshape(n, D_MODEL)
    slabs = _inproj(x2d, p["norm_mix_g"], w_slab)
    oatt = _attention(slabs, slopes, p["lambda_q1"], p["lambda_k1"], p["lambda_q2"], p["lambda_k2"],
                      p["subln_g"], batch, seq)
    r, v, na, g, kd, b, ld = _rwprep(slabs, mu_slab, *rwp[:7], batch, seq)
    yf, yb = _rwscan(r, v, na, kd, b, ld, rwp[7], batch, seq)
    h = _merge(x2d, oatt, yf, yb, g, slabs, p["gate_b"][0].reshape(2, 1, D_MODEL), rwp[8], rwp[9],
               moe_w["wba"], moe_w["wbr"], moe_w["wout"])

    cap = max(1, CAPACITY_FACTOR * n // N_EXPERTS)
    hn, aff = _router(h, p["norm_ffn_g"], moe_w["wr"])
    aff3 = aff.reshape(N_EXPERTS, n // LANES, LANES)
    pos3 = _select(aff3, cap)
    pos = pos3.reshape(N_EXPERTS, n)
    sel = pos >= 0
    tok = jnp.broadcast_to(jnp.arange(n, dtype=I32)[None], pos.shape)
    eid = jnp.broadcast_to(jnp.arange(N_EXPERTS, dtype=I32)[:, None], pos.shape)
    idx = jnp.zeros((N_EXPERTS, cap), I32).at[eid, jnp.where(sel, pos, cap)].set(tok, mode="drop")
    gval = jnp.take_along_axis(aff3.reshape(N_EXPERTS, n), idx, axis=1)
    ntile = n // LANES
    cnt = jnp.sum(sel.reshape(N_EXPERTS, ntile, LANES), axis=-1, dtype=I32)
    p0 = jnp.concatenate([jnp.zeros((N_EXPERTS, 1), I32), jnp.cumsum(cnt, axis=1, dtype=I32)], axis=1)
    idx_flat = idx.reshape(-1)
    yexp = _expert_ffn(idx_flat, hn, gval, moe_w["wg"], moe_w["wu"], moe_w["wd"], cap)
    y = _combine(idx_flat, p0.reshape(-1), h, pos.T, p["norm_final_g"].reshape(1, D_MODEL), yexp, cap)
    return y.reshape(batch, seq, D_MODEL)


def kernel(x_prompt, x_sample, norm_mix_g, w_in, shift_mu, lambda_q1, lambda_k1, lambda_q2, lambda_k2, subln_g, rw_w0, rw_w2, rw_a0, rw_a2, rw_g2, rw_k_k, rw_k_a, rw_r_k, lnx_g, lnx_b, gate_b, w_br_att, w_br_rw, w_out, norm_ffn_g, w_router, w_gate_e, w_up_e, w_down_e, norm_final_g):
    p = dict(norm_mix_g=norm_mix_g, lambda_q1=lambda_q1, lambda_k1=lambda_k1, lambda_q2=lambda_q2,
             lambda_k2=lambda_k2, subln_g=subln_g, rw_w0=rw_w0, rw_w2=rw_w2, rw_a0=rw_a0, rw_a2=rw_a2,
             rw_g2=rw_g2, rw_k_k=rw_k_k, rw_k_a=rw_k_a, rw_r_k=rw_r_k, lnx_g=lnx_g, lnx_b=lnx_b,
             gate_b=gate_b, norm_ffn_g=norm_ffn_g, norm_final_g=norm_final_g)
    w_slab, mu_slab = _prep_in_weights(w_in[0], shift_mu[0])
    rwp = _prep_rw_params(p)
    moe_w = dict(
        wba=w_br_att[0].astype(BF16), wbr=w_br_rw[0].astype(BF16), wout=w_out[0].astype(BF16),
        wr=jnp.pad(w_router[0], ((0, 0), (0, LANES - N_EXPERTS))),
        wg=w_gate_e[0].astype(BF16), wu=w_up_e[0].astype(BF16), wd=w_down_e[0].astype(BF16))
    slopes = jnp.asarray([2.0 ** (-8.0 * (i + 1) / ATT_HEADS) for i in range(ATT_HEADS)], F32)
    return (_trunk(x_prompt, p, w_slab, mu_slab, rwp, moe_w, slopes),
            _trunk(x_sample, p, w_slab, mu_slab, rwp, moe_w, slopes))
```

```python
import functools
import math

import jax
import jax.numpy as jnp
from jax import lax
from jax.experimental import pallas as pl
from jax.experimental.pallas import tpu as pltpu

F32 = jnp.float32
BF16 = jnp.bfloat16
I32 = jnp.int32

D_MODEL = 2048
ATT_HEADS = 8
ATT_HEAD_DIM = 64
ATT_WIDTH = ATT_HEADS * 2 * ATT_HEAD_DIM
RW_HEAD = 64
RW_WIDTH = 1024
DECAY_LORA = 96
ICLR_LORA = 96
GATE_LORA = 256
SHIFT_WIDTH = 3 * RW_WIDTH + DECAY_LORA + ICLR_LORA + GATE_LORA
N_EXPERTS = 16
CAPACITY_FACTOR = 2
EXPERT_FF = 1024
NORM_EPS = 1e-6
SUBLN_EPS = 1e-5
LNX_EPS = 64e-5
LAM_INIT = 0.8 - 0.6 * math.exp(-0.3 * 0)

LANES = 128
VMEM_LIMIT = 56 * 1024 * 1024

SL_GATE_ATT, SL_GATE_RW = 0, 16
SL_ATT_Q, SL_ATT_K, SL_ATT_V = 32, 40, 48
SL_RW = 56
N_RW_SLABS = 28
N_SLABS = 84
CHUNK = 64


def _cparams(sem):
    return pltpu.CompilerParams(dimension_semantics=sem, vmem_limit_bytes=VMEM_LIMIT)


def _sigmoid(x):
    return 1.0 / (1.0 + jnp.exp(-x))


def _split3(x):
    hi = x.astype(BF16)
    r1 = x - hi.astype(F32)
    mid = r1.astype(BF16)
    lo = (r1 - mid.astype(F32)).astype(BF16)
    return hi, mid, lo


def _dot(a, b):
    return jnp.dot(a, b, preferred_element_type=F32)


def _dot_nt(a, b):
    return lax.dot_general(a, b, (((1,), (1,)), ((), ())), preferred_element_type=F32)


def _dot_tn(a, b):
    return lax.dot_general(a, b, (((0,), (0,)), ((), ())), preferred_element_type=F32)


def _dot_f32(a_bf16_exact, x):
    hi, mid, lo = _split3(x)
    return _dot(a_bf16_exact, hi) + _dot(a_bf16_exact, mid) + _dot(a_bf16_exact, lo)


def _rearrange_in_cols(a):
    att = a[..., :3 * ATT_WIDTH]
    zr = a[..., 3 * ATT_WIDTH:3 * ATT_WIDTH + SHIFT_WIDTH]
    gates = a[..., 3 * ATT_WIDTH + SHIFT_WIDTH:]
    o3 = 3 * RW_WIDTH
    o4 = o3 + DECAY_LORA
    o5 = o4 + ICLR_LORA
    pad = [(0, 0)] * (a.ndim - 1)
    lw = jnp.pad(zr[..., o3:o4], pad + [(0, LANES - DECAY_LORA)])
    la = jnp.pad(zr[..., o4:o5], pad + [(0, LANES - ICLR_LORA)])
    return jnp.concatenate([gates, att, zr[..., :o3], lw, la, zr[..., o5:]], axis=-1)


def _prep_in_weights(w_in, shift_mu):
    w_slab = _rearrange_in_cols(w_in).astype(BF16)
    mu_full = jnp.pad(shift_mu, ((0, 0), (3 * ATT_WIDTH, 2 * D_MODEL)))
    mu_slab = _rearrange_in_cols(mu_full)[:, SL_RW * LANES:]
    return w_slab, mu_slab.reshape(2, N_RW_SLABS, 1, LANES)


def _inproj_kernel(x_ref, g_ref, w_ref, o_ref, xn_ref, *, n_out_slabs):
    @pl.when(pl.program_id(1) == 0)
    def _():
        x = x_ref[...]
        ms = jnp.mean(x * x, axis=-1, keepdims=True)
        xn_ref[...] = (x * lax.rsqrt(ms + NORM_EPS) * g_ref[...]).astype(BF16)

    acc = _dot(xn_ref[...], w_ref[...])
    for c in range(n_out_slabs):
        o_ref[c] = acc[:, c * LANES:(c + 1) * LANES].astype(BF16)


def _inproj(x2d, g, w_slab):
    n = x2d.shape[0]
    tm = min(1024, n)
    tn = 768
    n_out_slabs = tn // LANES
    grid = (n // tm, (N_SLABS * LANES) // tn)
    return pl.pallas_call(
        functools.partial(_inproj_kernel, n_out_slabs=n_out_slabs),
        grid=grid,
        in_specs=[
            pl.BlockSpec((tm, D_MODEL), lambda i, j: (i, 0)),
            pl.BlockSpec((1, D_MODEL), lambda i, j: (0, 0)),
            pl.BlockSpec((D_MODEL, tn), lambda i, j: (0, j)),
        ],
        out_specs=pl.BlockSpec((n_out_slabs, tm, LANES), lambda i, j: (j, i, 0)),
        out_shape=jax.ShapeDtypeStruct((N_SLABS, n, LANES), BF16),
        scratch_shapes=[pltpu.VMEM((tm, D_MODEL), BF16)],
        compiler_params=_cparams(("parallel", "arbitrary")),
        name="inproj",
    )(x2d, g, w_slab)


def _attn_kernel(slopes_ref, lq1_ref, lk1_ref, lq2_ref, lk2_ref, subg_ref, q_ref, k_ref, v_ref, o_ref,
                 kt1_ref, kt2_ref, vaug_ref, s1_ref, s2_ref, *, seq, tq):
    h = pl.program_id(1)
    qi = pl.program_id(2)
    slope = slopes_ref[h]
    q0 = pl.multiple_of(qi * tq, tq)

    @pl.when(qi == 0)
    def _():
        kt = k_ref[0].astype(F32).T
        row = lax.broadcasted_iota(I32, kt.shape, 0)
        kt1_ref[...] = jnp.where(row < ATT_HEAD_DIM, kt, 0.0).astype(BF16)
        kt2_ref[...] = jnp.where(row >= ATT_HEAD_DIM, kt, 0.0).astype(BF16)
        lane = lax.broadcasted_iota(I32, (seq, LANES), 1)
        vaug_ref[:, :LANES] = v_ref[0]
        vaug_ref[:, LANES:] = jnp.where(lane == 0, 1.0, 0.0).astype(BF16)

    col = lax.broadcasted_iota(I32, (16, seq), 1)
    r16 = lax.broadcasted_iota(I32, (16, seq), 0)
    jp = col - q0
    sigma = jnp.where(jp < 0, -1.0, jnp.where(jp >= tq, 1.0, 0.0)).astype(F32)
    jh = (jp >> 8).astype(F32)
    jl = (jp & 255).astype(F32)
    feat = jnp.where(r16 <= 1, sigma,
                     jnp.where(r16 == 2, -sigma * (slope * 256.0) * jh,
                               jnp.where(r16 == 3, -sigma * slope * jl, 0.0))).astype(BF16)
    kt1_ref[ATT_HEAD_DIM:ATT_HEAD_DIM + 16, :] = feat
    kt2_ref[0:16, :] = feat

    q = q_ref[0].astype(F32) * (ATT_HEAD_DIM ** -0.5)
    lane = lax.broadcasted_iota(I32, (tq, LANES), 1)
    ip = lax.broadcasted_iota(I32, (tq, LANES), 0)
    ih = slope * (ip & ~255).astype(F32)
    il = slope * (ip & 255).astype(F32)

    def query_side(fl):
        return jnp.where(fl == 0, ih, jnp.where(fl == 1, il, jnp.where(fl <= 3, 1.0, 0.0)))

    lhs1 = jnp.where(lane < ATT_HEAD_DIM, q, query_side(lane - ATT_HEAD_DIM)).astype(BF16)
    lhs2 = jnp.where(lane >= ATT_HEAD_DIM, q, query_side(lane)).astype(BF16)

    di = lax.broadcasted_iota(I32, (tq, tq), 0)
    dj = lax.broadcasted_iota(I32, (tq, tq), 1)
    diag_bias = -slope * jnp.abs(di - dj).astype(F32)

    def weights(lhs, kt_ref, s_ref):
        s_ref[...] = _dot(lhs, kt_ref[...])
        s_ref[:, pl.ds(q0, tq)] += diag_bias
        s = s_ref[...]
        m = jnp.max(s, axis=-1, keepdims=True)
        return jnp.exp(s - m).astype(BF16)

    e = jnp.concatenate([weights(lhs1, kt1_ref, s1_ref), weights(lhs2, kt2_ref, s2_ref)], axis=0)
    oa = _dot(e, vaug_ref[...])
    o1 = oa[:tq, :LANES] / oa[:tq, LANES:LANES + 1]
    o2 = oa[tq:, :LANES] / oa[tq:, LANES:LANES + 1]
    lam = (jnp.exp(jnp.sum(lq1_ref[...] * lk1_ref[...], keepdims=True))
           - jnp.exp(jnp.sum(lq2_ref[...] * lk2_ref[...], keepdims=True)) + LAM_INIT)
    out = o1 - lam * o2
    ms = jnp.mean(out * out, axis=-1, keepdims=True)
    y = out * lax.rsqrt(ms + SUBLN_EPS) * subg_ref[...]
    o_ref[0] = (y * (1.0 - LAM_INIT)).astype(BF16)


def _attention(slabs, slopes, lq1, lk1, lq2, lk2, subg, batch, seq):
    n = batch * seq
    tq = 512
    nq = seq // tq
    vec = lambda: pl.BlockSpec((1, ATT_HEAD_DIM), lambda b, h, i: (0, 0))
    return pl.pallas_call(
        functools.partial(_attn_kernel, seq=seq, tq=tq),
        grid=(batch, ATT_HEADS, nq),
        in_specs=[
            pl.BlockSpec(memory_space=pltpu.SMEM),
            vec(), vec(), vec(), vec(),
            pl.BlockSpec((1, LANES), lambda b, h, i: (0, 0)),
            pl.BlockSpec((1, tq, LANES), lambda b, h, i: (SL_ATT_Q + h, b * nq + i, 0)),
            pl.BlockSpec((1, seq, LANES), lambda b, h, i: (SL_ATT_K + h, b, 0)),
            pl.BlockSpec((1, seq, LANES), lambda b, h, i: (SL_ATT_V + h, b, 0)),
        ],
        out_specs=pl.BlockSpec((1, tq, LANES), lambda b, h, i: (h, b * nq + i, 0)),
        out_shape=jax.ShapeDtypeStruct((ATT_HEADS, n, LANES), BF16),
        scratch_shapes=[
            pltpu.VMEM((LANES, seq), BF16),
            pltpu.VMEM((LANES, seq), BF16),
            pltpu.VMEM((seq, 2 * LANES), BF16),
            pltpu.VMEM((tq, seq), F32),
            pltpu.VMEM((tq, seq), F32),
        ],
        compiler_params=_cparams(("parallel", "parallel", "arbitrary")),
        name="diff_attn",
    )(slopes, lq1, lk1, lq2, lk2, subg, slabs, slabs, slabs)


def _head_segsum(x, bd):
    hi, mid, lo = _split3(x)
    return _dot(hi, bd) + _dot(mid, bd) + _dot(lo, bd)


def _block_ones():
    ri = lax.broadcasted_iota(I32, (LANES, LANES), 0)
    ci = lax.broadcasted_iota(I32, (LANES, LANES), 1)
    return jnp.where((ri >> 6) == (ci >> 6), 1.0, 0.0).astype(BF16)


def _softplus(x):
    return jnp.maximum(x, 0.0) + jnp.log1p(jnp.exp(-jnp.abs(x)))


def _rwprep_kernel(main_ref, prev_ref, next_ref, mu_ref, w0_ref, a0_ref, kk_ref, ka_ref, w2_ref, a2_ref,
                   g2_ref, r_ref, v_ref, na_ref, g_ref, kd_ref, b_ref, ld_ref, *, nt):
    i = pl.program_id(1)
    z = main_ref[...].astype(F32)
    t = z.shape[1]
    prev_row = jnp.where(i > 0, prev_ref[...].astype(F32)[:, 15:16, :], 0.0)
    next_row = jnp.where(i < nt - 1, next_ref[...].astype(F32)[:, 0:1, :], 0.0)
    row = lax.broadcasted_iota(I32, z.shape, 1)
    zp = jnp.where(row == 0, prev_row, pltpu.roll(z, 1, 1))
    zn = jnp.where(row == t - 1, next_row, pltpu.roll(z, t - 1, 1))
    z = z + mu_ref[0] * (zp - z) + mu_ref[1] * (zn - z)

    xw = jnp.tanh(z[24]).astype(BF16)
    xa = z[25].astype(BF16)
    xg = _sigmoid(jnp.concatenate([z[26], z[27]], axis=1)).astype(BF16)
    g_full = _dot(xg, g2_ref[...])
    lw = [_dot(xw, w2_ref[d]) for d in range(2)]
    la = [_dot(xa, a2_ref[d]) for d in range(2)]
    bd = _block_ones()
    for c in range(8):
        cs = slice(c * LANES, (c + 1) * LANES)
        kc = z[8 + c]
        kk = kc * kk_ref[c]
        nrm = jnp.sqrt(_head_segsum(kk * kk, bd))
        kk = kk / jnp.maximum(nrm, 1e-12)
        r_ref[c] = z[c]
        v_ref[c] = z[16 + c]
        na_ref[c] = -kk
        g_ref[c] = g_full[:, cs]
        for d in range(2):
            wl = -_softplus(-(w0_ref[d, c] + lw[d][:, cs])) - 0.5
            ld_ref[d, c] = -jnp.exp(wl)
            asig = _sigmoid(a0_ref[d, c] + la[d][:, cs])
            kd_ref[d, c] = kc * (1.0 + (asig - 1.0) * ka_ref[c])
            b_ref[d, c] = kk * asig


def _prep_rw_params(p):
    vec = lambda a: a.reshape(a.shape[:-1] + (8, 1, LANES))
    pad_rows = lambda a: jnp.pad(a, ((0, 0), (0, LANES - a.shape[1]), (0, 0))).astype(BF16)
    return (vec(p["rw_w0"][0]), vec(p["rw_a0"][0]), vec(p["rw_k_k"][0]), vec(p["rw_k_a"][0]),
            pad_rows(p["rw_w2"][0]), pad_rows(p["rw_a2"][0]), p["rw_g2"][0].astype(BF16),
            vec(p["rw_r_k"][0].reshape(RW_WIDTH)), vec(p["lnx_g"][0]), vec(p["lnx_b"][0]))


def _rwprep(slabs, mu_slab, w0, a0, k_k, k_a, w2, a2, g2, batch, seq):
    n = batch * seq
    t = 256
    nt = seq // t
    hb = 16
    full = lambda shape: pl.BlockSpec(shape, lambda b, i: (0,) * len(shape))
    rows = lambda b, i: b * nt + i
    o8 = pl.BlockSpec((8, t, LANES), lambda b, i: (0, rows(b, i), 0))
    o28 = pl.BlockSpec((2, 8, t, LANES), lambda b, i: (0, 0, rows(b, i), 0))
    s8 = jax.ShapeDtypeStruct((8, n, LANES), F32)
    s28 = jax.ShapeDtypeStruct((2, 8, n, LANES), F32)
    rw_blk = SL_RW // N_RW_SLABS
    return pl.pallas_call(
        functools.partial(_rwprep_kernel, nt=nt),
        grid=(batch, nt),
        in_specs=[
            pl.BlockSpec((N_RW_SLABS, t, LANES), lambda b, i: (rw_blk, rows(b, i), 0)),
            pl.BlockSpec((N_RW_SLABS, hb, LANES),
                         lambda b, i: (rw_blk, jnp.maximum((b * seq + i * t) // hb - 1, 0), 0)),
            pl.BlockSpec((N_RW_SLABS, hb, LANES),
                         lambda b, i: (rw_blk, jnp.minimum((b * seq + (i + 1) * t) // hb, n // hb - 1), 0)),
            full((2, N_RW_SLABS, 1, LANES)),
            full((2, 8, 1, LANES)), full((2, 8, 1, LANES)), full((8, 1, LANES)), full((8, 1, LANES)),
            full((2, LANES, RW_WIDTH)), full((2, LANES, RW_WIDTH)), full((GATE_LORA, RW_WIDTH)),
        ],
        out_specs=[o8, o8, o8, o8, o28, o28, o28],
        out_shape=[s8, s8, s8, s8, s28, s28, s28],
        compiler_params=_cparams(("parallel", "parallel")),
        name="rwkv_prep",
    )(slabs, slabs, slabs, mu_slab, w0, a0, k_k, k_a, w2, a2, g2)


def _rwscan_kernel(rf_ref, vf_ref, naf_ref, kdf_ref, bf_ref, ldf_ref,
                   rb_ref, vb_ref, nab_ref, kdb_ref, bb_ref, ldb_ref, rk_ref,
                   yf_ref, yb_ref, st_ref, *, nc):
    @pl.when(pl.program_id(2) == 0)
    def _():
        st_ref[...] = jnp.zeros_like(st_ref)

    lane = lax.broadcasted_iota(I32, (CHUNK, LANES), 1)
    head0 = lane < RW_HEAD
    ri = lax.broadcasted_iota(I32, (LANES, LANES), 0)
    ci = lax.broadcasted_iota(I32, (LANES, LANES), 1)
    same = (ri >> 6) == (ci >> 6)
    tt = ri & (CHUNK - 1)
    ss = ci & (CHUNK - 1)
    eye = jnp.where(ri == ci, 1.0, 0.0).astype(F32)
    tr = lax.broadcasted_iota(I32, (CHUNK, CHUNK), 0)
    tc = lax.broadcasted_iota(I32, (CHUNK, CHUNK), 1)
    bd = _block_ones()
    rk = rk_ref[0]

    def stack(x):
        return jnp.concatenate([jnp.where(head0, x, 0.0), jnp.where(head0, 0.0, x)], axis=0)

    dirs = (
        (rf_ref, vf_ref, naf_ref, kdf_ref, bf_ref, ldf_ref, yf_ref),
        (rb_ref, vb_ref, nab_ref, kdb_ref, bb_ref, ldb_ref, yb_ref),
    )
    strict = (same & (ss < tt), same & (ss > tt))
    incl = (same & (ss <= tt), same & (ss >= tt))
    tri = (jnp.where(tc <= tr, 1.0, 0.0).astype(BF16), jnp.where(tc >= tr, 1.0, 0.0).astype(BF16))
    last = (CHUNK - 1, 0)
    insts = [(d, k if d == 0 else nc - 1 - k) for k in range(nc) for d in range(2)]
    every = range(len(insts))

    def load(ref, d, ch, lead):
        sl = slice(ch * CHUNK, (ch + 1) * CHUNK)
        return ref[(0,) * lead + (sl, slice(None))]

    r = [load(dirs[d][0], d, ch, 1) for d, ch in insts]
    v = [load(dirs[d][1], d, ch, 1) for d, ch in insts]
    na = [load(dirs[d][2], d, ch, 1) for d, ch in insts]
    kd = [load(dirs[d][3], d, ch, 2) for d, ch in insts]
    b = [load(dirs[d][4], d, ch, 2) for d, ch in insts]
    ld = [load(dirs[d][5], d, ch, 2) for d, ch in insts]
    c = [_dot_f32(tri[insts[i][0]], ld[i]) for i in every]
    total = [c[i][last[insts[i][0]]:last[insts[i][0]] + 1] for i in every]
    e_nc = [jnp.exp(-c[i]) for i in every]
    e_tc = [jnp.exp(total[i] - c[i]) for i in every]
    a_t = [stack(na[i] * jnp.exp(c[i] - ld[i])).astype(BF16) for i in every]
    r_t = [stack(r[i] * jnp.exp(c[i])).astype(BF16) for i in every]
    v_s = [stack(v[i]).astype(BF16) for i in every]
    rhs = [jnp.concatenate([stack(b[i] * e_nc[i]), stack(kd[i] * e_nc[i])], axis=0).astype(BF16) for i in every]
    bk = [jnp.concatenate([stack(b[i] * e_tc[i]), stack(kd[i] * e_tc[i])], axis=0).astype(BF16) for i in every]
    p = [_dot_nt(jnp.concatenate([a_t[i], r_t[i]], axis=0), rhs[i]) for i in every]
    n_ab = [jnp.where(strict[insts[i][0]], p[i][:LANES, :LANES], 0.0) for i in every]
    a_ak = [jnp.where(strict[insts[i][0]], p[i][:LANES, LANES:], 0.0).astype(BF16) for i in every]
    p_rb = [jnp.where(incl[insts[i][0]], p[i][LANES:, :LANES], 0.0).astype(BF16) for i in every]
    p_rk = [jnp.where(incl[insts[i][0]], p[i][LANES:, LANES:], 0.0).astype(BF16) for i in every]
    x = [eye + n_ab[i] for i in every]
    nk = n_ab
    for _ in range(5):
        nkb = [nk[i].astype(BF16) for i in every]
        nk = [_dot(nkb[i], nkb[i]) for i in every]
        x = [x[i] + _dot(x[i].astype(BF16), nk[i].astype(BF16)) for i in every]
    w = [_dot(a_ak[i], v_s[i]) for i in every]
    au = [_dot(x[i].astype(BF16), jnp.concatenate([a_t[i], w[i].astype(BF16)], axis=1)) for i in every]
    y0 = [_dot(p_rk[i], v_s[i]) for i in every]
    bonus = [_head_segsum(r[i] * kd[i] * rk, bd) * v[i] for i in every]
    st = [st_ref[0], st_ref[1]]
    for i, (d, ch) in enumerate(insts):
        uy = _dot_nt(jnp.concatenate([au[i][:, :LANES].astype(BF16), r_t[i]], axis=0), st[d].astype(BF16))
        u = (uy[:LANES] + au[i][:, LANES:]).astype(BF16)
        y = uy[LANES:] + _dot(p_rb[i], u) + y0[i]
        st[d] = st[d] * jnp.exp(total[i]) + _dot_tn(jnp.concatenate([u, v_s[i]], axis=0), bk[i])
        dirs[d][6][0, ch * CHUNK:(ch + 1) * CHUNK, :] = y[:CHUNK] + y[CHUNK:] + bonus[i]
    st_ref[0] = st[0]
    st_ref[1] = st[1]


def _rwscan(r, v, na, kd, b, ld, r_k, batch, seq):
    n = batch * seq
    nc = min(8, seq // CHUNK)
    tcs = CHUNK * nc
    nt = seq // tcs
    fwd = lambda bi, c, t: bi * nt + t
    bwd = lambda bi, c, t: bi * nt + nt - 1 - t
    s3 = lambda rows: pl.BlockSpec((1, tcs, LANES), lambda bi, c, t: (c, rows(bi, c, t), 0))
    s4 = lambda d, rows: pl.BlockSpec((1, 1, tcs, LANES), lambda bi, c, t: (d, c, rows(bi, c, t), 0))
    out = jax.ShapeDtypeStruct((8, n, LANES), F32)
    return pl.pallas_call(
        functools.partial(_rwscan_kernel, nc=nc),
        grid=(batch, 8, nt),
        in_specs=[s3(fwd), s3(fwd), s3(fwd), s4(0, fwd), s4(0, fwd), s4(0, fwd),
                  s3(bwd), s3(bwd), s3(bwd), s4(1, bwd), s4(1, bwd), s4(1, bwd),
                  pl.BlockSpec((1, 1, LANES), lambda bi, c, t: (c, 0, 0))],
        out_specs=[s3(fwd), s3(bwd)],
        out_shape=[out, out],
        scratch_shapes=[pltpu.VMEM((2, LANES, LANES), F32)],
        compiler_params=_cparams(("parallel", "parallel", "arbitrary")),
        name="rwkv_scan",
    )(r, v, na, kd, b, ld, r, v, na, kd, b, ld, r_k)


def _merge_kernel(x_ref, oatt_ref, yf_ref, yb_ref, g_ref, gates_ref, gb_ref, lng_ref, lnb_ref,
                  wba_ref, wbr_ref, wout_ref, h_ref):
    bd = _block_ones()
    orw = []
    for c in range(8):
        y = yf_ref[c] + yb_ref[c]
        mu = _head_segsum(y, bd) * (1.0 / RW_HEAD)
        yc = y - mu
        var = _head_segsum(yc * yc, bd) * (1.0 / RW_HEAD)
        yn = yc * lax.rsqrt(var + LNX_EPS) * lng_ref[c] + lnb_ref[c]
        orw.append((yn * g_ref[c]).astype(BF16))
    orw = jnp.concatenate(orw, axis=1)
    oatt = jnp.concatenate([oatt_ref[c] for c in range(8)], axis=1)
    ga = jnp.concatenate([gates_ref[c] for c in range(16)], axis=1).astype(F32) + gb_ref[0]
    gr = jnp.concatenate([gates_ref[16 + c] for c in range(16)], axis=1).astype(F32) + gb_ref[1]
    merged = _sigmoid(ga) * _dot(oatt, wba_ref[...]) + _sigmoid(gr) * _dot(orw, wbr_ref[...])
    h_ref[...] = x_ref[...] + _dot(merged.astype(BF16), wout_ref[...])


def _merge(x2d, oatt, yf, yb, g, slabs, gate_b, lng, lnb, wba, wbr, wout):
    n = x2d.shape[0]
    tm = min(256, n)
    const = lambda shape: pl.BlockSpec(shape, lambda i: (0,) * len(shape), pipeline_mode=pl.Buffered(1))
    s8 = pl.BlockSpec((8, tm, LANES), lambda i: (0, i, 0))
    return pl.pallas_call(
        _merge_kernel,
        grid=(n // tm,),
        in_specs=[
            pl.BlockSpec((tm, D_MODEL), lambda i: (i, 0)),
            s8, s8, s8, s8,
            pl.BlockSpec((32, tm, LANES), lambda i: (0, i, 0)),
            const((2, 1, D_MODEL)), const((8, 1, LANES)), const((8, 1, LANES)),
            const((ATT_WIDTH, D_MODEL)), const((RW_WIDTH, D_MODEL)), const((D_MODEL, D_MODEL)),
        ],
        out_specs=pl.BlockSpec((tm, D_MODEL), lambda i: (i, 0)),
        out_shape=jax.ShapeDtypeStruct((n, D_MODEL), F32),
        compiler_params=_cparams(("parallel",)),
        name="merge_outproj",
    )(x2d, oatt, yf, yb, g, slabs, gate_b, lng, lnb, wba, wbr, wout)


def _router_kernel(h_ref, g_ref, wr_ref, hn_ref, aff_ref):
    x = h_ref[...]
    ms = jnp.mean(x * x, axis=-1, keepdims=True)
    hn = x * lax.rsqrt(ms + NORM_EPS) * g_ref[...]
    hn_ref[...] = hn
    xh, xm, xl = _split3(hn)
    wh, wm, wl = _split3(wr_ref[...])
    logits = (_dot(xh, wh) + _dot(xh, wm) + _dot(xm, wh)
              + _dot(xh, wl) + _dot(xl, wh) + _dot(xm, wm))
    lt = logits.T[:N_EXPERTS]
    m = jnp.max(lt, axis=0, keepdims=True)
    e = jnp.exp(lt - m)
    aff_ref[...] = e / jnp.sum(e, axis=0, keepdims=True)


def _router(h2d, g, wr_pad):
    n = h2d.shape[0]
    tm = min(256, n)
    return pl.pallas_call(
        _router_kernel,
        grid=(n // tm,),
        in_specs=[
            pl.BlockSpec((tm, D_MODEL), lambda i: (i, 0)),
            pl.BlockSpec((1, D_MODEL), lambda i: (0, 0)),
            pl.BlockSpec((D_MODEL, LANES), lambda i: (0, 0)),
        ],
        out_specs=[
            pl.BlockSpec((tm, D_MODEL), lambda i: (i, 0)),
            pl.BlockSpec((N_EXPERTS, tm), lambda i: (0, i)),
        ],
        out_shape=[
            jax.ShapeDtypeStruct((n, D_MODEL), F32),
            jax.ShapeDtypeStruct((N_EXPERTS, n), F32),
        ],
        compiler_params=_cparams(("parallel",)),
        name="router",
    )(h2d, g, wr_pad)


def _select_kernel(aff_ref, incl_ref, tbl_ref, gval_ref, cnt_ref, slot_ref, *, cap):
    bits = pltpu.bitcast(aff_ref[...], I32)
    nrow = bits.shape[1]

    def count(mask):
        c = jnp.sum(jnp.where(mask, 1, 0), axis=2, keepdims=True)
        return jnp.sum(c, axis=1, keepdims=True)

    def body(_, carry):
        lo, hi = carry
        mid = lo + ((hi - lo) >> 1)
        ok = count(bits >= mid) >= cap
        return jnp.where(ok, mid, lo), jnp.where(ok, hi, mid)

    lo0 = jnp.zeros((N_EXPERTS, 1, 1), I32)
    hi0 = jnp.full((N_EXPERTS, 1, 1), 0x7F800000, I32)
    thr, _ = lax.fori_loop(0, 31, body, (lo0, hi0))
    gt = bits > thr
    eq = bits == thr
    need = cap - count(gt)

    ri = lax.broadcasted_iota(I32, (LANES, LANES), 0)
    ci = lax.broadcasted_iota(I32, (LANES, LANES), 1)
    upper = jnp.where(ri <= ci, 1.0, 0.0).astype(BF16)
    rr = lax.broadcasted_iota(I32, (nrow, nrow), 0)
    rc = lax.broadcasted_iota(I32, (nrow, nrow), 1)
    lower_strict = jnp.where(rc < rr, 1.0, 0.0).astype(BF16)

    def incl_prefix(mask):
        x = jnp.where(mask, 1.0, 0.0).astype(BF16)
        incl = _dot(x.reshape(N_EXPERTS * nrow, LANES), upper).reshape(N_EXPERTS, nrow, LANES)
        tot = jnp.broadcast_to(incl[:, :, LANES - 1:LANES], incl.shape).astype(BF16)
        before = jnp.stack([_dot(lower_strict, tot[e]) for e in range(N_EXPERTS)], axis=0)
        return incl + before

    sel = gt | (eq & (incl_prefix(eq) - 1.0 < need.astype(F32)))
    incl_ref[...] = incl_prefix(sel).astype(I32)
    run = jnp.zeros((nrow, LANES), F32)
    for e in range(N_EXPERTS):
        slot_ref[e] = run
        run = run + jnp.where(sel[e], 1.0, 0.0)
    cnt_ref[...] = run.astype(I32)

    pf = lax.broadcasted_iota(I32, (1, cap), 1).astype(F32)
    jrow = lax.broadcasted_iota(I32, (nrow, cap), 0).astype(F32)
    lrow = lax.broadcasted_iota(I32, (LANES, cap), 0).astype(F32)

    def compact(e, carry):
        g = incl_ref[e].astype(F32)
        jsel = jnp.sum(jnp.where(g[:, LANES - 1:LANES] <= pf, 1.0, 0.0), axis=0, keepdims=True)
        onehot = jnp.where(jrow == jsel, 1.0, 0.0).astype(BF16)
        ghi = jnp.floor(g * (1.0 / 256.0))
        glo = g - 256.0 * ghi
        grow = 256.0 * _dot_tn(ghi.astype(BF16), onehot) + _dot_tn(glo.astype(BF16), onehot)
        lstar = jnp.sum(jnp.where(grow <= pf, 1.0, 0.0), axis=0, keepdims=True)
        lsel = lrow == lstar
        ah, am, al = _split3(aff_ref[e])
        arow = _dot_tn(ah, onehot) + _dot_tn(am, onehot) + _dot_tn(al, onehot)
        gval_ref[pl.ds(e, 1), :] = jnp.sum(jnp.where(lsel, arow, 0.0), axis=0, keepdims=True)
        krow = _dot_tn(slot_ref[e].astype(BF16), onehot)
        kk = jnp.sum(jnp.where(lsel, krow, 0.0), axis=0, keepdims=True)
        tbl_ref[pl.ds(e, 1), :] = (jsel * float(LANES) + lstar).astype(I32) | (kk.astype(I32) << 16)
        return carry

    lax.fori_loop(0, N_EXPERTS, compact, 0)


def _select(aff3, cap):
    nrow = aff3.shape[1]
    return pl.pallas_call(
        functools.partial(_select_kernel, cap=cap),
        out_shape=[
            jax.ShapeDtypeStruct(aff3.shape, I32),
            jax.ShapeDtypeStruct((N_EXPERTS, cap), I32),
            jax.ShapeDtypeStruct((N_EXPERTS, cap), F32),
            jax.ShapeDtypeStruct((nrow, LANES), I32),
        ],
        scratch_shapes=[pltpu.VMEM((N_EXPERTS, nrow, LANES), F32)],
        compiler_params=pltpu.CompilerParams(vmem_limit_bytes=VMEM_LIMIT),
        name="expert_select",
    )(aff3)


def _ffn_kernel(tbl_ref, hn_hbm, gval_ref, wg_ref, wu_ref, wd_ref, out_ref, xbuf, sem, *, tc, nt):
    step = pl.program_id(0) * nt + pl.program_id(1)
    slot = step % 2

    def issue(stp, slt):
        def body(i, carry):
            tok = tbl_ref[stp * tc + i] & 0xFFFF
            pltpu.make_async_copy(hn_hbm.at[pl.ds(tok, 1)], xbuf.at[slt, pl.ds(i, 1)], sem.at[slt]).start()
            return carry
        lax.fori_loop(0, tc, body, 0, unroll=8)

    @pl.when(step == 0)
    def _():
        issue(0, 0)

    @pl.when(step + 1 < N_EXPERTS * nt)
    def _():
        issue(step + 1, 1 - slot)

    pltpu.make_async_copy(hn_hbm.at[pl.ds(0, tc)], xbuf.at[slot], sem.at[slot]).wait()
    xe = xbuf[slot].astype(BF16)
    a = _dot(xe, wg_ref[0])
    u = _dot(xe, wu_ref[0])
    hmid = (a * _sigmoid(a) * u).astype(BF16)
    out_ref[...] = _dot(hmid, wd_ref[0]) * gval_ref[0]


def _expert_ffn(tbl_flat, hn, gval, wg, wu, wd, cap):
    tc = min(256, cap)
    nt = cap // tc
    grid_spec = pltpu.PrefetchScalarGridSpec(
        num_scalar_prefetch=1,
        grid=(N_EXPERTS, nt),
        in_specs=[
            pl.BlockSpec(memory_space=pl.ANY),
            pl.BlockSpec((1, tc, 1), lambda e, j, idx: (e * nt + j, 0, 0)),
            pl.BlockSpec((1, D_MODEL, EXPERT_FF), lambda e, j, idx: (e, 0, 0)),
            pl.BlockSpec((1, D_MODEL, EXPERT_FF), lambda e, j, idx: (e, 0, 0)),
            pl.BlockSpec((1, EXPERT_FF, D_MODEL), lambda e, j, idx: (e, 0, 0)),
        ],
        out_specs=pl.BlockSpec((tc, D_MODEL), lambda e, j, idx: (e * nt + j, 0)),
        scratch_shapes=[pltpu.VMEM((2, tc, D_MODEL), F32), pltpu.SemaphoreType.DMA((2,))],
    )
    return pl.pallas_call(
        functools.partial(_ffn_kernel, tc=tc, nt=nt),
        grid_spec=grid_spec,
        out_shape=jax.ShapeDtypeStruct((N_EXPERTS * cap, D_MODEL), F32),
        compiler_params=_cparams(("arbitrary", "arbitrary")),
        name="expert_ffn",
    )(tbl_flat, hn, gval.reshape(N_EXPERTS * nt, tc, 1), wg, wu, wd)


def _combine_kernel(tbl_ref, p0_ref, km_ref, h_ref, cnt_ref, g_ref, ye_hbm, out_ref, stage, sem,
                    *, tt, cap, ntile):
    tile = pl.program_id(0)
    slot = tile % 2

    def issue(tl, slt):
        for e in range(N_EXPERTS):
            p0 = p0_ref[e * (ntile + 1) + tl]
            cnt = p0_ref[e * (ntile + 1) + tl + 1] - p0

            def body(q, carry, e=e, p0=p0):
                row = e * cap + p0 + q
                packed = tbl_ref[row]
                i = (packed & 0xFFFF) - tl * tt
                pltpu.make_async_copy(ye_hbm.at[pl.ds(row, 1)], stage.at[slt, packed >> 16, pl.ds(i, 1)],
                                      sem.at[slt]).start()
                return carry

            lax.fori_loop(0, cnt, body, 0)

    @pl.when(tile == 0)
    def _():
        issue(0, 0)

    @pl.when(tile + 1 < ntile)
    def _():
        issue(tile + 1, 1 - slot)

    def wait_row(q, carry):
        pltpu.make_async_copy(ye_hbm.at[pl.ds(0, 1)], stage.at[slot, 0, pl.ds(0, 1)], sem.at[slot]).wait()
        return carry

    lax.fori_loop(0, km_ref[2 * tile + 1], wait_row, 0)

    out_ref[...] = h_ref[...]
    cnt = cnt_ref[...]

    def add(k, carry):
        out_ref[...] += jnp.where(cnt > k, stage[slot, k], 0.0)
        return carry

    lax.fori_loop(0, km_ref[2 * tile], add, 0)
    acc = out_ref[...]
    ms = jnp.mean(acc * acc, axis=-1, keepdims=True)
    out_ref[...] = acc * lax.rsqrt(ms + NORM_EPS) * g_ref[...]


def _combine(tbl_flat, p0_flat, km_flat, h2d, cnt_tok, g, yexp, cap, tt):
    n = h2d.shape[0]
    ntile = n // tt
    grid_spec = pltpu.PrefetchScalarGridSpec(
        num_scalar_prefetch=3,
        grid=(ntile,),
        in_specs=[
            pl.BlockSpec((tt, D_MODEL), lambda i, a, b, c: (i, 0)),
            pl.BlockSpec((tt, 1), lambda i, a, b, c: (i, 0)),
            pl.BlockSpec((1, D_MODEL), lambda i, a, b, c: (0, 0)),
            pl.BlockSpec(memory_space=pl.ANY),
        ],
        out_specs=pl.BlockSpec((tt, D_MODEL), lambda i, a, b, c: (i, 0)),
        scratch_shapes=[pltpu.VMEM((2, N_EXPERTS, tt, D_MODEL), F32), pltpu.SemaphoreType.DMA((2,))],
    )
    return pl.pallas_call(
        functools.partial(_combine_kernel, tt=tt, cap=cap, ntile=ntile),
        grid_spec=grid_spec,
        out_shape=jax.ShapeDtypeStruct((n, D_MODEL), F32),
        compiler_params=_cparams(("arbitrary",)),
        name="moe_combine",
    )(tbl_flat, p0_flat, km_flat, h2d, cnt_tok, g, yexp)


def _trunk(x, p, w_slab, mu_slab, rwp, moe_w, slopes):
    batch, seq, _ = x.shape
    n = batch * seq
    x2d = x.reshape(n, D_MODEL)
    slabs = _inproj(x2d, p["norm_mix_g"], w_slab)
    oatt = _attention(slabs, slopes, p["lambda_q1"], p["lambda_k1"], p["lambda_q2"], p["lambda_k2"],
                      p["subln_g"], batch, seq)
    r, v, na, g, kd, b, ld = _rwprep(slabs, mu_slab, *rwp[:7], batch, seq)
    yf, yb = _rwscan(r, v, na, kd, b, ld, rwp[7], batch, seq)
    h = _merge(x2d, oatt, yf, yb, g, slabs, p["gate_b"][0].reshape(2, 1, D_MODEL), rwp[8], rwp[9],
               moe_w["wba"], moe_w["wbr"], moe_w["wout"])

    cap = max(1, CAPACITY_FACTOR * n // N_EXPERTS)
    hn, aff = _router(h, p["norm_ffn_g"], moe_w["wr"])
    aff3 = aff.reshape(N_EXPERTS, n // LANES, LANES)
    assert n <= 1 << 16
    incl, tbl, gval, cnt = _select(aff3, cap)
    tt = 64
    ntile = n // tt
    ends = incl.reshape(N_EXPERTS, n)[:, tt - 1::tt]
    p0 = jnp.concatenate([jnp.zeros((N_EXPERTS, 1), I32), ends], axis=1)
    ct = cnt.reshape(ntile, tt)
    km = jnp.stack([jnp.max(ct, axis=1), jnp.sum(ct, axis=1)], axis=1)
    tbl_flat = tbl.reshape(-1)
    yexp = _expert_ffn(tbl_flat, hn, gval, moe_w["wg"], moe_w["wu"], moe_w["wd"], cap)
    y = _combine(tbl_flat, p0.reshape(-1), km.reshape(-1), h, cnt.reshape(n, 1),
                 p["norm_final_g"].reshape(1, D_MODEL), yexp, cap, tt)
    return y.reshape(batch, seq, D_MODEL)


def kernel(x_prompt, x_sample, norm_mix_g, w_in, shift_mu, lambda_q1, lambda_k1, lambda_q2, lambda_k2, subln_g, rw_w0, rw_w2, rw_a0, rw_a2, rw_g2, rw_k_k, rw_k_a, rw_r_k, lnx_g, lnx_b, gate_b, w_br_att, w_br_rw, w_out, norm_ffn_g, w_router, w_gate_e, w_up_e, w_down_e, norm_final_g):
    p = dict(norm_mix_g=norm_mix_g, lambda_q1=lambda_q1, lambda_k1=lambda_k1, lambda_q2=lambda_q2,
             lambda_k2=lambda_k2, subln_g=subln_g, rw_w0=rw_w0, rw_w2=rw_w2, rw_a0=rw_a0, rw_a2=rw_a2,
             rw_g2=rw_g2, rw_k_k=rw_k_k, rw_k_a=rw_k_a, rw_r_k=rw_r_k, lnx_g=lnx_g, lnx_b=lnx_b,
             gate_b=gate_b, norm_ffn_g=norm_ffn_g, norm_final_g=norm_final_g)
    w_slab, mu_slab = _prep_in_weights(w_in[0], shift_mu[0])
    rwp = _prep_rw_params(p)
    moe_w = dict(
        wba=w_br_att[0].astype(BF16), wbr=w_br_rw[0].astype(BF16), wout=w_out[0].astype(BF16),
        wr=jnp.pad(w_router[0], ((0, 0), (0, LANES - N_EXPERTS))),
        wg=w_gate_e[0].astype(BF16), wu=w_up_e[0].astype(BF16), wd=w_down_e[0].astype(BF16))
    slopes = jnp.asarray([2.0 ** (-8.0 * (i + 1) / ATT_HEADS) for i in range(ATT_HEADS)], F32)
    return (_trunk(x_prompt, p, w_slab, mu_slab, rwp, moe_w, slopes),
            _trunk(x_sample, p, w_slab, mu_slab, rwp, moe_w, slopes))
```

```python
import functools
import math

import jax
import jax.numpy as jnp
from jax import lax
from jax.experimental import pallas as pl
from jax.experimental.pallas import tpu as pltpu

F32 = jnp.float32
BF16 = jnp.bfloat16
I32 = jnp.int32

D_MODEL = 2048
ATT_HEADS = 8
ATT_HEAD_DIM = 64
ATT_WIDTH = ATT_HEADS * 2 * ATT_HEAD_DIM
RW_HEAD = 64
RW_WIDTH = 1024
DECAY_LORA = 96
ICLR_LORA = 96
GATE_LORA = 256
SHIFT_WIDTH = 3 * RW_WIDTH + DECAY_LORA + ICLR_LORA + GATE_LORA
N_EXPERTS = 16
CAPACITY_FACTOR = 2
EXPERT_FF = 1024
NORM_EPS = 1e-6
SUBLN_EPS = 1e-5
LNX_EPS = 64e-5
LAM_INIT = 0.8 - 0.6 * math.exp(-0.3 * 0)

LANES = 128
VMEM_LIMIT = 56 * 1024 * 1024

SL_GATE_ATT, SL_GATE_RW = 0, 16
SL_ATT_Q, SL_ATT_K, SL_ATT_V = 32, 40, 48
SL_RW = 56
N_RW_SLABS = 28
N_SLABS = 84
CHUNK = 64


def _cparams(sem):
    return pltpu.CompilerParams(dimension_semantics=sem, vmem_limit_bytes=VMEM_LIMIT)


def _sigmoid(x):
    return 1.0 / (1.0 + jnp.exp(-x))


def _split3(x):
    hi = x.astype(BF16)
    r1 = x - hi.astype(F32)
    mid = r1.astype(BF16)
    lo = (r1 - mid.astype(F32)).astype(BF16)
    return hi, mid, lo


def _dot(a, b):
    return jnp.dot(a, b, preferred_element_type=F32)


def _dot_nt(a, b):
    return lax.dot_general(a, b, (((1,), (1,)), ((), ())), preferred_element_type=F32)


def _dot_tn(a, b):
    return lax.dot_general(a, b, (((0,), (0,)), ((), ())), preferred_element_type=F32)


def _dot_f32(a_bf16_exact, x):
    hi, mid, lo = _split3(x)
    return _dot(a_bf16_exact, hi) + _dot(a_bf16_exact, mid) + _dot(a_bf16_exact, lo)


def _rearrange_in_cols(a):
    att = a[..., :3 * ATT_WIDTH]
    zr = a[..., 3 * ATT_WIDTH:3 * ATT_WIDTH + SHIFT_WIDTH]
    gates = a[..., 3 * ATT_WIDTH + SHIFT_WIDTH:]
    o3 = 3 * RW_WIDTH
    o4 = o3 + DECAY_LORA
    o5 = o4 + ICLR_LORA
    pad = [(0, 0)] * (a.ndim - 1)
    lw = jnp.pad(zr[..., o3:o4], pad + [(0, LANES - DECAY_LORA)])
    la = jnp.pad(zr[..., o4:o5], pad + [(0, LANES - ICLR_LORA)])
    return jnp.concatenate([gates, att, zr[..., :o3], lw, la, zr[..., o5:]], axis=-1)


def _prep_in_weights(w_in, shift_mu):
    w_slab = _rearrange_in_cols(w_in).astype(BF16)
    mu_full = jnp.pad(shift_mu, ((0, 0), (3 * ATT_WIDTH, 2 * D_MODEL)))
    mu_slab = _rearrange_in_cols(mu_full)[:, SL_RW * LANES:]
    return w_slab, mu_slab.reshape(2, 1, N_RW_SLABS * LANES)


def _inproj_kernel(x_ref, g_ref, w_ref, o_ref, xn_ref, *, n_out_slabs):
    @pl.when(pl.program_id(1) == 0)
    def _():
        x = x_ref[...]
        ms = jnp.mean(x * x, axis=-1, keepdims=True)
        xn_ref[...] = (x * lax.rsqrt(ms + NORM_EPS) * g_ref[...]).astype(BF16)

    acc = _dot(xn_ref[...], w_ref[...])
    for c in range(n_out_slabs):
        o_ref[c] = acc[:, c * LANES:(c + 1) * LANES].astype(BF16)


def _inproj(x2d, g, w_slab):
    n = x2d.shape[0]
    tm = min(1024, n)
    tn = 1536
    n_out_slabs = tn // LANES
    grid = (n // tm, (N_SLABS * LANES) // tn)
    return pl.pallas_call(
        functools.partial(_inproj_kernel, n_out_slabs=n_out_slabs),
        grid=grid,
        in_specs=[
            pl.BlockSpec((tm, D_MODEL), lambda i, j: (i, 0)),
            pl.BlockSpec((1, D_MODEL), lambda i, j: (0, 0)),
            pl.BlockSpec((D_MODEL, tn), lambda i, j: (0, j)),
        ],
        out_specs=pl.BlockSpec((n_out_slabs, tm, LANES), lambda i, j: (j, i, 0)),
        out_shape=jax.ShapeDtypeStruct((N_SLABS, n, LANES), BF16),
        scratch_shapes=[pltpu.VMEM((tm, D_MODEL), BF16)],
        compiler_params=_cparams(("parallel", "arbitrary")),
        name="inproj",
    )(x2d, g, w_slab)


def _attn_kernel(slopes_ref, lq1_ref, lk1_ref, lq2_ref, lk2_ref, subg_ref, q_ref, k_ref, v_ref, o_ref,
                 kt1_ref, kt2_ref, vaug_ref, *, seq, tq):
    h = pl.program_id(1)
    qi = pl.program_id(2)
    slope = slopes_ref[h]
    q0 = pl.multiple_of(qi * tq, tq)
    view = pl.ds(q0, seq)

    @pl.when(qi == 0)
    def _():
        kt = k_ref[0].astype(F32).T
        row = lax.broadcasted_iota(I32, kt.shape, 0)
        k1 = jnp.where(row < ATT_HEAD_DIM, kt, 0.0).astype(BF16)
        k2 = jnp.where(row >= ATT_HEAD_DIM, kt, 0.0).astype(BF16)
        lane = lax.broadcasted_iota(I32, (seq, LANES), 1)
        va = jnp.concatenate([v_ref[0], jnp.where(lane == 0, 1.0, 0.0).astype(BF16)], axis=1)
        for half in (slice(0, seq), slice(seq, 2 * seq)):
            kt1_ref[:, half] = k1
            kt2_ref[:, half] = k2
            vaug_ref[half, :] = va

    col = lax.broadcasted_iota(I32, (16, seq), 1)
    r16 = lax.broadcasted_iota(I32, (16, seq), 0)
    wrapped = col + q0 >= seq
    jp = jnp.where(wrapped, col - seq, col)
    sigma = jnp.where(col < tq, 0.0, jnp.where(wrapped, -1.0, 1.0)).astype(F32)
    jh = (jp >> 8).astype(F32)
    jl = (jp & 255).astype(F32)
    feat = jnp.where(r16 <= 1, sigma,
                     jnp.where(r16 == 2, -sigma * (slope * 256.0) * jh,
                               jnp.where(r16 == 3, -sigma * slope * jl, 0.0))).astype(BF16)
    kt1_ref[ATT_HEAD_DIM:ATT_HEAD_DIM + 16, view] = feat
    kt2_ref[0:16, view] = feat

    q = q_ref[0].astype(F32) * (ATT_HEAD_DIM ** -0.5)
    lane = lax.broadcasted_iota(I32, (tq, LANES), 1)
    ip = lax.broadcasted_iota(I32, (tq, LANES), 0)
    ih = slope * (ip & ~255).astype(F32)
    il = slope * (ip & 255).astype(F32)

    def query_side(fl):
        return jnp.where(fl == 0, ih, jnp.where(fl == 1, il, jnp.where(fl <= 3, 1.0, 0.0)))

    lhs1 = jnp.where(lane < ATT_HEAD_DIM, q, query_side(lane - ATT_HEAD_DIM)).astype(BF16)
    lhs2 = jnp.where(lane >= ATT_HEAD_DIM, q, query_side(lane)).astype(BF16)

    di = lax.broadcasted_iota(I32, (tq, tq), 0)
    dj = lax.broadcasted_iota(I32, (tq, tq), 1)
    diag_bias = -slope * jnp.abs(di - dj).astype(F32)

    def weights(lhs, kt_ref):
        s = _dot(lhs, kt_ref[:, view])
        s = jnp.concatenate([s[:, :tq] + diag_bias, s[:, tq:]], axis=1)
        m = jnp.max(s, axis=-1, keepdims=True)
        return jnp.exp(s - m).astype(BF16)

    e = jnp.concatenate([weights(lhs1, kt1_ref), weights(lhs2, kt2_ref)], axis=0)
    oa = _dot(e, vaug_ref[view, :])
    o1 = oa[:tq, :LANES] / oa[:tq, LANES:LANES + 1]
    o2 = oa[tq:, :LANES] / oa[tq:, LANES:LANES + 1]
    lam = (jnp.exp(jnp.sum(lq1_ref[...] * lk1_ref[...], keepdims=True))
           - jnp.exp(jnp.sum(lq2_ref[...] * lk2_ref[...], keepdims=True)) + LAM_INIT)
    out = o1 - lam * o2
    ms = jnp.mean(out * out, axis=-1, keepdims=True)
    y = out * lax.rsqrt(ms + SUBLN_EPS) * subg_ref[...]
    o_ref[0] = (y * (1.0 - LAM_INIT)).astype(BF16)


def _attention(slabs, slopes, lq1, lk1, lq2, lk2, subg, batch, seq):
    n = batch * seq
    tq = 512
    nq = seq // tq
    vec = lambda: pl.BlockSpec((1, ATT_HEAD_DIM), lambda b, h, i: (0, 0))
    return pl.pallas_call(
        functools.partial(_attn_kernel, seq=seq, tq=tq),
        grid=(batch, ATT_HEADS, nq),
        in_specs=[
            pl.BlockSpec(memory_space=pltpu.SMEM),
            vec(), vec(), vec(), vec(),
            pl.BlockSpec((1, LANES), lambda b, h, i: (0, 0)),
            pl.BlockSpec((1, tq, LANES), lambda b, h, i: (SL_ATT_Q + h, b * nq + i, 0)),
            pl.BlockSpec((1, seq, LANES), lambda b, h, i: (SL_ATT_K + h, b, 0)),
            pl.BlockSpec((1, seq, LANES), lambda b, h, i: (SL_ATT_V + h, b, 0)),
        ],
        out_specs=pl.BlockSpec((1, tq, LANES), lambda b, h, i: (h, b * nq + i, 0)),
        out_shape=jax.ShapeDtypeStruct((ATT_HEADS, n, LANES), BF16),
        scratch_shapes=[
            pltpu.VMEM((LANES, 2 * seq), BF16),
            pltpu.VMEM((LANES, 2 * seq), BF16),
            pltpu.VMEM((2 * seq, 2 * LANES), BF16),
        ],
        compiler_params=_cparams(("parallel", "parallel", "arbitrary")),
        name="diff_attn",
    )(slopes, lq1, lk1, lq2, lk2, subg, slabs, slabs, slabs)


def _head_segsum(x, bd):
    hi, mid, lo = _split3(x)
    return _dot(hi, bd) + _dot(mid, bd) + _dot(lo, bd)


def _block_ones():
    ri = lax.broadcasted_iota(I32, (LANES, LANES), 0)
    ci = lax.broadcasted_iota(I32, (LANES, LANES), 1)
    return jnp.where((ri >> 6) == (ci >> 6), 1.0, 0.0).astype(BF16)


def _rwprep_kernel(main_ref, prev_ref, next_ref, mu_ref, w0_ref, a0_ref, kk_ref, ka_ref, w2_ref, a2_ref,
                   g2_ref, r_ref, v_ref, na_ref, g_ref, kd_ref, b_ref, ld_ref, *, nt):
    i = pl.program_id(1)
    t = main_ref.shape[1]
    hb = prev_ref.shape[1]
    wide = lambda ref: jnp.concatenate([ref[c] for c in range(N_RW_SLABS)], axis=1)
    zb = wide(main_ref)
    prev = jnp.where(i > 0, wide(prev_ref), jnp.zeros((), BF16))
    nxt = jnp.where(i < nt - 1, wide(next_ref), jnp.zeros((), BF16))
    halo = jnp.concatenate([prev, zb, nxt], axis=0)
    ri = lax.broadcasted_iota(I32, (t, t + 2 * hb), 0)
    ci = lax.broadcasted_iota(I32, (t, t + 2 * hb), 1)
    zp = _dot(jnp.where(ci == ri + hb - 1, 1.0, 0.0).astype(BF16), halo)
    zn = _dot(jnp.where(ci == ri + hb + 1, 1.0, 0.0).astype(BF16), halo)
    z = zb.astype(F32)
    z = z + mu_ref[0] * (zp - z) + mu_ref[1] * (zn - z)
    slab = lambda s: z[:, s * LANES:(s + 1) * LANES]

    xw = jnp.tanh(slab(24)).astype(BF16)
    xa = slab(25).astype(BF16)
    xg = _sigmoid(z[:, 26 * LANES:28 * LANES]).astype(BF16)
    g_full = _dot(xg, g2_ref[...])
    lw = [_dot(xw, w2_ref[d]) for d in range(2)]
    la = [_dot(xa, a2_ref[d]) for d in range(2)]
    bd = _block_ones()
    for c in range(8):
        cs = slice(c * LANES, (c + 1) * LANES)
        kc = slab(8 + c)
        kk = kc * kk_ref[c]
        nrm = jnp.sqrt(_head_segsum(kk * kk, bd))
        kk = kk / jnp.maximum(nrm, 1e-12)
        r_ref[c] = slab(c)
        v_ref[c] = slab(16 + c)
        na_ref[c] = -kk
        g_ref[c] = g_full[:, cs]
        for d in range(2):
            ld_ref[d, c] = -math.exp(-0.5) * _sigmoid(w0_ref[d, c] + lw[d][:, cs])
            asig = _sigmoid(a0_ref[d, c] + la[d][:, cs])
            kd_ref[d, c] = kc * (1.0 + (asig - 1.0) * ka_ref[c])
            b_ref[d, c] = kk * asig


def _prep_rw_params(p):
    vec = lambda a: a.reshape(a.shape[:-1] + (8, 1, LANES))
    pad_rows = lambda a: jnp.pad(a, ((0, 0), (0, LANES - a.shape[1]), (0, 0))).astype(BF16)
    return (vec(p["rw_w0"][0]), vec(p["rw_a0"][0]), vec(p["rw_k_k"][0]), vec(p["rw_k_a"][0]),
            pad_rows(p["rw_w2"][0]), pad_rows(p["rw_a2"][0]), p["rw_g2"][0].astype(BF16),
            vec(p["rw_r_k"][0].reshape(RW_WIDTH)), vec(p["lnx_g"][0]), vec(p["lnx_b"][0]))


def _rwprep(slabs, mu_slab, w0, a0, k_k, k_a, w2, a2, g2, batch, seq):
    n = batch * seq
    t = 256
    nt = seq // t
    hb = 16
    full = lambda shape: pl.BlockSpec(shape, lambda b, i: (0,) * len(shape))
    rows = lambda b, i: b * nt + i
    o8 = pl.BlockSpec((8, t, LANES), lambda b, i: (0, rows(b, i), 0))
    o28 = pl.BlockSpec((2, 8, t, LANES), lambda b, i: (0, 0, rows(b, i), 0))
    s8 = jax.ShapeDtypeStruct((8, n, LANES), F32)
    s28 = jax.ShapeDtypeStruct((2, 8, n, LANES), F32)
    rw_blk = SL_RW // N_RW_SLABS
    return pl.pallas_call(
        functools.partial(_rwprep_kernel, nt=nt),
        grid=(batch, nt),
        in_specs=[
            pl.BlockSpec((N_RW_SLABS, t, LANES), lambda b, i: (rw_blk, rows(b, i), 0)),
            pl.BlockSpec((N_RW_SLABS, hb, LANES),
                         lambda b, i: (rw_blk, jnp.maximum((b * seq + i * t) // hb - 1, 0), 0)),
            pl.BlockSpec((N_RW_SLABS, hb, LANES),
                         lambda b, i: (rw_blk, jnp.minimum((b * seq + (i + 1) * t) // hb, n // hb - 1), 0)),
            full((2, 1, N_RW_SLABS * LANES)),
            full((2, 8, 1, LANES)), full((2, 8, 1, LANES)), full((8, 1, LANES)), full((8, 1, LANES)),
            full((2, LANES, RW_WIDTH)), full((2, LANES, RW_WIDTH)), full((GATE_LORA, RW_WIDTH)),
        ],
        out_specs=[o8, o8, o8, o8, o28, o28, o28],
        out_shape=[s8, s8, s8, s8, s28, s28, s28],
        compiler_params=_cparams(("parallel", "parallel")),
        name="rwkv_prep",
    )(slabs, slabs, slabs, mu_slab, w0, a0, k_k, k_a, w2, a2, g2)


def _rwscan_kernel(rf_ref, vf_ref, naf_ref, kdf_ref, bf_ref, ldf_ref,
                   rb_ref, vb_ref, nab_ref, kdb_ref, bb_ref, ldb_ref, rk_ref,
                   yf_ref, yb_ref, st_ref, *, nc):
    @pl.when(pl.program_id(2) == 0)
    def _():
        st_ref[...] = jnp.zeros_like(st_ref)

    lane = lax.broadcasted_iota(I32, (CHUNK, LANES), 1)
    head0 = lane < RW_HEAD
    ri = lax.broadcasted_iota(I32, (LANES, LANES), 0)
    ci = lax.broadcasted_iota(I32, (LANES, LANES), 1)
    same = (ri >> 6) == (ci >> 6)
    tt = ri & (CHUNK - 1)
    ss = ci & (CHUNK - 1)
    eye = jnp.where(ri == ci, 1.0, 0.0).astype(F32)
    tr = lax.broadcasted_iota(I32, (CHUNK, CHUNK), 0)
    tc = lax.broadcasted_iota(I32, (CHUNK, CHUNK), 1)
    bd = _block_ones()
    rk = rk_ref[0]

    def stack(x):
        return jnp.concatenate([jnp.where(head0, x, 0.0), jnp.where(head0, 0.0, x)], axis=0)

    dirs = (
        (rf_ref, vf_ref, naf_ref, kdf_ref, bf_ref, ldf_ref, yf_ref),
        (rb_ref, vb_ref, nab_ref, kdb_ref, bb_ref, ldb_ref, yb_ref),
    )
    strict = (same & (ss < tt), same & (ss > tt))
    incl = (same & (ss <= tt), same & (ss >= tt))
    tri = (jnp.where(tc <= tr, 1.0, 0.0).astype(BF16), jnp.where(tc >= tr, 1.0, 0.0).astype(BF16))
    last = (CHUNK - 1, 0)
    insts = [(d, k if d == 0 else nc - 1 - k) for k in range(nc) for d in range(2)]
    every = range(len(insts))

    def load(ref, d, ch, lead):
        sl = slice(ch * CHUNK, (ch + 1) * CHUNK)
        return ref[(0,) * lead + (sl, slice(None))]

    r = [load(dirs[d][0], d, ch, 1) for d, ch in insts]
    v = [load(dirs[d][1], d, ch, 1) for d, ch in insts]
    na = [load(dirs[d][2], d, ch, 1) for d, ch in insts]
    kd = [load(dirs[d][3], d, ch, 2) for d, ch in insts]
    b = [load(dirs[d][4], d, ch, 2) for d, ch in insts]
    ld = [load(dirs[d][5], d, ch, 2) for d, ch in insts]
    tri3 = [jnp.concatenate([t, t, t], axis=1) for t in tri]
    c = [_dot(tri3[insts[i][0]], jnp.concatenate(_split3(ld[i]), axis=0)) for i in every]
    total = [c[i][last[insts[i][0]]:last[insts[i][0]] + 1] for i in every]
    e_nc = [jnp.exp(-c[i]) for i in every]
    e_tc = [jnp.exp(total[i] - c[i]) for i in every]
    a_t = [stack(na[i] * jnp.exp(c[i] - ld[i])).astype(BF16) for i in every]
    r_t = [stack(r[i] * jnp.exp(c[i])).astype(BF16) for i in every]
    v_s = [stack(v[i]).astype(BF16) for i in every]
    rhs = [jnp.concatenate([stack(b[i] * e_nc[i]), stack(kd[i] * e_nc[i])], axis=0).astype(BF16) for i in every]
    bk = [jnp.concatenate([stack(b[i] * e_tc[i]), stack(kd[i] * e_tc[i])], axis=0).astype(BF16) for i in every]
    p = [_dot_nt(jnp.concatenate([a_t[i], r_t[i]], axis=0), rhs[i]) for i in every]
    n_ab = [jnp.where(strict[insts[i][0]], p[i][:LANES, :LANES], 0.0) for i in every]
    a_ak = [jnp.where(strict[insts[i][0]], p[i][:LANES, LANES:], 0.0).astype(BF16) for i in every]
    p_rb = [jnp.where(incl[insts[i][0]], p[i][LANES:, :LANES], 0.0).astype(BF16) for i in every]
    p_rk = [jnp.where(incl[insts[i][0]], p[i][LANES:, LANES:], 0.0).astype(BF16) for i in every]
    x = [eye + n_ab[i] for i in every]
    nk = [n_ab[i].astype(BF16) for i in every]
    nk = [_dot(nk[i], nk[i]) for i in every]
    for _ in range(4):
        both = [_dot(nk[i].astype(BF16), jnp.concatenate([nk[i], x[i]], axis=1).astype(BF16)) for i in every]
        nk = [both[i][:, :LANES] for i in every]
        x = [x[i] + both[i][:, LANES:] for i in every]
    x = [x[i] + _dot(nk[i].astype(BF16), x[i].astype(BF16)) for i in every]
    w = [_dot(a_ak[i], v_s[i]) for i in every]
    au = [_dot(x[i].astype(BF16), jnp.concatenate([a_t[i], w[i].astype(BF16)], axis=1)) for i in every]
    p_rbk = [jnp.concatenate([p_rb[i], p_rk[i]], axis=1) for i in every]
    bonus = [_head_segsum(r[i] * kd[i] * rk, bd) * v[i] for i in every]
    st = [st_ref[0], st_ref[1]]
    for i, (d, ch) in enumerate(insts):
        uy = _dot_nt(jnp.concatenate([au[i][:, :LANES].astype(BF16), r_t[i]], axis=0), st[d].astype(BF16))
        u = (uy[:LANES] + au[i][:, LANES:]).astype(BF16)
        y = uy[LANES:] + _dot(p_rbk[i], jnp.concatenate([u, v_s[i]], axis=0))
        st[d] = st[d] * jnp.exp(total[i]) + _dot_tn(jnp.concatenate([u, v_s[i]], axis=0), bk[i])
        dirs[d][6][0, ch * CHUNK:(ch + 1) * CHUNK, :] = y[:CHUNK] + y[CHUNK:] + bonus[i]
    st_ref[0] = st[0]
    st_ref[1] = st[1]


def _rwscan(r, v, na, kd, b, ld, r_k, batch, seq):
    n = batch * seq
    nc = min(8, seq // CHUNK)
    tcs = CHUNK * nc
    nt = seq // tcs
    fwd = lambda bi, c, t: bi * nt + t
    bwd = lambda bi, c, t: bi * nt + nt - 1 - t
    s3 = lambda rows: pl.BlockSpec((1, tcs, LANES), lambda bi, c, t: (c, rows(bi, c, t), 0))
    s4 = lambda d, rows: pl.BlockSpec((1, 1, tcs, LANES), lambda bi, c, t: (d, c, rows(bi, c, t), 0))
    out = jax.ShapeDtypeStruct((8, n, LANES), F32)
    return pl.pallas_call(
        functools.partial(_rwscan_kernel, nc=nc),
        grid=(batch, 8, nt),
        in_specs=[s3(fwd), s3(fwd), s3(fwd), s4(0, fwd), s4(0, fwd), s4(0, fwd),
                  s3(bwd), s3(bwd), s3(bwd), s4(1, bwd), s4(1, bwd), s4(1, bwd),
                  pl.BlockSpec((1, 1, LANES), lambda bi, c, t: (c, 0, 0))],
        out_specs=[s3(fwd), s3(bwd)],
        out_shape=[out, out],
        scratch_shapes=[pltpu.VMEM((2, LANES, LANES), F32)],
        compiler_params=_cparams(("parallel", "parallel", "arbitrary")),
        name="rwkv_scan",
    )(r, v, na, kd, b, ld, r, v, na, kd, b, ld, r_k)


def _merge_kernel(x_ref, oatt_ref, yf_ref, yb_ref, g_ref, gates_ref, gb_ref, lng_ref, lnb_ref,
                  wba_ref, wbr_ref, wout_ref, h_ref):
    bd = _block_ones()
    orw = []
    for c in range(8):
        y = yf_ref[c] + yb_ref[c]
        mu = _head_segsum(y, bd) * (1.0 / RW_HEAD)
        yc = y - mu
        var = _head_segsum(yc * yc, bd) * (1.0 / RW_HEAD)
        yn = yc * lax.rsqrt(var + LNX_EPS) * lng_ref[c] + lnb_ref[c]
        orw.append((yn * g_ref[c]).astype(BF16))
    orw = jnp.concatenate(orw, axis=1)
    oatt = jnp.concatenate([oatt_ref[c] for c in range(8)], axis=1)
    ga = jnp.concatenate([gates_ref[c] for c in range(16)], axis=1).astype(F32) + gb_ref[0]
    gr = jnp.concatenate([gates_ref[16 + c] for c in range(16)], axis=1).astype(F32) + gb_ref[1]
    merged = _sigmoid(ga) * _dot(oatt, wba_ref[...]) + _sigmoid(gr) * _dot(orw, wbr_ref[...])
    h_ref[...] = x_ref[...] + _dot(merged.astype(BF16), wout_ref[...])


def _merge(x2d, oatt, yf, yb, g, slabs, gate_b, lng, lnb, wba, wbr, wout):
    n = x2d.shape[0]
    tm = min(256, n)
    const = lambda shape: pl.BlockSpec(shape, lambda i: (0,) * len(shape), pipeline_mode=pl.Buffered(1))
    s8 = pl.BlockSpec((8, tm, LANES), lambda i: (0, i, 0))
    return pl.pallas_call(
        _merge_kernel,
        grid=(n // tm,),
        in_specs=[
            pl.BlockSpec((tm, D_MODEL), lambda i: (i, 0)),
            s8, s8, s8, s8,
            pl.BlockSpec((32, tm, LANES), lambda i: (0, i, 0)),
            const((2, 1, D_MODEL)), const((8, 1, LANES)), const((8, 1, LANES)),
            const((ATT_WIDTH, D_MODEL)), const((RW_WIDTH, D_MODEL)), const((D_MODEL, D_MODEL)),
        ],
        out_specs=pl.BlockSpec((tm, D_MODEL), lambda i: (i, 0)),
        out_shape=jax.ShapeDtypeStruct((n, D_MODEL), F32),
        compiler_params=_cparams(("parallel",)),
        name="merge_outproj",
    )(x2d, oatt, yf, yb, g, slabs, gate_b, lng, lnb, wba, wbr, wout)


def _router_kernel(h_ref, g_ref, wr_ref, hn_ref, aff_ref):
    x = h_ref[...]
    ms = jnp.mean(x * x, axis=-1, keepdims=True)
    hn = x * lax.rsqrt(ms + NORM_EPS) * g_ref[...]
    hn_ref[...] = hn
    xh, xm, xl = _split3(hn)
    wh, wm, wl = _split3(wr_ref[...])
    logits = (_dot(xh, wh) + _dot(xh, wm) + _dot(xm, wh)
              + _dot(xh, wl) + _dot(xl, wh) + _dot(xm, wm))
    lt = logits.T[:N_EXPERTS]
    m = jnp.max(lt, axis=0, keepdims=True)
    e = jnp.exp(lt - m)
    aff_ref[...] = e / jnp.sum(e, axis=0, keepdims=True)


def _router(h2d, g, wr_pad):
    n = h2d.shape[0]
    tm = min(256, n)
    return pl.pallas_call(
        _router_kernel,
        grid=(n // tm,),
        in_specs=[
            pl.BlockSpec((tm, D_MODEL), lambda i: (i, 0)),
            pl.BlockSpec((1, D_MODEL), lambda i: (0, 0)),
            pl.BlockSpec((D_MODEL, LANES), lambda i: (0, 0)),
        ],
        out_specs=[
            pl.BlockSpec((tm, D_MODEL), lambda i: (i, 0)),
            pl.BlockSpec((N_EXPERTS, tm), lambda i: (0, i)),
        ],
        out_shape=[
            jax.ShapeDtypeStruct((n, D_MODEL), F32),
            jax.ShapeDtypeStruct((N_EXPERTS, n), F32),
        ],
        compiler_params=_cparams(("parallel",)),
        name="router",
    )(h2d, g, wr_pad)


def _select_kernel(aff_ref, incl_ref, tbl_ref, gval_ref, cnt_ref, slot_ref, *, cap, tt):
    bits = pltpu.bitcast(aff_ref[...], I32)
    nrow = bits.shape[1]

    def count(mask):
        c = jnp.sum(jnp.where(mask, 1, 0), axis=2, keepdims=True)
        return jnp.sum(c, axis=1, keepdims=True)

    def body(_, carry):
        lo, hi = carry
        mid = lo + ((hi - lo) >> 1)
        ok = count(bits >= mid) >= cap
        return jnp.where(ok, mid, lo), jnp.where(ok, hi, mid)

    lo0 = jnp.zeros((N_EXPERTS, 1, 1), I32)
    hi0 = jnp.full((N_EXPERTS, 1, 1), 0x7F800000, I32)
    thr, _ = lax.fori_loop(0, 31, body, (lo0, hi0))
    gt = bits > thr
    eq = bits == thr
    need = cap - count(gt)

    ri = lax.broadcasted_iota(I32, (LANES, LANES), 0)
    ci = lax.broadcasted_iota(I32, (LANES, LANES), 1)
    upper = jnp.where(ri <= ci, 1.0, 0.0).astype(BF16)
    rr = lax.broadcasted_iota(I32, (nrow, nrow), 0)
    rc = lax.broadcasted_iota(I32, (nrow, nrow), 1)
    lower_strict = jnp.where(rc < rr, 1.0, 0.0).astype(BF16)

    def incl_prefix(mask):
        x = jnp.where(mask, 1.0, 0.0).astype(BF16)
        incl = _dot(x.reshape(N_EXPERTS * nrow, LANES), upper).reshape(N_EXPERTS, nrow, LANES)
        tot = jnp.broadcast_to(incl[:, :, LANES - 1:LANES], incl.shape).astype(BF16)
        before = jnp.stack([_dot(lower_strict, tot[e]) for e in range(N_EXPERTS)], axis=0)
        return incl + before

    sel = gt | (eq & (incl_prefix(eq) - 1.0 < need.astype(F32)))
    incl_ref[...] = incl_prefix(sel).astype(I32)
    run = jnp.zeros((nrow, LANES), F32)
    for e in range(N_EXPERTS):
        slot_ref[e] = run
        run = run + jnp.where(sel[e], 1.0, 0.0)
    cnt_ref[...] = run.astype(I32)

    pf = lax.broadcasted_iota(I32, (1, cap), 1).astype(F32)
    jrow = lax.broadcasted_iota(I32, (nrow, cap), 0).astype(F32)
    lrow = lax.broadcasted_iota(I32, (LANES, cap), 0).astype(F32)

    def compact(e, carry):
        g = incl_ref[e].astype(F32)
        jsel = jnp.sum(jnp.where(g[:, LANES - 1:LANES] <= pf, 1.0, 0.0), axis=0, keepdims=True)
        onehot = jnp.where(jrow == jsel, 1.0, 0.0).astype(BF16)
        ghi = jnp.floor(g * (1.0 / 256.0))
        glo = g - 256.0 * ghi
        grow = 256.0 * _dot_tn(ghi.astype(BF16), onehot) + _dot_tn(glo.astype(BF16), onehot)
        lstar = jnp.sum(jnp.where(grow <= pf, 1.0, 0.0), axis=0, keepdims=True)
        lsel = lrow == lstar
        ah, am, al = _split3(aff_ref[e])
        arow = _dot_tn(ah, onehot) + _dot_tn(am, onehot) + _dot_tn(al, onehot)
        gval_ref[pl.ds(e, 1), :] = jnp.sum(jnp.where(lsel, arow, 0.0), axis=0, keepdims=True)
        krow = _dot_tn(slot_ref[e].astype(BF16), onehot)
        kk = jnp.sum(jnp.where(lsel, krow, 0.0), axis=0, keepdims=True)
        tok = (jsel * float(LANES) + lstar).astype(I32)
        dest = kk.astype(I32) * tt + (tok & (tt - 1))
        tbl_ref[pl.ds(e, 1), :] = tok | (dest << 16)
        return carry

    lax.fori_loop(0, N_EXPERTS, compact, 0)


def _select(aff3, cap, tt):
    nrow = aff3.shape[1]
    return pl.pallas_call(
        functools.partial(_select_kernel, cap=cap, tt=tt),
        out_shape=[
            jax.ShapeDtypeStruct(aff3.shape, I32),
            jax.ShapeDtypeStruct((N_EXPERTS, cap), I32),
            jax.ShapeDtypeStruct((N_EXPERTS, cap), F32),
            jax.ShapeDtypeStruct((nrow, LANES), I32),
        ],
        scratch_shapes=[pltpu.VMEM((N_EXPERTS, nrow, LANES), F32)],
        compiler_params=pltpu.CompilerParams(vmem_limit_bytes=VMEM_LIMIT),
        name="expert_select",
    )(aff3)


def _ffn_kernel(tbl_ref, hn_hbm, gval_ref, wg_ref, wu_ref, wd_ref, out_ref, xbuf, sem, *, tc, nt):
    step = pl.program_id(0) * nt + pl.program_id(1)
    slot = step % 2

    def issue(stp, slt):
        def body(i, carry):
            tok = tbl_ref[stp * tc + i] & 0xFFFF
            pltpu.make_async_copy(hn_hbm.at[pl.ds(tok, 1)], xbuf.at[slt, pl.ds(i, 1)], sem.at[slt]).start()
            return carry
        lax.fori_loop(0, tc, body, 0, unroll=8)

    @pl.when(step == 0)
    def _():
        issue(0, 0)

    @pl.when(step + 1 < N_EXPERTS * nt)
    def _():
        issue(step + 1, 1 - slot)

    pltpu.make_async_copy(hn_hbm.at[pl.ds(0, tc)], xbuf.at[slot], sem.at[slot]).wait()
    xe = xbuf[slot].astype(BF16)
    a = _dot(xe, wg_ref[0])
    u = _dot(xe, wu_ref[0])
    hmid = (a * _sigmoid(a) * u).astype(BF16)
    out_ref[...] = _dot(hmid, wd_ref[0]) * gval_ref[0]


def _expert_ffn(tbl_flat, hn, gval, wg, wu, wd, cap):
    tc = min(256, cap)
    nt = cap // tc
    grid_spec = pltpu.PrefetchScalarGridSpec(
        num_scalar_prefetch=1,
        grid=(N_EXPERTS, nt),
        in_specs=[
            pl.BlockSpec(memory_space=pl.ANY),
            pl.BlockSpec((1, tc, 1), lambda e, j, idx: (e * nt + j, 0, 0)),
            pl.BlockSpec((1, D_MODEL, EXPERT_FF), lambda e, j, idx: (e, 0, 0)),
            pl.BlockSpec((1, D_MODEL, EXPERT_FF), lambda e, j, idx: (e, 0, 0)),
            pl.BlockSpec((1, EXPERT_FF, D_MODEL), lambda e, j, idx: (e, 0, 0)),
        ],
        out_specs=pl.BlockSpec((tc, D_MODEL), lambda e, j, idx: (e * nt + j, 0)),
        scratch_shapes=[pltpu.VMEM((2, tc, D_MODEL), F32), pltpu.SemaphoreType.DMA((2,))],
    )
    return pl.pallas_call(
        functools.partial(_ffn_kernel, tc=tc, nt=nt),
        grid_spec=grid_spec,
        out_shape=jax.ShapeDtypeStruct((N_EXPERTS * cap, D_MODEL), F32),
        compiler_params=_cparams(("arbitrary", "arbitrary")),
        name="expert_ffn",
    )(tbl_flat, hn, gval.reshape(N_EXPERTS * nt, tc, 1), wg, wu, wd)


def _combine_kernel(tbl_ref, p0_ref, km_ref, h_ref, cnt_ref, g_ref, ye_hbm, out_ref, stage, sem,
                    *, tt, cap, ntile):
    tile = pl.program_id(0)
    slot = tile % 2

    def issue(tl, slt):
        for e in range(N_EXPERTS):
            p0 = p0_ref[e * (ntile + 1) + tl]
            cnt = p0_ref[e * (ntile + 1) + tl + 1] - p0

            def fetch(q, e=e, p0=p0):
                row = e * cap + p0 + q
                pltpu.make_async_copy(ye_hbm.at[pl.ds(row, 1)], stage.at[slt, pl.ds(tbl_ref[row] >> 16, 1)],
                                      sem.at[slt]).start()

            def four(j, carry, fetch=fetch):
                for u in range(4):
                    fetch(4 * j + u)
                return carry

            def one(q, carry, fetch=fetch, cnt=cnt):
                fetch((cnt & ~3) + q)
                return carry

            lax.fori_loop(0, cnt >> 2, four, 0)
            lax.fori_loop(0, cnt & 3, one, 0)

    @pl.when(tile == 0)
    def _():
        issue(0, 0)

    @pl.when(tile + 1 < ntile)
    def _():
        issue(tile + 1, 1 - slot)

    def wait_rows(nrows):
        def body(q, carry):
            pltpu.make_async_copy(ye_hbm.at[pl.ds(0, nrows)], stage.at[slot, pl.ds(0, nrows)], sem.at[slot]).wait()
            return carry
        return body

    total = km_ref[2 * tile + 1]
    lax.fori_loop(0, total >> 3, wait_rows(8), 0)
    lax.fori_loop(0, total & 7, wait_rows(1), 0)

    cnt = cnt_ref[...]
    parts = []
    for cs in (slice(c * 512, (c + 1) * 512) for c in range(D_MODEL // 512)):
        def add(k, acc, cs=cs):
            rows = stage[slot, pl.ds(pl.multiple_of(k * tt, tt), tt), cs]
            return acc + jnp.where(cnt > k, rows, 0.0)
        parts.append(lax.fori_loop(0, km_ref[2 * tile], add, h_ref[:, cs]))
    acc = jnp.concatenate(parts, axis=1)
    ms = jnp.mean(acc * acc, axis=-1, keepdims=True)
    out_ref[...] = acc * lax.rsqrt(ms + NORM_EPS) * g_ref[...]


def _combine(tbl_flat, p0_flat, km_flat, h2d, cnt_tok, g, yexp, cap, tt):
    n = h2d.shape[0]
    ntile = n // tt
    grid_spec = pltpu.PrefetchScalarGridSpec(
        num_scalar_prefetch=3,
        grid=(ntile,),
        in_specs=[
            pl.BlockSpec((tt, D_MODEL), lambda i, a, b, c: (i, 0)),
            pl.BlockSpec((tt, 1), lambda i, a, b, c: (i, 0)),
            pl.BlockSpec((1, D_MODEL), lambda i, a, b, c: (0, 0)),
            pl.BlockSpec(memory_space=pl.ANY),
        ],
        out_specs=pl.BlockSpec((tt, D_MODEL), lambda i, a, b, c: (i, 0)),
        scratch_shapes=[pltpu.VMEM((2, N_EXPERTS * tt, D_MODEL), F32), pltpu.SemaphoreType.DMA((2,))],
    )
    return pl.pallas_call(
        functools.partial(_combine_kernel, tt=tt, cap=cap, ntile=ntile),
        grid_spec=grid_spec,
        out_shape=jax.ShapeDtypeStruct((n, D_MODEL), F32),
        compiler_params=_cparams(("arbitrary",)),
        name="moe_combine",
    )(tbl_flat, p0_flat, km_flat, h2d, cnt_tok, g, yexp)


def _trunk(x, p, w_slab, mu_slab, rwp, moe_w, slopes):
    batch, seq, _ = x.shape
    n = batch * seq
    x2d = x.reshape(n, D_MODEL)
    slabs = _inproj(x2d, p["norm_mix_g"], w_slab)
    oatt = _attention(slabs, slopes, p["lambda_q1"], p["lambda_k1"], p["lambda_q2"], p["lambda_k2"],
                      p["subln_g"], batch, seq)
    r, v, na, g, kd, b, ld = _rwprep(slabs, mu_slab, *rwp[:7], batch, seq)
    yf, yb = _rwscan(r, v, na, kd, b, ld, rwp[7], batch, seq)
    h = _merge(x2d, oatt, yf, yb, g, slabs, p["gate_b"][0].reshape(2, 1, D_MODEL), rwp[8], rwp[9],
               moe_w["wba"], moe_w["wbr"], moe_w["wout"])

    cap = max(1, CAPACITY_FACTOR * n // N_EXPERTS)
    hn, aff = _router(h, p["norm_ffn_g"], moe_w["wr"])
    aff3 = aff.reshape(N_EXPERTS, n // LANES, LANES)
    assert n <= 1 << 16
    tt = 64
    incl, tbl, gval, cnt = _select(aff3, cap, tt)
    ntile = n // tt
    ends = incl.reshape(N_EXPERTS, n)[:, tt - 1::tt]
    p0 = jnp.concatenate([jnp.zeros((N_EXPERTS, 1), I32), ends], axis=1)
    ct = cnt.reshape(ntile, tt)
    km = jnp.stack([jnp.max(ct, axis=1), jnp.sum(ct, axis=1)], axis=1)
    tbl_flat = tbl.reshape(-1)
    yexp = _expert_ffn(tbl_flat, hn, gval, moe_w["wg"], moe_w["wu"], moe_w["wd"], cap)
    y = _combine(tbl_flat, p0.reshape(-1), km.reshape(-1), h, cnt.reshape(n, 1),
                 p["norm_final_g"].reshape(1, D_MODEL), yexp, cap, tt)
    return y.reshape(batch, seq, D_MODEL)


def kernel(x_prompt, x_sample, norm_mix_g, w_in, shift_mu, lambda_q1, lambda_k1, lambda_q2, lambda_k2, subln_g, rw_w0, rw_w2, rw_a0, rw_a2, rw_g2, rw_k_k, rw_k_a, rw_r_k, lnx_g, lnx_b, gate_b, w_br_att, w_br_rw, w_out, norm_ffn_g, w_router, w_gate_e, w_up_e, w_down_e, norm_final_g):
    p = dict(norm_mix_g=norm_mix_g, lambda_q1=lambda_q1, lambda_k1=lambda_k1, lambda_q2=lambda_q2,
             lambda_k2=lambda_k2, subln_g=subln_g, rw_w0=rw_w0, rw_w2=rw_w2, rw_a0=rw_a0, rw_a2=rw_a2,
             rw_g2=rw_g2, rw_k_k=rw_k_k, rw_k_a=rw_k_a, rw_r_k=rw_r_k, lnx_g=lnx_g, lnx_b=lnx_b,
             gate_b=gate_b, norm_ffn_g=norm_ffn_g, norm_final_g=norm_final_g)
    w_slab, mu_slab = _prep_in_weights(w_in[0], shift_mu[0])
    rwp = _prep_rw_params(p)
    moe_w = dict(
        wba=w_br_att[0].astype(BF16), wbr=w_br_rw[0].astype(BF16), wout=w_out[0].astype(BF16),
        wr=jnp.pad(w_router[0], ((0, 0), (0, LANES - N_EXPERTS))),
        wg=w_gate_e[0].astype(BF16), wu=w_up_e[0].astype(BF16), wd=w_down_e[0].astype(BF16))
    slopes = jnp.asarray([2.0 ** (-8.0 * (i + 1) / ATT_HEADS) for i in range(ATT_HEADS)], F32)
    return (_trunk(x_prompt, p, w_slab, mu_slab, rwp, moe_w, slopes),
            _trunk(x_sample, p, w_slab, mu_slab, rwp, moe_w, slopes))
```

```python
import functools
import math

import jax
import jax.numpy as jnp
from jax import lax
from jax.experimental import pallas as pl
from jax.experimental.pallas import tpu as pltpu

F32 = jnp.float32
BF16 = jnp.bfloat16
I32 = jnp.int32

D_MODEL = 2048
ATT_HEADS = 8
ATT_HEAD_DIM = 64
ATT_WIDTH = ATT_HEADS * 2 * ATT_HEAD_DIM
RW_HEAD = 64
RW_WIDTH = 1024
DECAY_LORA = 96
ICLR_LORA = 96
GATE_LORA = 256
SHIFT_WIDTH = 3 * RW_WIDTH + DECAY_LORA + ICLR_LORA + GATE_LORA
N_EXPERTS = 16
CAPACITY_FACTOR = 2
EXPERT_FF = 1024
NORM_EPS = 1e-6
SUBLN_EPS = 1e-5
LNX_EPS = 64e-5
LAM_INIT = 0.8 - 0.6 * math.exp(-0.3 * 0)

LANES = 128
VMEM_LIMIT = 56 * 1024 * 1024

SL_GATE_ATT, SL_GATE_RW = 0, 16
SL_ATT_Q, SL_ATT_K, SL_ATT_V = 32, 40, 48
SL_RW = 56
N_RW_SLABS = 28
N_SLABS = 84
CHUNK = 64
COMBINE_SLOTS = 3
FFN_SLOTS = 3


def _cparams(sem):
    return pltpu.CompilerParams(dimension_semantics=sem, vmem_limit_bytes=VMEM_LIMIT)


def _sigmoid(x):
    return 1.0 / (1.0 + jnp.exp(-x))


def _split3(x):
    hi = x.astype(BF16)
    r1 = x - hi.astype(F32)
    mid = r1.astype(BF16)
    lo = (r1 - mid.astype(F32)).astype(BF16)
    return hi, mid, lo


def _dot(a, b):
    return jnp.dot(a, b, preferred_element_type=F32)


def _dot_nt(a, b):
    return lax.dot_general(a, b, (((1,), (1,)), ((), ())), preferred_element_type=F32)


def _dot_tn(a, b):
    return lax.dot_general(a, b, (((0,), (0,)), ((), ())), preferred_element_type=F32)


def _dot_f32(a_bf16_exact, x):
    hi, mid, lo = _split3(x)
    return _dot(a_bf16_exact, hi) + _dot(a_bf16_exact, mid) + _dot(a_bf16_exact, lo)


def _rearrange_in_cols(a):
    att = a[..., :3 * ATT_WIDTH]
    zr = a[..., 3 * ATT_WIDTH:3 * ATT_WIDTH + SHIFT_WIDTH]
    gates = a[..., 3 * ATT_WIDTH + SHIFT_WIDTH:]
    o3 = 3 * RW_WIDTH
    o4 = o3 + DECAY_LORA
    o5 = o4 + ICLR_LORA
    pad = [(0, 0)] * (a.ndim - 1)
    lw = jnp.pad(zr[..., o3:o4], pad + [(0, LANES - DECAY_LORA)])
    la = jnp.pad(zr[..., o4:o5], pad + [(0, LANES - ICLR_LORA)])
    return jnp.concatenate([gates, att, zr[..., :o3], lw, la, zr[..., o5:]], axis=-1)


def _prep_in_weights(w_in, shift_mu):
    w_slab = _rearrange_in_cols(w_in).astype(BF16)
    mu_full = jnp.pad(shift_mu, ((0, 0), (3 * ATT_WIDTH, 2 * D_MODEL)))
    mu_slab = _rearrange_in_cols(mu_full)[:, SL_RW * LANES:]
    return w_slab, mu_slab.reshape(2, 1, N_RW_SLABS * LANES)


def _inproj_kernel(x_ref, g_ref, w_ref, o_ref, xn_ref, *, n_out_slabs):
    @pl.when(pl.program_id(1) == 0)
    def _():
        x = x_ref[...]
        ms = jnp.mean(x * x, axis=-1, keepdims=True)
        xn_ref[...] = (x * lax.rsqrt(ms + NORM_EPS) * g_ref[...]).astype(BF16)

    acc = _dot(xn_ref[...], w_ref[...])
    for c in range(n_out_slabs):
        o_ref[c] = acc[:, c * LANES:(c + 1) * LANES].astype(BF16)


def _inproj(x2d, g, w_slab):
    n = x2d.shape[0]
    tm = min(1024, n)
    tn = 1536
    n_out_slabs = tn // LANES
    grid = (n // tm, (N_SLABS * LANES) // tn)
    return pl.pallas_call(
        functools.partial(_inproj_kernel, n_out_slabs=n_out_slabs),
        grid=grid,
        in_specs=[
            pl.BlockSpec((tm, D_MODEL), lambda i, j: (i, 0)),
            pl.BlockSpec((1, D_MODEL), lambda i, j: (0, 0)),
            pl.BlockSpec((D_MODEL, tn), lambda i, j: (0, j)),
        ],
        out_specs=pl.BlockSpec((n_out_slabs, tm, LANES), lambda i, j: (j, i, 0)),
        out_shape=jax.ShapeDtypeStruct((N_SLABS, n, LANES), BF16),
        scratch_shapes=[pltpu.VMEM((tm, D_MODEL), BF16)],
        compiler_params=_cparams(("parallel", "arbitrary")),
        name="inproj",
    )(x2d, g, w_slab)


def _attn_kernel(slopes_ref, lq1_ref, lk1_ref, lq2_ref, lk2_ref, subg_ref, q_ref, k_ref, v_ref, o_ref,
                 kt1_ref, kt2_ref, vaug_ref, *, seq, tq):
    h = pl.program_id(1)
    qi = pl.program_id(2)
    slope = slopes_ref[h]
    q0 = pl.multiple_of(qi * tq, tq)
    view = pl.ds(q0, seq)

    @pl.when(qi == 0)
    def _():
        kt = k_ref[0].astype(F32).T
        row = lax.broadcasted_iota(I32, kt.shape, 0)
        k1 = jnp.where(row < ATT_HEAD_DIM, kt, 0.0).astype(BF16)
        k2 = jnp.where(row >= ATT_HEAD_DIM, kt, 0.0).astype(BF16)
        lane = lax.broadcasted_iota(I32, (seq, LANES), 1)
        va = jnp.concatenate([v_ref[0], jnp.where(lane == 0, 1.0, 0.0).astype(BF16)], axis=1)
        for half in (slice(0, seq), slice(seq, 2 * seq)):
            kt1_ref[:, half] = k1
            kt2_ref[:, half] = k2
            vaug_ref[half, :] = va

    col = lax.broadcasted_iota(I32, (16, seq), 1)
    r16 = lax.broadcasted_iota(I32, (16, seq), 0)
    wrapped = col + q0 >= seq
    jp = jnp.where(wrapped, col - seq, col)
    sigma = jnp.where(col < tq, 0.0, jnp.where(wrapped, -1.0, 1.0)).astype(F32)
    jh = (jp >> 8).astype(F32)
    jl = (jp & 255).astype(F32)
    feat = jnp.where(r16 <= 1, sigma,
                     jnp.where(r16 == 2, -sigma * (slope * 256.0) * jh,
                               jnp.where(r16 == 3, -sigma * slope * jl, 0.0))).astype(BF16)
    kt1_ref[ATT_HEAD_DIM:ATT_HEAD_DIM + 16, view] = feat
    kt2_ref[0:16, view] = feat

    q = q_ref[0].astype(F32) * (ATT_HEAD_DIM ** -0.5)
    lane = lax.broadcasted_iota(I32, (tq, LANES), 1)
    ip = lax.broadcasted_iota(I32, (tq, LANES), 0)
    ih = slope * (ip & ~255).astype(F32)
    il = slope * (ip & 255).astype(F32)

    def query_side(fl):
        return jnp.where(fl == 0, ih, jnp.where(fl == 1, il, jnp.where(fl <= 3, 1.0, 0.0)))

    lhs1 = jnp.where(lane < ATT_HEAD_DIM, q, query_side(lane - ATT_HEAD_DIM)).astype(BF16)
    lhs2 = jnp.where(lane >= ATT_HEAD_DIM, q, query_side(lane)).astype(BF16)

    di = lax.broadcasted_iota(I32, (tq, tq), 0)
    dj = lax.broadcasted_iota(I32, (tq, tq), 1)
    diag_bias = -slope * jnp.abs(di - dj).astype(F32)

    def weights(lhs, kt_ref):
        s = _dot(lhs, kt_ref[:, view])
        s = jnp.concatenate([s[:, :tq] + diag_bias, s[:, tq:]], axis=1)
        m = jnp.max(s, axis=-1, keepdims=True)
        return jnp.exp(s - m).astype(BF16)

    e = jnp.concatenate([weights(lhs1, kt1_ref), weights(lhs2, kt2_ref)], axis=0)
    oa = _dot(e, vaug_ref[view, :])
    o1 = oa[:tq, :LANES] / oa[:tq, LANES:LANES + 1]
    o2 = oa[tq:, :LANES] / oa[tq:, LANES:LANES + 1]
    lam = (jnp.exp(jnp.sum(lq1_ref[...] * lk1_ref[...], keepdims=True))
           - jnp.exp(jnp.sum(lq2_ref[...] * lk2_ref[...], keepdims=True)) + LAM_INIT)
    out = o1 - lam * o2
    ms = jnp.mean(out * out, axis=-1, keepdims=True)
    y = out * lax.rsqrt(ms + SUBLN_EPS) * subg_ref[...]
    o_ref[0] = (y * (1.0 - LAM_INIT)).astype(BF16)


def _attention(slabs, slopes, lq1, lk1, lq2, lk2, subg, batch, seq):
    n = batch * seq
    tq = 512
    nq = seq // tq
    vec = lambda: pl.BlockSpec((1, ATT_HEAD_DIM), lambda b, h, i: (0, 0))
    return pl.pallas_call(
        functools.partial(_attn_kernel, seq=seq, tq=tq),
        grid=(batch, ATT_HEADS, nq),
        in_specs=[
            pl.BlockSpec(memory_space=pltpu.SMEM),
            vec(), vec(), vec(), vec(),
            pl.BlockSpec((1, LANES), lambda b, h, i: (0, 0)),
            pl.BlockSpec((1, tq, LANES), lambda b, h, i: (SL_ATT_Q + h, b * nq + i, 0)),
            pl.BlockSpec((1, seq, LANES), lambda b, h, i: (SL_ATT_K + h, b, 0)),
            pl.BlockSpec((1, seq, LANES), lambda b, h, i: (SL_ATT_V + h, b, 0)),
        ],
        out_specs=pl.BlockSpec((1, tq, LANES), lambda b, h, i: (h, b * nq + i, 0)),
        out_shape=jax.ShapeDtypeStruct((ATT_HEADS, n, LANES), BF16),
        scratch_shapes=[
            pltpu.VMEM((LANES, 2 * seq), BF16),
            pltpu.VMEM((LANES, 2 * seq), BF16),
            pltpu.VMEM((2 * seq, 2 * LANES), BF16),
        ],
        compiler_params=_cparams(("parallel", "parallel", "arbitrary")),
        name="diff_attn",
    )(slopes, lq1, lk1, lq2, lk2, subg, slabs, slabs, slabs)


def _head_segsum(x, bd):
    hi, mid, lo = _split3(x)
    return _dot(hi, bd) + _dot(mid, bd) + _dot(lo, bd)


def _block_ones():
    ri = lax.broadcasted_iota(I32, (LANES, LANES), 0)
    ci = lax.broadcasted_iota(I32, (LANES, LANES), 1)
    return jnp.where((ri >> 6) == (ci >> 6), 1.0, 0.0).astype(BF16)


def _rwprep_kernel(main_ref, prev_ref, next_ref, mu_ref, w0_ref, a0_ref, kk_ref, ka_ref, w2_ref, a2_ref,
                   g2_ref, r_ref, v_ref, na_ref, g_ref, kd_ref, b_ref, ld_ref, *, nt):
    i = pl.program_id(1)
    t = main_ref.shape[1]
    hb = prev_ref.shape[1]
    wide = lambda ref: jnp.concatenate([ref[c] for c in range(N_RW_SLABS)], axis=1)
    zb = wide(main_ref)
    prev = jnp.where(i > 0, wide(prev_ref), jnp.zeros((), BF16))
    nxt = jnp.where(i < nt - 1, wide(next_ref), jnp.zeros((), BF16))
    halo = jnp.concatenate([prev, zb, nxt], axis=0)
    ri = lax.broadcasted_iota(I32, (t, t + 2 * hb), 0)
    ci = lax.broadcasted_iota(I32, (t, t + 2 * hb), 1)
    zp = _dot(jnp.where(ci == ri + hb - 1, 1.0, 0.0).astype(BF16), halo)
    zn = _dot(jnp.where(ci == ri + hb + 1, 1.0, 0.0).astype(BF16), halo)
    z = zb.astype(F32)
    z = z + mu_ref[0] * (zp - z) + mu_ref[1] * (zn - z)
    slab = lambda s: z[:, s * LANES:(s + 1) * LANES]

    xw = jnp.tanh(slab(24)).astype(BF16)
    xa = slab(25).astype(BF16)
    xg = _sigmoid(z[:, 26 * LANES:28 * LANES]).astype(BF16)
    g_full = _dot(xg, g2_ref[...])
    lw = [_dot(xw, w2_ref[d]) for d in range(2)]
    la = [_dot(xa, a2_ref[d]) for d in range(2)]
    bd = _block_ones()
    for c in range(8):
        cs = slice(c * LANES, (c + 1) * LANES)
        kc = slab(8 + c)
        kk = kc * kk_ref[c]
        nrm = jnp.sqrt(_head_segsum(kk * kk, bd))
        kk = kk / jnp.maximum(nrm, 1e-12)
        r_ref[c] = slab(c)
        v_ref[c] = slab(16 + c)
        na_ref[c] = -kk
        g_ref[c] = g_full[:, cs]
        for d in range(2):
            ld_ref[d, c] = -math.exp(-0.5) * _sigmoid(w0_ref[d, c] + lw[d][:, cs])
            asig = _sigmoid(a0_ref[d, c] + la[d][:, cs])
            kd_ref[d, c] = kc * (1.0 + (asig - 1.0) * ka_ref[c])
            b_ref[d, c] = kk * asig


def _prep_rw_params(p):
    vec = lambda a: a.reshape(a.shape[:-1] + (8, 1, LANES))
    pad_rows = lambda a: jnp.pad(a, ((0, 0), (0, LANES - a.shape[1]), (0, 0))).astype(BF16)
    return (vec(p["rw_w0"][0]), vec(p["rw_a0"][0]), vec(p["rw_k_k"][0]), vec(p["rw_k_a"][0]),
            pad_rows(p["rw_w2"][0]), pad_rows(p["rw_a2"][0]), p["rw_g2"][0].astype(BF16),
            vec(p["rw_r_k"][0].reshape(RW_WIDTH)), vec(p["lnx_g"][0]), vec(p["lnx_b"][0]))


def _rwprep(slabs, mu_slab, w0, a0, k_k, k_a, w2, a2, g2, batch, seq):
    n = batch * seq
    t = 256
    nt = seq // t
    hb = 16
    full = lambda shape: pl.BlockSpec(shape, lambda b, i: (0,) * len(shape))
    rows = lambda b, i: b * nt + i
    o8 = pl.BlockSpec((8, t, LANES), lambda b, i: (0, rows(b, i), 0))
    o28 = pl.BlockSpec((2, 8, t, LANES), lambda b, i: (0, 0, rows(b, i), 0))
    s8 = jax.ShapeDtypeStruct((8, n, LANES), F32)
    s28 = jax.ShapeDtypeStruct((2, 8, n, LANES), F32)
    rw_blk = SL_RW // N_RW_SLABS
    return pl.pallas_call(
        functools.partial(_rwprep_kernel, nt=nt),
        grid=(batch, nt),
        in_specs=[
            pl.BlockSpec((N_RW_SLABS, t, LANES), lambda b, i: (rw_blk, rows(b, i), 0)),
            pl.BlockSpec((N_RW_SLABS, hb, LANES),
                         lambda b, i: (rw_blk, jnp.maximum((b * seq + i * t) // hb - 1, 0), 0)),
            pl.BlockSpec((N_RW_SLABS, hb, LANES),
                         lambda b, i: (rw_blk, jnp.minimum((b * seq + (i + 1) * t) // hb, n // hb - 1), 0)),
            full((2, 1, N_RW_SLABS * LANES)),
            full((2, 8, 1, LANES)), full((2, 8, 1, LANES)), full((8, 1, LANES)), full((8, 1, LANES)),
            full((2, LANES, RW_WIDTH)), full((2, LANES, RW_WIDTH)), full((GATE_LORA, RW_WIDTH)),
        ],
        out_specs=[o8, o8, o8, o8, o28, o28, o28],
        out_shape=[s8, s8, s8, s8, s28, s28, s28],
        compiler_params=_cparams(("parallel", "parallel")),
        name="rwkv_prep",
    )(slabs, slabs, slabs, mu_slab, w0, a0, k_k, k_a, w2, a2, g2)


def _rwscan_kernel(rf_ref, vf_ref, naf_ref, kdf_ref, bf_ref, ldf_ref,
                   rb_ref, vb_ref, nab_ref, kdb_ref, bb_ref, ldb_ref, rk_ref,
                   yf_ref, yb_ref, st_ref, *, nc, group):
    @pl.when(pl.program_id(2) == 0)
    def _():
        st_ref[...] = jnp.zeros_like(st_ref)

    lane = lax.broadcasted_iota(I32, (CHUNK, LANES), 1)
    head0 = lane < RW_HEAD
    ri = lax.broadcasted_iota(I32, (LANES, LANES), 0)
    ci = lax.broadcasted_iota(I32, (LANES, LANES), 1)
    same = (ri >> 6) == (ci >> 6)
    tt = ri & (CHUNK - 1)
    ss = ci & (CHUNK - 1)
    eye = jnp.where(ri == ci, 1.0, 0.0).astype(F32)
    tr = lax.broadcasted_iota(I32, (CHUNK, CHUNK), 0)
    tc = lax.broadcasted_iota(I32, (CHUNK, CHUNK), 1)
    bd = _block_ones()
    rk = rk_ref[0]

    def stack(x):
        return jnp.concatenate([jnp.where(head0, x, 0.0), jnp.where(head0, 0.0, x)], axis=0)

    dirs = (
        (rf_ref, vf_ref, naf_ref, kdf_ref, bf_ref, ldf_ref, yf_ref),
        (rb_ref, vb_ref, nab_ref, kdb_ref, bb_ref, ldb_ref, yb_ref),
    )
    strict = (same & (ss < tt), same & (ss > tt))
    incl = (same & (ss <= tt), same & (ss >= tt))
    tri = (jnp.where(tc <= tr, 1.0, 0.0).astype(BF16), jnp.where(tc >= tr, 1.0, 0.0).astype(BF16))
    last = (CHUNK - 1, 0)
    all_insts = [(d, k if d == 0 else nc - 1 - k) for k in range(nc) for d in range(2)]

    def load(ref, d, ch, lead):
        sl = slice(ch * CHUNK, (ch + 1) * CHUNK)
        return ref[(0,) * lead + (sl, slice(None))]

    def state_free_part(insts):
        every = range(len(insts))
        r = [load(dirs[d][0], d, ch, 1) for d, ch in insts]
        v = [load(dirs[d][1], d, ch, 1) for d, ch in insts]
        na = [load(dirs[d][2], d, ch, 1) for d, ch in insts]
        kd = [load(dirs[d][3], d, ch, 2) for d, ch in insts]
        b = [load(dirs[d][4], d, ch, 2) for d, ch in insts]
        ld = [load(dirs[d][5], d, ch, 2) for d, ch in insts]
        tri3 = [jnp.concatenate([t, t, t], axis=1) for t in tri]
        c = [_dot(tri3[insts[i][0]], jnp.concatenate(_split3(ld[i]), axis=0)) for i in every]
        total = [c[i][last[insts[i][0]]:last[insts[i][0]] + 1] for i in every]
        e_nc = [jnp.exp(-c[i]) for i in every]
        e_tc = [jnp.exp(total[i] - c[i]) for i in every]
        a_t = [stack(na[i] * jnp.exp(c[i] - ld[i])).astype(BF16) for i in every]
        r_t = [stack(r[i] * jnp.exp(c[i])).astype(BF16) for i in every]
        v_s = [stack(v[i]).astype(BF16) for i in every]
        rhs = [jnp.concatenate([stack(b[i] * e_nc[i]), stack(kd[i] * e_nc[i])], axis=0).astype(BF16) for i in every]
        bk = [jnp.concatenate([stack(b[i] * e_tc[i]), stack(kd[i] * e_tc[i])], axis=0).astype(BF16) for i in every]
        p = [_dot_nt(jnp.concatenate([a_t[i], r_t[i]], axis=0), rhs[i]) for i in every]
        n_ab = [jnp.where(strict[insts[i][0]], p[i][:LANES, :LANES], 0.0) for i in every]
        a_ak = [jnp.where(strict[insts[i][0]], p[i][:LANES, LANES:], 0.0).astype(BF16) for i in every]
        p_rb = [jnp.where(incl[insts[i][0]], p[i][LANES:, :LANES], 0.0).astype(BF16) for i in every]
        p_rk = [jnp.where(incl[insts[i][0]], p[i][LANES:, LANES:], 0.0).astype(BF16) for i in every]
        x = [eye + n_ab[i] for i in every]
        nk = [n_ab[i].astype(BF16) for i in every]
        nk = [_dot(nk[i], nk[i]) for i in every]
        for _ in range(4):
            both = [_dot(nk[i].astype(BF16), jnp.concatenate([nk[i], x[i]], axis=1).astype(BF16)) for i in every]
            nk = [both[i][:, :LANES] for i in every]
            x = [x[i] + both[i][:, LANES:] for i in every]
        x = [x[i] + _dot(nk[i].astype(BF16), x[i].astype(BF16)) for i in every]
        w = [_dot(a_ak[i], v_s[i]) for i in every]
        au = [_dot(x[i].astype(BF16), jnp.concatenate([a_t[i], w[i].astype(BF16)], axis=1)) for i in every]
        rpp = [jnp.concatenate([r_t[i], p_rb[i], p_rk[i]], axis=1) for i in every]
        bonus = [_head_segsum(r[i] * kd[i] * rk, bd) * v[i] for i in every]
        u0_t = [au[i][:, LANES:].T for i in every]
        v_t = [v_s[i].astype(F32).T.astype(BF16) for i in every]
        return [dict(a_hat=au[i][:, :LANES].astype(BF16), u0_t=u0_t[i], v_t=v_t[i], rpp=rpp[i], bk=bk[i],
                     decay=jnp.exp(total[i]), bonus=bonus[i]) for i in every]

    pre = []
    for g in range(0, len(all_insts), group):
        pre += state_free_part(all_insts[g:g + group])
    st = [st_ref[0], st_ref[1]]
    for (d, ch), f in zip(all_insts, pre):
        st_b = st[d].astype(BF16)
        u_t = (_dot_nt(st_b, f["a_hat"]) + f["u0_t"]).astype(BF16)
        uv_t = jnp.concatenate([u_t, f["v_t"]], axis=1)
        y_t = _dot_nt(jnp.concatenate([st_b, uv_t], axis=1), f["rpp"])
        st[d] = st[d] * f["decay"] + _dot(uv_t, f["bk"])
        y = y_t.T
        dirs[d][6][0, ch * CHUNK:(ch + 1) * CHUNK, :] = y[:CHUNK] + y[CHUNK:] + f["bonus"]
    st_ref[0] = st[0]
    st_ref[1] = st[1]


def _rwscan(r, v, na, kd, b, ld, r_k, batch, seq):
    n = batch * seq
    nc = min(8, seq // CHUNK)
    tcs = CHUNK * nc
    nt = seq // tcs
    fwd = lambda bi, c, t: bi * nt + t
    bwd = lambda bi, c, t: bi * nt + nt - 1 - t
    s3 = lambda rows: pl.BlockSpec((1, tcs, LANES), lambda bi, c, t: (c, rows(bi, c, t), 0))
    s4 = lambda d, rows: pl.BlockSpec((1, 1, tcs, LANES), lambda bi, c, t: (d, c, rows(bi, c, t), 0))
    out = jax.ShapeDtypeStruct((8, n, LANES), F32)
    return pl.pallas_call(
        functools.partial(_rwscan_kernel, nc=nc, group=2 * nc),
        grid=(batch, 8, nt),
        in_specs=[s3(fwd), s3(fwd), s3(fwd), s4(0, fwd), s4(0, fwd), s4(0, fwd),
                  s3(bwd), s3(bwd), s3(bwd), s4(1, bwd), s4(1, bwd), s4(1, bwd),
                  pl.BlockSpec((1, 1, LANES), lambda bi, c, t: (c, 0, 0))],
        out_specs=[s3(fwd), s3(bwd)],
        out_shape=[out, out],
        scratch_shapes=[pltpu.VMEM((2, LANES, LANES), F32)],
        compiler_params=_cparams(("parallel", "parallel", "arbitrary")),
        name="rwkv_scan",
    )(r, v, na, kd, b, ld, r, v, na, kd, b, ld, r_k)


def _merge_kernel(x_ref, oatt_ref, yf_ref, yb_ref, g_ref, gates_ref, gb_ref, lng_ref, lnb_ref,
                  wba_ref, wbr_ref, wout_ref, h_ref):
    bd = _block_ones()
    orw = []
    for c in range(8):
        y = yf_ref[c] + yb_ref[c]
        mu = _head_segsum(y, bd) * (1.0 / RW_HEAD)
        yc = y - mu
        var = _head_segsum(yc * yc, bd) * (1.0 / RW_HEAD)
        yn = yc * lax.rsqrt(var + LNX_EPS) * lng_ref[c] + lnb_ref[c]
        orw.append((yn * g_ref[c]).astype(BF16))
    orw = jnp.concatenate(orw, axis=1)
    oatt = jnp.concatenate([oatt_ref[c] for c in range(8)], axis=1)
    ga = jnp.concatenate([gates_ref[c] for c in range(16)], axis=1).astype(F32) + gb_ref[0]
    gr = jnp.concatenate([gates_ref[16 + c] for c in range(16)], axis=1).astype(F32) + gb_ref[1]
    merged = _sigmoid(ga) * _dot(oatt, wba_ref[...]) + _sigmoid(gr) * _dot(orw, wbr_ref[...])
    h_ref[...] = x_ref[...] + _dot(merged.astype(BF16), wout_ref[...])


def _merge(x2d, oatt, yf, yb, g, slabs, gate_b, lng, lnb, wba, wbr, wout):
    n = x2d.shape[0]
    tm = min(256, n)
    const = lambda shape: pl.BlockSpec(shape, lambda i: (0,) * len(shape), pipeline_mode=pl.Buffered(1))
    s8 = pl.BlockSpec((8, tm, LANES), lambda i: (0, i, 0))
    return pl.pallas_call(
        _merge_kernel,
        grid=(n // tm,),
        in_specs=[
            pl.BlockSpec((tm, D_MODEL), lambda i: (i, 0)),
            s8, s8, s8, s8,
            pl.BlockSpec((32, tm, LANES), lambda i: (0, i, 0)),
            const((2, 1, D_MODEL)), const((8, 1, LANES)), const((8, 1, LANES)),
            const((ATT_WIDTH, D_MODEL)), const((RW_WIDTH, D_MODEL)), const((D_MODEL, D_MODEL)),
        ],
        out_specs=pl.BlockSpec((tm, D_MODEL), lambda i: (i, 0)),
        out_shape=jax.ShapeDtypeStruct((n, D_MODEL), F32),
        compiler_params=_cparams(("parallel",)),
        name="merge_outproj",
    )(x2d, oatt, yf, yb, g, slabs, gate_b, lng, lnb, wba, wbr, wout)


def _router_kernel(h_ref, g_ref, wr_ref, hn_ref, aff_ref):
    x = h_ref[...]
    ms = jnp.mean(x * x, axis=-1, keepdims=True)
    hn = x * lax.rsqrt(ms + NORM_EPS) * g_ref[...]
    hn_ref[...] = hn
    xh, xm, xl = _split3(hn)
    wh, wm, wl = _split3(wr_ref[...])
    logits = (_dot(xh, wh) + _dot(xh, wm) + _dot(xm, wh)
              + _dot(xh, wl) + _dot(xl, wh) + _dot(xm, wm))
    lt = logits.T[:N_EXPERTS]
    m = jnp.max(lt, axis=0, keepdims=True)
    e = jnp.exp(lt - m)
    aff_ref[...] = e / jnp.sum(e, axis=0, keepdims=True)


def _router(h2d, g, wr_pad):
    n = h2d.shape[0]
    tm = min(256, n)
    return pl.pallas_call(
        _router_kernel,
        grid=(n // tm,),
        in_specs=[
            pl.BlockSpec((tm, D_MODEL), lambda i: (i, 0)),
            pl.BlockSpec((1, D_MODEL), lambda i: (0, 0)),
            pl.BlockSpec((D_MODEL, LANES), lambda i: (0, 0)),
        ],
        out_specs=[
            pl.BlockSpec((tm, D_MODEL), lambda i: (i, 0)),
            pl.BlockSpec((N_EXPERTS, tm), lambda i: (0, i)),
        ],
        out_shape=[
            jax.ShapeDtypeStruct((n, D_MODEL), F32),
            jax.ShapeDtypeStruct((N_EXPERTS, n), F32),
        ],
        compiler_params=_cparams(("parallel",)),
        name="router",
    )(h2d, g, wr_pad)


def _select_kernel(aff_ref, incl_ref, tbl_ref, gval_ref, cnt_ref, slot_ref, *, cap, tt):
    bits = pltpu.bitcast(aff_ref[...], I32)
    nrow = bits.shape[1]

    def count(mask):
        c = jnp.sum(jnp.where(mask, 1, 0), axis=2, keepdims=True)
        return jnp.sum(c, axis=1, keepdims=True)

    def body(_, carry):
        lo, hi = carry
        mid = lo + ((hi - lo) >> 1)
        ok = count(bits >= mid) >= cap
        return jnp.where(ok, mid, lo), jnp.where(ok, hi, mid)

    lo0 = jnp.zeros((N_EXPERTS, 1, 1), I32)
    hi0 = jnp.full((N_EXPERTS, 1, 1), 0x7F800000, I32)
    thr, _ = lax.fori_loop(0, 31, body, (lo0, hi0))
    gt = bits > thr
    eq = bits == thr
    need = cap - count(gt)

    ri = lax.broadcasted_iota(I32, (LANES, LANES), 0)
    ci = lax.broadcasted_iota(I32, (LANES, LANES), 1)
    upper = jnp.where(ri <= ci, 1.0, 0.0).astype(BF16)
    rr = lax.broadcasted_iota(I32, (nrow, nrow), 0)
    rc = lax.broadcasted_iota(I32, (nrow, nrow), 1)
    lower_strict = jnp.where(rc < rr, 1.0, 0.0).astype(BF16)

    def incl_prefix(mask):
        x = jnp.where(mask, 1.0, 0.0).astype(BF16)
        incl = _dot(x.reshape(N_EXPERTS * nrow, LANES), upper).reshape(N_EXPERTS, nrow, LANES)
        tot = jnp.broadcast_to(incl[:, :, LANES - 1:LANES], incl.shape).astype(BF16)
        before = jnp.stack([_dot(lower_strict, tot[e]) for e in range(N_EXPERTS)], axis=0)
        return incl + before

    sel = gt | (eq & (incl_prefix(eq) - 1.0 < need.astype(F32)))
    incl_ref[...] = incl_prefix(sel).astype(I32)
    run = jnp.zeros((nrow, LANES), F32)
    for e in range(N_EXPERTS):
        slot_ref[e] = run
        run = run + jnp.where(sel[e], 1.0, 0.0)
    cnt_ref[...] = run.astype(I32)

    pf = lax.broadcasted_iota(I32, (1, cap), 1).astype(F32)
    jrow = lax.broadcasted_iota(I32, (nrow, cap), 0).astype(F32)
    lrow = lax.broadcasted_iota(I32, (LANES, cap), 0).astype(F32)

    def compact(e, carry):
        g = incl_ref[e].astype(F32)
        jsel = jnp.sum(jnp.where(g[:, LANES - 1:LANES] <= pf, 1.0, 0.0), axis=0, keepdims=True)
        onehot = jnp.where(jrow == jsel, 1.0, 0.0).astype(BF16)
        ghi = jnp.floor(g * (1.0 / 256.0))
        glo = g - 256.0 * ghi
        grow = 256.0 * _dot_tn(ghi.astype(BF16), onehot) + _dot_tn(glo.astype(BF16), onehot)
        lstar = jnp.sum(jnp.where(grow <= pf, 1.0, 0.0), axis=0, keepdims=True)
        lsel = lrow == lstar
        ah, am, al = _split3(aff_ref[e])
        arow = _dot_tn(ah, onehot) + _dot_tn(am, onehot) + _dot_tn(al, onehot)
        gval_ref[pl.ds(e, 1), :] = jnp.sum(jnp.where(lsel, arow, 0.0), axis=0, keepdims=True)
        krow = _dot_tn(slot_ref[e].astype(BF16), onehot)
        kk = jnp.sum(jnp.where(lsel, krow, 0.0), axis=0, keepdims=True)
        tok = (jsel * float(LANES) + lstar).astype(I32)
        dest = kk.astype(I32) * tt + (tok & (tt - 1))
        tbl_ref[pl.ds(e, 1), :] = tok | (dest << 16)
        return carry

    lax.fori_loop(0, N_EXPERTS, compact, 0)


def _select(aff3, cap, tt):
    nrow = aff3.shape[1]
    return pl.pallas_call(
        functools.partial(_select_kernel, cap=cap, tt=tt),
        out_shape=[
            jax.ShapeDtypeStruct(aff3.shape, I32),
            jax.ShapeDtypeStruct((N_EXPERTS, cap), I32),
            jax.ShapeDtypeStruct((N_EXPERTS, cap), F32),
            jax.ShapeDtypeStruct((nrow, LANES), I32),
        ],
        scratch_shapes=[pltpu.VMEM((N_EXPERTS, nrow, LANES), F32)],
        compiler_params=pltpu.CompilerParams(vmem_limit_bytes=VMEM_LIMIT),
        name="expert_select",
    )(aff3)


def _ffn_kernel(tbl_ref, hn_hbm, gval_ref, wg_ref, wu_ref, wd_ref, out_ref, xbuf, sem, *, tc, nt):
    step = pl.program_id(0) * nt + pl.program_id(1)
    last = N_EXPERTS * nt - 1
    slot = step % FFN_SLOTS
    ahead = FFN_SLOTS - 1

    def wait(slt):
        pltpu.make_async_copy(hn_hbm.at[pl.ds(0, tc)], xbuf.at[slt], sem.at[slt]).wait()

    @pl.when(step == 0)
    def _():
        for t in range(ahead):
            def body(i, carry, t=t):
                tok = tbl_ref[t * tc + i] & 0xFFFF
                pltpu.make_async_copy(hn_hbm.at[pl.ds(tok, 1)], xbuf.at[t, pl.ds(i, 1)], sem.at[t]).start()
                return carry
            lax.fori_loop(0, tc, body, 0, unroll=8)

    wait(slot)
    xe = xbuf[slot].astype(BF16)
    a = _dot(xe, wg_ref[0])
    u = _dot(xe, wu_ref[0])
    hmid = (a * _sigmoid(a) * u).astype(BF16)
    nxt = jnp.minimum(step + ahead, last)
    into = (step + ahead) % FFN_SLOTS
    for i in range(tc):
        tok = tbl_ref[nxt * tc + i] & 0xFFFF
        pltpu.make_async_copy(hn_hbm.at[pl.ds(tok, 1)], xbuf.at[into, pl.ds(i, 1)], sem.at[into]).start()
    out_ref[...] = _dot(hmid, wd_ref[0]) * gval_ref[0]

    @pl.when(step == last)
    def _():
        for t in range(1, FFN_SLOTS):
            wait((last + t) % FFN_SLOTS)


def _expert_ffn(tbl_flat, hn, gval, wg, wu, wd, cap):
    tc = min(256, cap)
    nt = cap // tc
    grid_spec = pltpu.PrefetchScalarGridSpec(
        num_scalar_prefetch=1,
        grid=(N_EXPERTS, nt),
        in_specs=[
            pl.BlockSpec(memory_space=pl.ANY),
            pl.BlockSpec((1, tc, 1), lambda e, j, idx: (e * nt + j, 0, 0)),
            pl.BlockSpec((1, D_MODEL, EXPERT_FF), lambda e, j, idx: (e, 0, 0)),
            pl.BlockSpec((1, D_MODEL, EXPERT_FF), lambda e, j, idx: (e, 0, 0)),
            pl.BlockSpec((1, EXPERT_FF, D_MODEL), lambda e, j, idx: (e, 0, 0)),
        ],
        out_specs=pl.BlockSpec((tc, D_MODEL), lambda e, j, idx: (e * nt + j, 0)),
        scratch_shapes=[pltpu.VMEM((FFN_SLOTS, tc, D_MODEL), F32), pltpu.SemaphoreType.DMA((FFN_SLOTS,))],
    )
    return pl.pallas_call(
        functools.partial(_ffn_kernel, tc=tc, nt=nt),
        grid_spec=grid_spec,
        out_shape=jax.ShapeDtypeStruct((N_EXPERTS * cap, D_MODEL), F32),
        compiler_params=_cparams(("arbitrary", "arbitrary")),
        name="expert_ffn",
    )(tbl_flat, hn, gval.reshape(N_EXPERTS * nt, tc, 1), wg, wu, wd)


def _combine_kernel(tbl_ref, p0_ref, km_ref, h_ref, cnt_ref, g_ref, ye_hbm, out_ref, stage, sem,
                    *, tt, cap, ntile):
    tile = pl.program_id(0)
    slot = tile % COMBINE_SLOTS

    def issue(tl, slt):
        for e in range(N_EXPERTS):
            p0 = p0_ref[e * (ntile + 1) + tl]
            cnt = p0_ref[e * (ntile + 1) + tl + 1] - p0

            def fetch(q, e=e, p0=p0):
                row = e * cap + p0 + q
                pltpu.make_async_copy(ye_hbm.at[pl.ds(row, 1)], stage.at[slt, pl.ds(tbl_ref[row] >> 16, 1)],
                                      sem.at[slt]).start()

            def four(j, carry, fetch=fetch):
                for u in range(4):
                    fetch(4 * j + u)
                return carry

            def one(q, carry, fetch=fetch, cnt=cnt):
                fetch((cnt & ~3) + q)
                return carry

            lax.fori_loop(0, cnt >> 2, four, 0)
            lax.fori_loop(0, cnt & 3, one, 0)

    ahead = COMBINE_SLOTS - 1

    @pl.when(tile == 0)
    def _():
        for tl in range(min(ahead, ntile)):
            issue(tl, tl)

    @pl.when(tile + ahead < ntile)
    def _():
        issue(tile + ahead, (tile + ahead) % COMBINE_SLOTS)

    def wait_rows(nrows):
        def body(q, carry):
            pltpu.make_async_copy(ye_hbm.at[pl.ds(0, nrows)], stage.at[slot, pl.ds(0, nrows)], sem.at[slot]).wait()
            return carry
        return body

    total = km_ref[2 * tile + 1]
    lax.fori_loop(0, total >> 3, wait_rows(8), 0)
    lax.fori_loop(0, total & 7, wait_rows(1), 0)

    cnt = cnt_ref[...]
    parts = []
    for cs in (slice(c * 512, (c + 1) * 512) for c in range(D_MODEL // 512)):
        def add(k, acc, cs=cs):
            rows = stage[slot, pl.ds(pl.multiple_of(k * tt, tt), tt), cs]
            return acc + jnp.where(cnt > k, rows, 0.0)
        parts.append(lax.fori_loop(0, km_ref[2 * tile], add, h_ref[:, cs]))
    acc = jnp.concatenate(parts, axis=1)
    ms = jnp.mean(acc * acc, axis=-1, keepdims=True)
    out_ref[...] = acc * lax.rsqrt(ms + NORM_EPS) * g_ref[...]


def _combine(tbl_flat, p0_flat, km_flat, h2d, cnt_tok, g, yexp, cap, tt):
    n = h2d.shape[0]
    ntile = n // tt
    grid_spec = pltpu.PrefetchScalarGridSpec(
        num_scalar_prefetch=3,
        grid=(ntile,),
        in_specs=[
            pl.BlockSpec((tt, D_MODEL), lambda i, a, b, c: (i, 0)),
            pl.BlockSpec((tt, 1), lambda i, a, b, c: (i, 0)),
            pl.BlockSpec((1, D_MODEL), lambda i, a, b, c: (0, 0)),
            pl.BlockSpec(memory_space=pl.ANY),
        ],
        out_specs=pl.BlockSpec((tt, D_MODEL), lambda i, a, b, c: (i, 0)),
        scratch_shapes=[pltpu.VMEM((COMBINE_SLOTS, N_EXPERTS * tt, D_MODEL), F32),
                        pltpu.SemaphoreType.DMA((COMBINE_SLOTS,))],
    )
    return pl.pallas_call(
        functools.partial(_combine_kernel, tt=tt, cap=cap, ntile=ntile),
        grid_spec=grid_spec,
        out_shape=jax.ShapeDtypeStruct((n, D_MODEL), F32),
        compiler_params=_cparams(("arbitrary",)),
        name="moe_combine",
    )(tbl_flat, p0_flat, km_flat, h2d, cnt_tok, g, yexp)


def _trunk(x, p, w_slab, mu_slab, rwp, moe_w, slopes):
    batch, seq, _ = x.shape
    n = batch * seq
    x2d = x.reshape(n, D_MODEL)
    slabs = _inproj(x2d, p["norm_mix_g"], w_slab)
    oatt = _attention(slabs, slopes, p["lambda_q1"], p["lambda_k1"], p["lambda_q2"], p["lambda_k2"],
                      p["subln_g"], batch, seq)
    r, v, na, g, kd, b, ld = _rwprep(slabs, mu_slab, *rwp[:7], batch, seq)
    yf, yb = _rwscan(r, v, na, kd, b, ld, rwp[7], batch, seq)
    h = _merge(x2d, oatt, yf, yb, g, slabs, p["gate_b"][0].reshape(2, 1, D_MODEL), rwp[8], rwp[9],
               moe_w["wba"], moe_w["wbr"], moe_w["wout"])

    cap = max(1, CAPACITY_FACTOR * n // N_EXPERTS)
    hn, aff = _router(h, p["norm_ffn_g"], moe_w["wr"])
    aff3 = aff.reshape(N_EXPERTS, n // LANES, LANES)
    assert n <= 1 << 16
    tt = 64
    incl, tbl, gval, cnt = _select(aff3, cap, tt)
    ntile = n // tt
    ends = incl.reshape(N_EXPERTS, n)[:, tt - 1::tt]
    p0 = jnp.concatenate([jnp.zeros((N_EXPERTS, 1), I32), ends], axis=1)
    ct = cnt.reshape(ntile, tt)
    km = jnp.stack([jnp.max(ct, axis=1), jnp.sum(ct, axis=1)], axis=1)
    tbl_flat = tbl.reshape(-1)
    yexp = _expert_ffn(tbl_flat, hn, gval, moe_w["wg"], moe_w["wu"], moe_w["wd"], cap)
    y = _combine(tbl_flat, p0.reshape(-1), km.reshape(-1), h, cnt.reshape(n, 1),
                 p["norm_final_g"].reshape(1, D_MODEL), yexp, cap, tt)
    return y.reshape(batch, seq, D_MODEL)


def kernel(x_prompt, x_sample, norm_mix_g, w_in, shift_mu, lambda_q1, lambda_k1, lambda_q2, lambda_k2, subln_g, rw_w0, rw_w2, rw_a0, rw_a2, rw_g2, rw_k_k, rw_k_a, rw_r_k, lnx_g, lnx_b, gate_b, w_br_att, w_br_rw, w_out, norm_ffn_g, w_router, w_gate_e, w_up_e, w_down_e, norm_final_g):
    p = dict(norm_mix_g=norm_mix_g, lambda_q1=lambda_q1, lambda_k1=lambda_k1, lambda_q2=lambda_q2,
             lambda_k2=lambda_k2, subln_g=subln_g, rw_w0=rw_w0, rw_w2=rw_w2, rw_a0=rw_a0, rw_a2=rw_a2,
             rw_g2=rw_g2, rw_k_k=rw_k_k, rw_k_a=rw_k_a, rw_r_k=rw_r_k, lnx_g=lnx_g, lnx_b=lnx_b,
             gate_b=gate_b, norm_ffn_g=norm_ffn_g, norm_final_g=norm_final_g)
    w_slab, mu_slab = _prep_in_weights(w_in[0], shift_mu[0])
    rwp = _prep_rw_params(p)
    moe_w = dict(
        wba=w_br_att[0].astype(BF16), wbr=w_br_rw[0].astype(BF16), wout=w_out[0].astype(BF16),
        wr=jnp.pad(w_router[0], ((0, 0), (0, LANES - N_EXPERTS))),
        wg=w_gate_e[0].astype(BF16), wu=w_up_e[0].astype(BF16), wd=w_down_e[0].astype(BF16))
    slopes = jnp.asarray([2.0 ** (-8.0 * (i + 1) / ATT_HEADS) for i in range(ATT_HEADS)], F32)
    return (_trunk(x_prompt, p, w_slab, mu_slab, rwp, moe_w, slopes),
            _trunk(x_sample, p, w_slab, mu_slab, rwp, moe_w, slopes))
```

```python
import functools
import math

import jax
import jax.numpy as jnp
from jax import lax
from jax.experimental import pallas as pl
from jax.experimental.pallas import tpu as pltpu

F32 = jnp.float32
BF16 = jnp.bfloat16
I32 = jnp.int32

D_MODEL = 2048
ATT_HEADS = 8
ATT_HEAD_DIM = 64
ATT_WIDTH = ATT_HEADS * 2 * ATT_HEAD_DIM
RW_HEAD = 64
RW_WIDTH = 1024
DECAY_LORA = 96
ICLR_LORA = 96
GATE_LORA = 256
SHIFT_WIDTH = 3 * RW_WIDTH + DECAY_LORA + ICLR_LORA + GATE_LORA
N_EXPERTS = 16
CAPACITY_FACTOR = 2
EXPERT_FF = 1024
NORM_EPS = 1e-6
SUBLN_EPS = 1e-5
LNX_EPS = 64e-5
LAM_INIT = 0.8 - 0.6 * math.exp(-0.3 * 0)

LANES = 128
VMEM_LIMIT = 56 * 1024 * 1024

SL_GATE_ATT, SL_GATE_RW = 0, 16
SL_ATT_Q, SL_ATT_K, SL_ATT_V = 32, 40, 48
SL_RW = 56
N_RW_SLABS = 28
N_SLABS = 84
CHUNK = 64
COMBINE_SLOTS = 2
FFN_SLOTS = 3


def _cparams(sem):
    return pltpu.CompilerParams(dimension_semantics=sem, vmem_limit_bytes=VMEM_LIMIT)


def _sigmoid(x):
    return 1.0 / (1.0 + jnp.exp(-x))


def _split3(x):
    hi = x.astype(BF16)
    r1 = x - hi.astype(F32)
    mid = r1.astype(BF16)
    lo = (r1 - mid.astype(F32)).astype(BF16)
    return hi, mid, lo


def _dot(a, b):
    return jnp.dot(a, b, preferred_element_type=F32)


def _dot_nt(a, b):
    return lax.dot_general(a, b, (((1,), (1,)), ((), ())), preferred_element_type=F32)


def _dot_tn(a, b):
    return lax.dot_general(a, b, (((0,), (0,)), ((), ())), preferred_element_type=F32)


def _dot_f32(a_bf16_exact, x):
    hi, mid, lo = _split3(x)
    return _dot(a_bf16_exact, hi) + _dot(a_bf16_exact, mid) + _dot(a_bf16_exact, lo)


def _rearrange_in_cols(a):
    att = a[..., :3 * ATT_WIDTH]
    zr = a[..., 3 * ATT_WIDTH:3 * ATT_WIDTH + SHIFT_WIDTH]
    gates = a[..., 3 * ATT_WIDTH + SHIFT_WIDTH:]
    o3 = 3 * RW_WIDTH
    o4 = o3 + DECAY_LORA
    o5 = o4 + ICLR_LORA
    pad = [(0, 0)] * (a.ndim - 1)
    lw = jnp.pad(zr[..., o3:o4], pad + [(0, LANES - DECAY_LORA)])
    la = jnp.pad(zr[..., o4:o5], pad + [(0, LANES - ICLR_LORA)])
    return jnp.concatenate([gates, att, zr[..., :o3], lw, la, zr[..., o5:]], axis=-1)


def _prep_in_weights(w_in, shift_mu):
    w_slab = _rearrange_in_cols(w_in).astype(BF16)
    mu_full = jnp.pad(shift_mu, ((0, 0), (3 * ATT_WIDTH, 2 * D_MODEL)))
    mu_slab = _rearrange_in_cols(mu_full)[:, SL_RW * LANES:]
    return w_slab, mu_slab.reshape(2, 1, N_RW_SLABS * LANES)


def _inproj_kernel(x_ref, g_ref, w_ref, o_ref, xn_ref, *, n_out_slabs):
    @pl.when(pl.program_id(1) == 0)
    def _():
        x = x_ref[...]
        ms = jnp.mean(x * x, axis=-1, keepdims=True)
        xn_ref[...] = (x * lax.rsqrt(ms + NORM_EPS) * g_ref[...]).astype(BF16)

    acc = _dot(xn_ref[...], w_ref[...])
    for c in range(n_out_slabs):
        o_ref[c] = acc[:, c * LANES:(c + 1) * LANES].astype(BF16)


def _inproj(x2d, g, w_slab):
    n = x2d.shape[0]
    tm = min(1024, n)
    tn = 1536
    n_out_slabs = tn // LANES
    grid = (n // tm, (N_SLABS * LANES) // tn)
    return pl.pallas_call(
        functools.partial(_inproj_kernel, n_out_slabs=n_out_slabs),
        grid=grid,
        in_specs=[
            pl.BlockSpec((tm, D_MODEL), lambda i, j: (i, 0)),
            pl.BlockSpec((1, D_MODEL), lambda i, j: (0, 0)),
            pl.BlockSpec((D_MODEL, tn), lambda i, j: (0, j)),
        ],
        out_specs=pl.BlockSpec((n_out_slabs, tm, LANES), lambda i, j: (j, i, 0)),
        out_shape=jax.ShapeDtypeStruct((N_SLABS, n, LANES), BF16),
        scratch_shapes=[pltpu.VMEM((tm, D_MODEL), BF16)],
        compiler_params=_cparams(("parallel", "arbitrary")),
        name="inproj",
    )(x2d, g, w_slab)


def _attn_kernel(slopes_ref, lq1_ref, lk1_ref, lq2_ref, lk2_ref, subg_ref, q_ref, k_ref, v_ref, o_ref,
                 kt1_ref, kt2_ref, vaug_ref, *, seq, tq):
    h = pl.program_id(1)
    qi = pl.program_id(2)
    slope = slopes_ref[h]
    q0 = pl.multiple_of(qi * tq, tq)
    view = pl.ds(q0, seq)

    @pl.when(qi == 0)
    def _():
        kt = k_ref[0].astype(F32).T
        row = lax.broadcasted_iota(I32, kt.shape, 0)
        k1 = jnp.where(row < ATT_HEAD_DIM, kt, 0.0).astype(BF16)
        k2 = jnp.where(row >= ATT_HEAD_DIM, kt, 0.0).astype(BF16)
        lane = lax.broadcasted_iota(I32, (seq, LANES), 1)
        va = jnp.concatenate([v_ref[0], jnp.where(lane == 0, 1.0, 0.0).astype(BF16)], axis=1)
        for half in (slice(0, seq), slice(seq, 2 * seq)):
            kt1_ref[:, half] = k1
            kt2_ref[:, half] = k2
            vaug_ref[half, :] = va

    col = lax.broadcasted_iota(I32, (16, seq), 1)
    r16 = lax.broadcasted_iota(I32, (16, seq), 0)
    wrapped = col + q0 >= seq
    jp = jnp.where(wrapped, col - seq, col)
    sigma = jnp.where(col < tq, 0.0, jnp.where(wrapped, -1.0, 1.0)).astype(F32)
    jh = (jp >> 8).astype(F32)
    jl = (jp & 255).astype(F32)
    feat = jnp.where(r16 <= 1, sigma,
                     jnp.where(r16 == 2, -sigma * (slope * 256.0) * jh,
                               jnp.where(r16 == 3, -sigma * slope * jl, 0.0))).astype(BF16)
    kt1_ref[ATT_HEAD_DIM:ATT_HEAD_DIM + 16, view] = feat
    kt2_ref[0:16, view] = feat

    q = q_ref[0].astype(F32) * (ATT_HEAD_DIM ** -0.5)
    lane = lax.broadcasted_iota(I32, (tq, LANES), 1)
    ip = lax.broadcasted_iota(I32, (tq, LANES), 0)
    ih = slope * (ip & ~255).astype(F32)
    il = slope * (ip & 255).astype(F32)

    def query_side(fl):
        return jnp.where(fl == 0, ih, jnp.where(fl == 1, il, jnp.where(fl <= 3, 1.0, 0.0)))

    lhs1 = jnp.where(lane < ATT_HEAD_DIM, q, query_side(lane - ATT_HEAD_DIM)).astype(BF16)
    lhs2 = jnp.where(lane >= ATT_HEAD_DIM, q, query_side(lane)).astype(BF16)

    di = lax.broadcasted_iota(I32, (tq, tq), 0)
    dj = lax.broadcasted_iota(I32, (tq, tq), 1)
    diag_bias = -slope * jnp.abs(di - dj).astype(F32)

    def weights(lhs, kt_ref):
        s = _dot(lhs, kt_ref[:, view])
        s = jnp.concatenate([s[:, :tq] + diag_bias, s[:, tq:]], axis=1)
        m = jnp.max(s, axis=-1, keepdims=True)
        return jnp.exp(s - m).astype(BF16)

    e = jnp.concatenate([weights(lhs1, kt1_ref), weights(lhs2, kt2_ref)], axis=0)
    oa = _dot(e, vaug_ref[view, :])
    o1 = oa[:tq, :LANES] / oa[:tq, LANES:LANES + 1]
    o2 = oa[tq:, :LANES] / oa[tq:, LANES:LANES + 1]
    lam = (jnp.exp(jnp.sum(lq1_ref[...] * lk1_ref[...], keepdims=True))
           - jnp.exp(jnp.sum(lq2_ref[...] * lk2_ref[...], keepdims=True)) + LAM_INIT)
    out = o1 - lam * o2
    ms = jnp.mean(out * out, axis=-1, keepdims=True)
    y = out * lax.rsqrt(ms + SUBLN_EPS) * subg_ref[...]
    o_ref[0] = (y * (1.0 - LAM_INIT)).astype(BF16)


def _attention(slabs, slopes, lq1, lk1, lq2, lk2, subg, batch, seq):
    n = batch * seq
    tq = 512
    nq = seq // tq
    vec = lambda: pl.BlockSpec((1, ATT_HEAD_DIM), lambda b, h, i: (0, 0))
    return pl.pallas_call(
        functools.partial(_attn_kernel, seq=seq, tq=tq),
        grid=(batch, ATT_HEADS, nq),
        in_specs=[
            pl.BlockSpec(memory_space=pltpu.SMEM),
            vec(), vec(), vec(), vec(),
            pl.BlockSpec((1, LANES), lambda b, h, i: (0, 0)),
            pl.BlockSpec((1, tq, LANES), lambda b, h, i: (SL_ATT_Q + h, b * nq + i, 0)),
            pl.BlockSpec((1, seq, LANES), lambda b, h, i: (SL_ATT_K + h, b, 0)),
            pl.BlockSpec((1, seq, LANES), lambda b, h, i: (SL_ATT_V + h, b, 0)),
        ],
        out_specs=pl.BlockSpec((1, tq, LANES), lambda b, h, i: (h, b * nq + i, 0)),
        out_shape=jax.ShapeDtypeStruct((ATT_HEADS, n, LANES), BF16),
        scratch_shapes=[
            pltpu.VMEM((LANES, 2 * seq), BF16),
            pltpu.VMEM((LANES, 2 * seq), BF16),
            pltpu.VMEM((2 * seq, 2 * LANES), BF16),
        ],
        compiler_params=_cparams(("parallel", "parallel", "arbitrary")),
        name="diff_attn",
    )(slopes, lq1, lk1, lq2, lk2, subg, slabs, slabs, slabs)


def _head_segsum(x, bd):
    hi, mid, lo = _split3(x)
    return _dot(hi, bd) + _dot(mid, bd) + _dot(lo, bd)


def _block_ones():
    ri = lax.broadcasted_iota(I32, (LANES, LANES), 0)
    ci = lax.broadcasted_iota(I32, (LANES, LANES), 1)
    return jnp.where((ri >> 6) == (ci >> 6), 1.0, 0.0).astype(BF16)


def _rwprep_kernel(main_ref, prev_ref, next_ref, mu_ref, w0_ref, a0_ref, kk_ref, ka_ref, w2_ref, a2_ref,
                   g2_ref, rva_ref, g_ref, dirp_ref, *, nt):
    i = pl.program_id(1)
    t = main_ref.shape[1]
    hb = prev_ref.shape[1]
    wide = lambda ref: jnp.concatenate([ref[c] for c in range(N_RW_SLABS)], axis=1)
    zb = wide(main_ref)
    prev = jnp.where(i > 0, wide(prev_ref), jnp.zeros((), BF16))
    nxt = jnp.where(i < nt - 1, wide(next_ref), jnp.zeros((), BF16))
    halo = jnp.concatenate([prev, zb, nxt], axis=0)
    ri = lax.broadcasted_iota(I32, (t, t + 2 * hb), 0)
    ci = lax.broadcasted_iota(I32, (t, t + 2 * hb), 1)
    zp = _dot(jnp.where(ci == ri + hb - 1, 1.0, 0.0).astype(BF16), halo)
    zn = _dot(jnp.where(ci == ri + hb + 1, 1.0, 0.0).astype(BF16), halo)
    z = zb.astype(F32)
    z = z + mu_ref[0] * (zp - z) + mu_ref[1] * (zn - z)
    slab = lambda s: z[:, s * LANES:(s + 1) * LANES]

    xw = jnp.tanh(slab(24)).astype(BF16)
    xa = slab(25).astype(BF16)
    xg = _sigmoid(z[:, 26 * LANES:28 * LANES]).astype(BF16)
    g_full = _dot(xg, g2_ref[...])
    lw = [_dot(xw, w2_ref[d]) for d in range(2)]
    la = [_dot(xa, a2_ref[d]) for d in range(2)]
    bd = _block_ones()
    for c in range(8):
        cs = slice(c * LANES, (c + 1) * LANES)
        kc = slab(8 + c)
        kk = kc * kk_ref[c]
        nrm = jnp.sqrt(_head_segsum(kk * kk, bd))
        kk = kk / jnp.maximum(nrm, 1e-12)
        rva_ref[0, c] = slab(c)
        rva_ref[1, c] = slab(16 + c)
        rva_ref[2, c] = -kk
        g_ref[c] = g_full[:, cs]
        for d in range(2):
            dirp_ref[d, 2, c] = -math.exp(-0.5) * _sigmoid(w0_ref[d, c] + lw[d][:, cs])
            asig = _sigmoid(a0_ref[d, c] + la[d][:, cs])
            dirp_ref[d, 0, c] = kc * (1.0 + (asig - 1.0) * ka_ref[c])
            dirp_ref[d, 1, c] = kk * asig


def _prep_rw_params(p):
    vec = lambda a: a.reshape(a.shape[:-1] + (8, 1, LANES))
    pad_rows = lambda a: jnp.pad(a, ((0, 0), (0, LANES - a.shape[1]), (0, 0))).astype(BF16)
    return (vec(p["rw_w0"][0]), vec(p["rw_a0"][0]), vec(p["rw_k_k"][0]), vec(p["rw_k_a"][0]),
            pad_rows(p["rw_w2"][0]), pad_rows(p["rw_a2"][0]), p["rw_g2"][0].astype(BF16),
            vec(p["rw_r_k"][0].reshape(RW_WIDTH)), vec(p["lnx_g"][0]), vec(p["lnx_b"][0]))


def _rwprep(slabs, mu_slab, w0, a0, k_k, k_a, w2, a2, g2, batch, seq):
    n = batch * seq
    t = 256
    nt = seq // t
    hb = 16
    full = lambda shape: pl.BlockSpec(shape, lambda b, i: (0,) * len(shape))
    rows = lambda b, i: b * nt + i
    out_specs = [pl.BlockSpec((3, 8, t, LANES), lambda b, i: (0, 0, rows(b, i), 0)),
                 pl.BlockSpec((8, t, LANES), lambda b, i: (0, rows(b, i), 0)),
                 pl.BlockSpec((2, 3, 8, t, LANES), lambda b, i: (0, 0, 0, rows(b, i), 0))]
    out_shape = [jax.ShapeDtypeStruct((3, 8, n, LANES), F32), jax.ShapeDtypeStruct((8, n, LANES), F32),
                 jax.ShapeDtypeStruct((2, 3, 8, n, LANES), F32)]
    rw_blk = SL_RW // N_RW_SLABS
    return pl.pallas_call(
        functools.partial(_rwprep_kernel, nt=nt),
        grid=(batch, nt),
        in_specs=[
            pl.BlockSpec((N_RW_SLABS, t, LANES), lambda b, i: (rw_blk, rows(b, i), 0)),
            pl.BlockSpec((N_RW_SLABS, hb, LANES),
                         lambda b, i: (rw_blk, jnp.maximum((b * seq + i * t) // hb - 1, 0), 0)),
            pl.BlockSpec((N_RW_SLABS, hb, LANES),
                         lambda b, i: (rw_blk, jnp.minimum((b * seq + (i + 1) * t) // hb, n // hb - 1), 0)),
            full((2, 1, N_RW_SLABS * LANES)),
            full((2, 8, 1, LANES)), full((2, 8, 1, LANES)), full((8, 1, LANES)), full((8, 1, LANES)),
            full((2, LANES, RW_WIDTH)), full((2, LANES, RW_WIDTH)), full((GATE_LORA, RW_WIDTH)),
        ],
        out_specs=out_specs,
        out_shape=out_shape,
        compiler_params=_cparams(("parallel", "parallel")),
        name="rwkv_prep",
    )(slabs, slabs, slabs, mu_slab, w0, a0, k_k, k_a, w2, a2, g2)


def _rwscan_kernel(rvaf_ref, dirf_ref, rvab_ref, dirb_ref, rk_ref, yf_ref, yb_ref, st_ref, *, nc, group):
    @pl.when(pl.program_id(2) == 0)
    def _():
        st_ref[...] = jnp.zeros_like(st_ref)

    lane = lax.broadcasted_iota(I32, (CHUNK, LANES), 1)
    head0 = lane < RW_HEAD
    ri = lax.broadcasted_iota(I32, (LANES, LANES), 0)
    ci = lax.broadcasted_iota(I32, (LANES, LANES), 1)
    same = (ri >> 6) == (ci >> 6)
    tt = ri & (CHUNK - 1)
    ss = ci & (CHUNK - 1)
    eye = jnp.where(ri == ci, 1.0, 0.0).astype(F32)
    tr = lax.broadcasted_iota(I32, (CHUNK, CHUNK), 0)
    tc = lax.broadcasted_iota(I32, (CHUNK, CHUNK), 1)
    bd = _block_ones()
    rk = rk_ref[0]

    def stack(x):
        return jnp.concatenate([jnp.where(head0, x, 0.0), jnp.where(head0, 0.0, x)], axis=0)

    dirs = ((rvaf_ref, dirf_ref, yf_ref), (rvab_ref, dirb_ref, yb_ref))
    strict = (same & (ss < tt), same & (ss > tt))
    incl = (same & (ss <= tt), same & (ss >= tt))
    tri = (jnp.where(tc <= tr, 1.0, 0.0).astype(BF16), jnp.where(tc >= tr, 1.0, 0.0).astype(BF16))
    last = (CHUNK - 1, 0)
    all_insts = [(d, k if d == 0 else nc - 1 - k) for k in range(nc) for d in range(2)]

    def load(which, j, insts):
        lead = (j, 0) if which == 0 else (0, j, 0)
        return [dirs[d][which][lead + (slice(ch * CHUNK, (ch + 1) * CHUNK), slice(None))] for d, ch in insts]

    def state_free_part(insts, out):
        every = range(len(insts))
        r, v, na = (load(0, j, insts) for j in range(3))
        kd, b, ld = (load(1, j, insts) for j in range(3))
        tri3 = [jnp.concatenate([t, t, t], axis=1) for t in tri]
        c = [_dot(tri3[insts[i][0]], jnp.concatenate(_split3(ld[i]), axis=0)) for i in every]
        total = [c[i][last[insts[i][0]]:last[insts[i][0]] + 1] for i in every]
        yield
        e_nc = [jnp.exp(-c[i]) for i in every]
        e_tc = [jnp.exp(total[i] - c[i]) for i in every]
        a_t = [stack(na[i] * jnp.exp(c[i] - ld[i])).astype(BF16) for i in every]
        r_t = [stack(r[i] * jnp.exp(c[i])).astype(BF16) for i in every]
        v_s = [stack(v[i]).astype(BF16) for i in every]
        rhs = [jnp.concatenate([stack(b[i] * e_nc[i]), stack(kd[i] * e_nc[i])], axis=0).astype(BF16) for i in every]
        bk = [jnp.concatenate([stack(b[i] * e_tc[i]), stack(kd[i] * e_tc[i])], axis=0).astype(BF16) for i in every]
        yield
        p = [_dot_nt(jnp.concatenate([a_t[i], r_t[i]], axis=0), rhs[i]) for i in every]
        yield
        n_ab = [jnp.where(strict[insts[i][0]], p[i][:LANES, :LANES], 0.0) for i in every]
        a_ak = [jnp.where(strict[insts[i][0]], p[i][:LANES, LANES:], 0.0).astype(BF16) for i in every]
        p_rb = [jnp.where(incl[insts[i][0]], p[i][LANES:, :LANES], 0.0).astype(BF16) for i in every]
        p_rk = [jnp.where(incl[insts[i][0]], p[i][LANES:, LANES:], 0.0).astype(BF16) for i in every]
        yield
        x = [eye + n_ab[i] for i in every]
        nk = [n_ab[i].astype(BF16) for i in every]
        nk = [_dot(nk[i], nk[i]) for i in every]
        for _ in range(4):
            yield
            both = [_dot(nk[i].astype(BF16), jnp.concatenate([nk[i], x[i]], axis=1).astype(BF16)) for i in every]
            nk = [both[i][:, :LANES] for i in every]
            x = [x[i] + both[i][:, LANES:] for i in every]
        yield
        x = [x[i] + _dot(nk[i].astype(BF16), x[i].astype(BF16)) for i in every]
        w = [_dot(a_ak[i], v_s[i]) for i in every]
        yield
        au = [_dot(x[i].astype(BF16), jnp.concatenate([a_t[i], w[i].astype(BF16)], axis=1)) for i in every]
        yield
        rpp = [jnp.concatenate([r_t[i], p_rb[i], p_rk[i]], axis=1) for i in every]
        bonus = [_head_segsum(r[i] * kd[i] * rk, bd) * v[i] for i in every]
        u0_t = [au[i][:, LANES:].T for i in every]
        v_t = [v_s[i].astype(F32).T.astype(BF16) for i in every]
        out.extend(dict(a_hat=au[i][:, :LANES].astype(BF16), u0_t=u0_t[i], v_t=v_t[i], rpp=rpp[i], bk=bk[i],
                        decay=jnp.exp(total[i]), bonus=bonus[i]) for i in every)

    groups = [all_insts[g:g + group] for g in range(0, len(all_insts), group)]
    pre = [[] for _ in groups]
    parts = [state_free_part(gr, pre[gi]) for gi, gr in enumerate(groups)]
    for _ in parts[0]:
        pass
    st = [st_ref[0], st_ref[1]]
    for gi, gr in enumerate(groups):
        upcoming = parts[gi + 1] if gi + 1 < len(groups) else iter(())
        for (d, ch), f in zip(gr, pre[gi]):
            st_b = st[d].astype(BF16)
            u_t = (_dot_nt(st_b, f["a_hat"]) + f["u0_t"]).astype(BF16)
            uv_t = jnp.concatenate([u_t, f["v_t"]], axis=1)
            y_t = _dot_nt(jnp.concatenate([st_b, uv_t], axis=1), f["rpp"])
            st[d] = st[d] * f["decay"] + _dot(uv_t, f["bk"])
            y = y_t.T
            dirs[d][2][0, ch * CHUNK:(ch + 1) * CHUNK, :] = y[:CHUNK] + y[CHUNK:] + f["bonus"]
            next(upcoming, None)
            next(upcoming, None)
        for _ in upcoming:
            pass
    st_ref[0] = st[0]
    st_ref[1] = st[1]


def _rwscan(rva, dirp, r_k, batch, seq):
    n = batch * seq
    nc = min(8, seq // CHUNK)
    tcs = CHUNK * nc
    nt = seq // tcs
    fwd = lambda bi, c, t: bi * nt + t
    bwd = lambda bi, c, t: bi * nt + nt - 1 - t
    s3 = lambda rows: pl.BlockSpec((1, tcs, LANES), lambda bi, c, t: (c, rows(bi, c, t), 0))
    s_rva = lambda rows: pl.BlockSpec((3, 1, tcs, LANES), lambda bi, c, t: (0, c, rows(bi, c, t), 0))
    s_dir = lambda d, rows: pl.BlockSpec((1, 3, 1, tcs, LANES), lambda bi, c, t: (d, 0, c, rows(bi, c, t), 0))
    out = jax.ShapeDtypeStruct((8, n, LANES), F32)
    return pl.pallas_call(
        functools.partial(_rwscan_kernel, nc=nc, group=2 * nc),
        grid=(batch, 8, nt),
        in_specs=[s_rva(fwd), s_dir(0, fwd), s_rva(bwd), s_dir(1, bwd),
                  pl.BlockSpec((1, 1, LANES), lambda bi, c, t: (c, 0, 0))],
        out_specs=[s3(fwd), s3(bwd)],
        out_shape=[out, out],
        scratch_shapes=[pltpu.VMEM((2, LANES, LANES), F32)],
        compiler_params=_cparams(("parallel", "parallel", "arbitrary")),
        name="rwkv_scan",
    )(rva, dirp, rva, dirp, r_k)


def _merge_kernel(x_ref, oatt_ref, yf_ref, yb_ref, g_ref, gates_ref, gb_ref, lng_ref, lnb_ref,
                  wba_ref, wbr_ref, wout_ref, h_ref):
    bd = _block_ones()
    orw = []
    for c in range(8):
        y = yf_ref[c] + yb_ref[c]
        mu = _head_segsum(y, bd) * (1.0 / RW_HEAD)
        yc = y - mu
        var = _head_segsum(yc * yc, bd) * (1.0 / RW_HEAD)
        yn = yc * lax.rsqrt(var + LNX_EPS) * lng_ref[c] + lnb_ref[c]
        orw.append((yn * g_ref[c]).astype(BF16))
    orw = jnp.concatenate(orw, axis=1)
    oatt = jnp.concatenate([oatt_ref[c] for c in range(8)], axis=1)
    ga = jnp.concatenate([gates_ref[c] for c in range(16)], axis=1).astype(F32) + gb_ref[0]
    gr = jnp.concatenate([gates_ref[16 + c] for c in range(16)], axis=1).astype(F32) + gb_ref[1]
    merged = _sigmoid(ga) * _dot(oatt, wba_ref[...]) + _sigmoid(gr) * _dot(orw, wbr_ref[...])
    h_ref[...] = x_ref[...] + _dot(merged.astype(BF16), wout_ref[...])


def _merge(x2d, oatt, yf, yb, g, slabs, gate_b, lng, lnb, wba, wbr, wout):
    n = x2d.shape[0]
    tm = min(256, n)
    const = lambda shape: pl.BlockSpec(shape, lambda i: (0,) * len(shape), pipeline_mode=pl.Buffered(1))
    s8 = pl.BlockSpec((8, tm, LANES), lambda i: (0, i, 0))
    return pl.pallas_call(
        _merge_kernel,
        grid=(n // tm,),
        in_specs=[
            pl.BlockSpec((tm, D_MODEL), lambda i: (i, 0)),
            s8, s8, s8, s8,
            pl.BlockSpec((32, tm, LANES), lambda i: (0, i, 0)),
            const((2, 1, D_MODEL)), const((8, 1, LANES)), const((8, 1, LANES)),
            const((ATT_WIDTH, D_MODEL)), const((RW_WIDTH, D_MODEL)), const((D_MODEL, D_MODEL)),
        ],
        out_specs=pl.BlockSpec((tm, D_MODEL), lambda i: (i, 0)),
        out_shape=jax.ShapeDtypeStruct((n, D_MODEL), F32),
        compiler_params=_cparams(("parallel",)),
        name="merge_outproj",
    )(x2d, oatt, yf, yb, g, slabs, gate_b, lng, lnb, wba, wbr, wout)


def _router_kernel(h_ref, g_ref, wr_ref, hn_ref, aff_ref):
    x = h_ref[...]
    ms = jnp.mean(x * x, axis=-1, keepdims=True)
    hn = x * lax.rsqrt(ms + NORM_EPS) * g_ref[...]
    hn_ref[...] = hn
    xh, xm, xl = _split3(hn)
    wh, wm, wl = _split3(wr_ref[...])
    logits = (_dot(xh, wh) + _dot(xh, wm) + _dot(xm, wh)
              + _dot(xh, wl) + _dot(xl, wh) + _dot(xm, wm))
    lt = logits.T[:N_EXPERTS]
    m = jnp.max(lt, axis=0, keepdims=True)
    e = jnp.exp(lt - m)
    aff_ref[...] = e / jnp.sum(e, axis=0, keepdims=True)


def _router(h2d, g, wr_pad):
    n = h2d.shape[0]
    tm = min(256, n)
    return pl.pallas_call(
        _router_kernel,
        grid=(n // tm,),
        in_specs=[
            pl.BlockSpec((tm, D_MODEL), lambda i: (i, 0)),
            pl.BlockSpec((1, D_MODEL), lambda i: (0, 0)),
            pl.BlockSpec((D_MODEL, LANES), lambda i: (0, 0)),
        ],
        out_specs=[
            pl.BlockSpec((tm, D_MODEL), lambda i: (i, 0)),
            pl.BlockSpec((N_EXPERTS, tm), lambda i: (0, i)),
        ],
        out_shape=[
            jax.ShapeDtypeStruct((n, D_MODEL), F32),
            jax.ShapeDtypeStruct((N_EXPERTS, n), F32),
        ],
        compiler_params=_cparams(("parallel",)),
        name="router",
    )(h2d, g, wr_pad)


def _select_kernel(aff_ref, incl_ref, tbl_ref, gval_ref, cnt_ref, slot_ref, *, cap, tt):
    bits = pltpu.bitcast(aff_ref[...], I32)
    nrow = bits.shape[1]

    def count(mask):
        c = jnp.sum(jnp.where(mask, 1, 0), axis=2, keepdims=True)
        return jnp.sum(c, axis=1, keepdims=True)

    def body(_, carry):
        lo, hi = carry
        mid = lo + ((hi - lo) >> 1)
        ok = count(bits >= mid) >= cap
        return jnp.where(ok, mid, lo), jnp.where(ok, hi, mid)

    lo0 = jnp.zeros((N_EXPERTS, 1, 1), I32)
    hi0 = jnp.full((N_EXPERTS, 1, 1), 0x7F800000, I32)
    thr, _ = lax.fori_loop(0, 31, body, (lo0, hi0))
    gt = bits > thr
    eq = bits == thr
    need = cap - count(gt)

    ri = lax.broadcasted_iota(I32, (LANES, LANES), 0)
    ci = lax.broadcasted_iota(I32, (LANES, LANES), 1)
    upper = jnp.where(ri <= ci, 1.0, 0.0).astype(BF16)
    rr = lax.broadcasted_iota(I32, (nrow, nrow), 0)
    rc = lax.broadcasted_iota(I32, (nrow, nrow), 1)
    lower_strict = jnp.where(rc < rr, 1.0, 0.0).astype(BF16)

    def incl_prefix(mask):
        x = jnp.where(mask, 1.0, 0.0).astype(BF16)
        incl = _dot(x.reshape(N_EXPERTS * nrow, LANES), upper).reshape(N_EXPERTS, nrow, LANES)
        tot = jnp.broadcast_to(incl[:, :, LANES - 1:LANES], incl.shape).astype(BF16)
        before = jnp.stack([_dot(lower_strict, tot[e]) for e in range(N_EXPERTS)], axis=0)
        return incl + before

    sel = gt | (eq & (incl_prefix(eq) - 1.0 < need.astype(F32)))
    incl_ref[...] = incl_prefix(sel).astype(I32)
    run = jnp.zeros((nrow, LANES), F32)
    for e in range(N_EXPERTS):
        slot_ref[e] = run
        run = run + jnp.where(sel[e], 1.0, 0.0)
    cnt_ref[...] = run.astype(I32)

    pf = lax.broadcasted_iota(I32, (1, cap), 1).astype(F32)
    jrow = lax.broadcasted_iota(I32, (nrow, cap), 0).astype(F32)
    lrow = lax.broadcasted_iota(I32, (LANES, cap), 0).astype(F32)

    def compact(e, carry):
        g = incl_ref[e].astype(F32)
        jsel = jnp.sum(jnp.where(g[:, LANES - 1:LANES] <= pf, 1.0, 0.0), axis=0, keepdims=True)
        onehot = jnp.where(jrow == jsel, 1.0, 0.0).astype(BF16)
        ghi = jnp.floor(g * (1.0 / 256.0))
        glo = g - 256.0 * ghi
        grow = 256.0 * _dot_tn(ghi.astype(BF16), onehot) + _dot_tn(glo.astype(BF16), onehot)
        lstar = jnp.sum(jnp.where(grow <= pf, 1.0, 0.0), axis=0, keepdims=True)
        lsel = lrow == lstar
        ah, am, al = _split3(aff_ref[e])
        arow = _dot_tn(ah, onehot) + _dot_tn(am, onehot) + _dot_tn(al, onehot)
        gval_ref[pl.ds(e, 1), :] = jnp.sum(jnp.where(lsel, arow, 0.0), axis=0, keepdims=True)
        krow = _dot_tn(slot_ref[e].astype(BF16), onehot)
        kk = jnp.sum(jnp.where(lsel, krow, 0.0), axis=0, keepdims=True)
        tok = (jsel * float(LANES) + lstar).astype(I32)
        dest = kk.astype(I32) * tt + (tok & (tt - 1))
        tbl_ref[pl.ds(e, 1), :] = tok | (dest << 16)
        return carry

    lax.fori_loop(0, N_EXPERTS, compact, 0)


def _select(aff3, cap, tt):
    nrow = aff3.shape[1]
    return pl.pallas_call(
        functools.partial(_select_kernel, cap=cap, tt=tt),
        out_shape=[
            jax.ShapeDtypeStruct(aff3.shape, I32),
            jax.ShapeDtypeStruct((N_EXPERTS, cap), I32),
            jax.ShapeDtypeStruct((N_EXPERTS, cap), F32),
            jax.ShapeDtypeStruct((nrow, LANES), I32),
        ],
        scratch_shapes=[pltpu.VMEM((N_EXPERTS, nrow, LANES), F32)],
        compiler_params=pltpu.CompilerParams(vmem_limit_bytes=VMEM_LIMIT),
        name="expert_select",
    )(aff3)


def _ffn_kernel(tbl_ref, hn_hbm, gval_ref, wg_ref, wu_ref, wd_ref, out_ref, xbuf, sem, *, tc, nt):
    step = pl.program_id(0) * nt + pl.program_id(1)
    last = N_EXPERTS * nt - 1
    slot = step % FFN_SLOTS
    ahead = FFN_SLOTS - 1

    def wait(slt):
        pltpu.make_async_copy(hn_hbm.at[pl.ds(0, tc)], xbuf.at[slt], sem.at[slt]).wait()

    @pl.when(step == 0)
    def _():
        for t in range(ahead):
            def body(i, carry, t=t):
                tok = tbl_ref[t * tc + i] & 0xFFFF
                pltpu.make_async_copy(hn_hbm.at[pl.ds(tok, 1)], xbuf.at[t, pl.ds(i, 1)], sem.at[t]).start()
                return carry
            lax.fori_loop(0, tc, body, 0, unroll=8)

    wait(slot)
    xe = xbuf[slot].astype(BF16)
    a = _dot(xe, wg_ref[0])
    u = _dot(xe, wu_ref[0])
    hmid = (a * _sigmoid(a) * u).astype(BF16)
    nxt = jnp.minimum(step + ahead, last)
    into = (step + ahead) % FFN_SLOTS
    for i in range(tc):
        tok = tbl_ref[nxt * tc + i] & 0xFFFF
        pltpu.make_async_copy(hn_hbm.at[pl.ds(tok, 1)], xbuf.at[into, pl.ds(i, 1)], sem.at[into]).start()
    out_ref[...] = _dot(hmid, wd_ref[0]) * gval_ref[0]

    @pl.when(step == last)
    def _():
        for t in range(1, FFN_SLOTS):
            wait((last + t) % FFN_SLOTS)


def _expert_ffn(tbl_flat, hn, gval, wg, wu, wd, cap):
    tc = min(256, cap)
    nt = cap // tc
    grid_spec = pltpu.PrefetchScalarGridSpec(
        num_scalar_prefetch=1,
        grid=(N_EXPERTS, nt),
        in_specs=[
            pl.BlockSpec(memory_space=pl.ANY),
            pl.BlockSpec((1, tc, 1), lambda e, j, idx: (e * nt + j, 0, 0)),
            pl.BlockSpec((1, D_MODEL, EXPERT_FF), lambda e, j, idx: (e, 0, 0)),
            pl.BlockSpec((1, D_MODEL, EXPERT_FF), lambda e, j, idx: (e, 0, 0)),
            pl.BlockSpec((1, EXPERT_FF, D_MODEL), lambda e, j, idx: (e, 0, 0)),
        ],
        out_specs=pl.BlockSpec((tc, D_MODEL), lambda e, j, idx: (e * nt + j, 0)),
        scratch_shapes=[pltpu.VMEM((FFN_SLOTS, tc, D_MODEL), F32), pltpu.SemaphoreType.DMA((FFN_SLOTS,))],
    )
    return pl.pallas_call(
        functools.partial(_ffn_kernel, tc=tc, nt=nt),
        grid_spec=grid_spec,
        out_shape=jax.ShapeDtypeStruct((N_EXPERTS * cap, D_MODEL), F32),
        compiler_params=_cparams(("arbitrary", "arbitrary")),
        name="expert_ffn",
    )(tbl_flat, hn, gval.reshape(N_EXPERTS * nt, tc, 1), wg, wu, wd)


def _combine_kernel(tbl_ref, p0_ref, km_ref, h_ref, cnt_ref, g_ref, ye_hbm, out_ref, stage, sem,
                    *, tt, cap, ntile):
    tile = pl.program_id(0)
    slot = tile % COMBINE_SLOTS

    def issue(tl, slt):
        for e in range(N_EXPERTS):
            p0 = p0_ref[e * (ntile + 1) + tl]
            cnt = p0_ref[e * (ntile + 1) + tl + 1] - p0

            def fetch(q, e=e, p0=p0):
                row = e * cap + p0 + q
                pltpu.make_async_copy(ye_hbm.at[pl.ds(row, 1)], stage.at[slt, pl.ds(tbl_ref[row] >> 16, 1)],
                                      sem.at[slt]).start()

            def four(j, carry, fetch=fetch):
                for u in range(4):
                    fetch(4 * j + u)
                return carry

            def one(q, carry, fetch=fetch, cnt=cnt):
                fetch((cnt & ~3) + q)
                return carry

            lax.fori_loop(0, cnt >> 2, four, 0)
            lax.fori_loop(0, cnt & 3, one, 0)

    ahead = COMBINE_SLOTS - 1

    @pl.when(tile == 0)
    def _():
        for tl in range(min(ahead, ntile)):
            issue(tl, tl)

    @pl.when(tile + ahead < ntile)
    def _():
        issue(tile + ahead, (tile + ahead) % COMBINE_SLOTS)

    def wait_rows(nrows):
        def body(q, carry):
            pltpu.make_async_copy(ye_hbm.at[pl.ds(0, nrows)], stage.at[slot, pl.ds(0, nrows)], sem.at[slot]).wait()
            return carry
        return body

    total = km_ref[2 * tile + 1]
    lax.fori_loop(0, total >> 3, wait_rows(8), 0)
    lax.fori_loop(0, total & 7, wait_rows(1), 0)

    cnt = cnt_ref[...]
    parts = []
    for cs in (slice(c * 256, (c + 1) * 256) for c in range(D_MODEL // 256)):
        def add(k, acc, cs=cs):
            rows = stage[slot, pl.ds(pl.multiple_of(k * tt, tt), tt), cs]
            return acc + jnp.where(cnt > k, rows, 0.0)
        parts.append(lax.fori_loop(0, km_ref[2 * tile], add, h_ref[:, cs]))
    acc = jnp.concatenate(parts, axis=1)
    ms = jnp.mean(acc * acc, axis=-1, keepdims=True)
    out_ref[...] = acc * lax.rsqrt(ms + NORM_EPS) * g_ref[...]


def _combine(tbl_flat, p0_flat, km_flat, h2d, cnt_tok, g, yexp, cap, tt):
    n = h2d.shape[0]
    ntile = n // tt
    grid_spec = pltpu.PrefetchScalarGridSpec(
        num_scalar_prefetch=3,
        grid=(ntile,),
        in_specs=[
            pl.BlockSpec((tt, D_MODEL), lambda i, a, b, c: (i, 0)),
            pl.BlockSpec((tt, 1), lambda i, a, b, c: (i, 0)),
            pl.BlockSpec((1, D_MODEL), lambda i, a, b, c: (0, 0)),
            pl.BlockSpec(memory_space=pl.ANY),
        ],
        out_specs=pl.BlockSpec((tt, D_MODEL), lambda i, a, b, c: (i, 0)),
        scratch_shapes=[pltpu.VMEM((COMBINE_SLOTS, N_EXPERTS * tt, D_MODEL), F32),
                        pltpu.SemaphoreType.DMA((COMBINE_SLOTS,))],
    )
    return pl.pallas_call(
        functools.partial(_combine_kernel, tt=tt, cap=cap, ntile=ntile),
        grid_spec=grid_spec,
        out_shape=jax.ShapeDtypeStruct((n, D_MODEL), F32),
        compiler_params=_cparams(("arbitrary",)),
        name="moe_combine",
    )(tbl_flat, p0_flat, km_flat, h2d, cnt_tok, g, yexp)


def _trunk(x, p, w_slab, mu_slab, rwp, moe_w, slopes):
    batch, seq, _ = x.shape
    n = batch * seq
    x2d = x.reshape(n, D_MODEL)
    slabs = _inproj(x2d, p["norm_mix_g"], w_slab)
    oatt = _attention(slabs, slopes, p["lambda_q1"], p["lambda_k1"], p["lambda_q2"], p["lambda_k2"],
                      p["subln_g"], batch, seq)
    rva, g, dirp = _rwprep(slabs, mu_slab, *rwp[:7], batch, seq)
    yf, yb = _rwscan(rva, dirp, rwp[7], batch, seq)
    h = _merge(x2d, oatt, yf, yb, g, slabs, p["gate_b"][0].reshape(2, 1, D_MODEL), rwp[8], rwp[9],
               moe_w["wba"], moe_w["wbr"], moe_w["wout"])

    cap = max(1, CAPACITY_FACTOR * n // N_EXPERTS)
    hn, aff = _router(h, p["norm_ffn_g"], moe_w["wr"])
    aff3 = aff.reshape(N_EXPERTS, n // LANES, LANES)
    assert n <= 1 << 16
    tt = LANES
    incl, tbl, gval, cnt = _select(aff3, cap, tt)
    ntile = n // tt
    ends = incl.reshape(N_EXPERTS, n)[:, tt - 1::tt]
    p0 = jnp.concatenate([jnp.zeros((N_EXPERTS, 1), I32), ends], axis=1)
    ct = cnt.reshape(ntile, tt)
    km = jnp.stack([jnp.max(ct, axis=1), jnp.sum(ct, axis=1)], axis=1)
    tbl_flat = tbl.reshape(-1)
    yexp = _expert_ffn(tbl_flat, hn, gval, moe_w["wg"], moe_w["wu"], moe_w["wd"], cap)
    y = _combine(tbl_flat, p0.reshape(-1), km.reshape(-1), h, cnt.reshape(n, 1),
                 p["norm_final_g"].reshape(1, D_MODEL), yexp, cap, tt)
    return y.reshape(batch, seq, D_MODEL)


def kernel(x_prompt, x_sample, norm_mix_g, w_in, shift_mu, lambda_q1, lambda_k1, lambda_q2, lambda_k2, subln_g, rw_w0, rw_w2, rw_a0, rw_a2, rw_g2, rw_k_k, rw_k_a, rw_r_k, lnx_g, lnx_b, gate_b, w_br_att, w_br_rw, w_out, norm_ffn_g, w_router, w_gate_e, w_up_e, w_down_e, norm_final_g):
    p = dict(norm_mix_g=norm_mix_g, lambda_q1=lambda_q1, lambda_k1=lambda_k1, lambda_q2=lambda_q2,
             lambda_k2=lambda_k2, subln_g=subln_g, rw_w0=rw_w0, rw_w2=rw_w2, rw_a0=rw_a0, rw_a2=rw_a2,
             rw_g2=rw_g2, rw_k_k=rw_k_k, rw_k_a=rw_k_a, rw_r_k=rw_r_k, lnx_g=lnx_g, lnx_b=lnx_b,
             gate_b=gate_b, norm_ffn_g=norm_ffn_g, norm_final_g=norm_final_g)
    w_slab, mu_slab = _prep_in_weights(w_in[0], shift_mu[0])
    rwp = _prep_rw_params(p)
    moe_w = dict(
        wba=w_br_att[0].astype(BF16), wbr=w_br_rw[0].astype(BF16), wout=w_out[0].astype(BF16),
        wr=jnp.pad(w_router[0], ((0, 0), (0, LANES - N_EXPERTS))),
        wg=w_gate_e[0].astype(BF16), wu=w_up_e[0].astype(BF16), wd=w_down_e[0].astype(BF16))
    slopes = jnp.asarray([2.0 ** (-8.0 * (i + 1) / ATT_HEADS) for i in range(ATT_HEADS)], F32)
    return (_trunk(x_prompt, p, w_slab, mu_slab, rwp, moe_w, slopes),
            _trunk(x_sample, p, w_slab, mu_slab, rwp, moe_w, slopes))
```

```python
import functools
import math

import jax
import jax.numpy as jnp
from jax import lax
from jax.experimental import pallas as pl
from jax.experimental.pallas import tpu as pltpu

F32 = jnp.float32
BF16 = jnp.bfloat16
I32 = jnp.int32

D_MODEL = 2048
ATT_HEADS = 8
ATT_HEAD_DIM = 64
ATT_WIDTH = ATT_HEADS * 2 * ATT_HEAD_DIM
RW_HEAD = 64
RW_WIDTH = 1024
DECAY_LORA = 96
ICLR_LORA = 96
GATE_LORA = 256
SHIFT_WIDTH = 3 * RW_WIDTH + DECAY_LORA + ICLR_LORA + GATE_LORA
N_EXPERTS = 16
CAPACITY_FACTOR = 2
EXPERT_FF = 1024
NORM_EPS = 1e-6
SUBLN_EPS = 1e-5
LNX_EPS = 64e-5
LAM_INIT = 0.8 - 0.6 * math.exp(-0.3 * 0)

LANES = 128
VMEM_LIMIT = 56 * 1024 * 1024

SL_GATE_ATT, SL_GATE_RW = 0, 16
SL_ATT_Q, SL_ATT_K, SL_ATT_V = 32, 40, 48
SL_RW = 56
N_RW_SLABS = 28
N_SLABS = 84
CHUNK = 64
COMBINE_SLOTS = 2
FFN_SLOTS = 3
ROW_TILES = D_MODEL // LANES


def _cparams(sem):
    return pltpu.CompilerParams(dimension_semantics=sem, vmem_limit_bytes=VMEM_LIMIT)


def _sigmoid(x):
    return 1.0 / (1.0 + jnp.exp(-x))


def _split3(x):
    hi = x.astype(BF16)
    r1 = x - hi.astype(F32)
    mid = r1.astype(BF16)
    lo = (r1 - mid.astype(F32)).astype(BF16)
    return hi, mid, lo


def _dot(a, b):
    return jnp.dot(a, b, preferred_element_type=F32)


def _dot_nt(a, b):
    return lax.dot_general(a, b, (((1,), (1,)), ((), ())), preferred_element_type=F32)


def _dot_tn(a, b):
    return lax.dot_general(a, b, (((0,), (0,)), ((), ())), preferred_element_type=F32)


def _dot_f32(a_bf16_exact, x):
    hi, mid, lo = _split3(x)
    return _dot(a_bf16_exact, hi) + _dot(a_bf16_exact, mid) + _dot(a_bf16_exact, lo)


def _rearrange_in_cols(a):
    att = a[..., :3 * ATT_WIDTH]
    zr = a[..., 3 * ATT_WIDTH:3 * ATT_WIDTH + SHIFT_WIDTH]
    gates = a[..., 3 * ATT_WIDTH + SHIFT_WIDTH:]
    o3 = 3 * RW_WIDTH
    o4 = o3 + DECAY_LORA
    o5 = o4 + ICLR_LORA
    pad = [(0, 0)] * (a.ndim - 1)
    lw = jnp.pad(zr[..., o3:o4], pad + [(0, LANES - DECAY_LORA)])
    la = jnp.pad(zr[..., o4:o5], pad + [(0, LANES - ICLR_LORA)])
    return jnp.concatenate([gates, att, zr[..., :o3], lw, la, zr[..., o5:]], axis=-1)


def _prep_in_weights(w_in, shift_mu):
    w_slab = _rearrange_in_cols(w_in).astype(BF16)
    mu_full = jnp.pad(shift_mu, ((0, 0), (3 * ATT_WIDTH, 2 * D_MODEL)))
    mu_slab = _rearrange_in_cols(mu_full)[:, SL_RW * LANES:]
    return w_slab, mu_slab.reshape(2, 1, N_RW_SLABS * LANES)


def _inproj_kernel(x_ref, g_ref, w_ref, o_ref, xn_ref, *, n_out_slabs):
    @pl.when(pl.program_id(1) == 0)
    def _():
        x = x_ref[...]
        ms = jnp.mean(x * x, axis=-1, keepdims=True)
        xn_ref[...] = (x * lax.rsqrt(ms + NORM_EPS) * g_ref[...]).astype(BF16)

    acc = _dot(xn_ref[...], w_ref[...])
    for c in range(n_out_slabs):
        o_ref[c] = acc[:, c * LANES:(c + 1) * LANES].astype(BF16)


def _inproj(x2d, g, w_slab):
    n = x2d.shape[0]
    tm = min(1024, n)
    tn = 1536
    n_out_slabs = tn // LANES
    grid = (n // tm, (N_SLABS * LANES) // tn)
    return pl.pallas_call(
        functools.partial(_inproj_kernel, n_out_slabs=n_out_slabs),
        grid=grid,
        in_specs=[
            pl.BlockSpec((tm, D_MODEL), lambda i, j: (i, 0)),
            pl.BlockSpec((1, D_MODEL), lambda i, j: (0, 0)),
            pl.BlockSpec((D_MODEL, tn), lambda i, j: (0, j)),
        ],
        out_specs=pl.BlockSpec((n_out_slabs, tm, LANES), lambda i, j: (j, i, 0)),
        out_shape=jax.ShapeDtypeStruct((N_SLABS, n, LANES), BF16),
        scratch_shapes=[pltpu.VMEM((tm, D_MODEL), BF16)],
        compiler_params=_cparams(("parallel", "arbitrary")),
        name="inproj",
    )(x2d, g, w_slab)


def _attn_kernel(slopes_ref, lq1_ref, lk1_ref, lq2_ref, lk2_ref, subg_ref, q_ref, k_ref, v_ref, o_ref,
                 kt1_ref, kt2_ref, vaug_ref, *, seq, tq):
    h = pl.program_id(1)
    qi = pl.program_id(2)
    slope = slopes_ref[h]
    q0 = pl.multiple_of(qi * tq, tq)
    view = pl.ds(q0, seq)

    @pl.when(qi == 0)
    def _():
        kt = k_ref[0].astype(F32).T
        row = lax.broadcasted_iota(I32, kt.shape, 0)
        k1 = jnp.where(row < ATT_HEAD_DIM, kt, 0.0).astype(BF16)
        k2 = jnp.where(row >= ATT_HEAD_DIM, kt, 0.0).astype(BF16)
        lane = lax.broadcasted_iota(I32, (seq, LANES), 1)
        va = jnp.concatenate([v_ref[0], jnp.where(lane == 0, 1.0, 0.0).astype(BF16)], axis=1)
        for half in (slice(0, seq), slice(seq, 2 * seq)):
            kt1_ref[:, half] = k1
            kt2_ref[:, half] = k2
            vaug_ref[half, :] = va

    col = lax.broadcasted_iota(I32, (16, seq), 1)
    r16 = lax.broadcasted_iota(I32, (16, seq), 0)
    wrapped = col + q0 >= seq
    jp = jnp.where(wrapped, col - seq, col)
    sigma = jnp.where(col < tq, 0.0, jnp.where(wrapped, -1.0, 1.0)).astype(F32)
    jh = (jp >> 8).astype(F32)
    jl = (jp & 255).astype(F32)
    feat = jnp.where(r16 <= 1, sigma,
                     jnp.where(r16 == 2, -sigma * (slope * 256.0) * jh,
                               jnp.where(r16 == 3, -sigma * slope * jl, 0.0))).astype(BF16)
    kt1_ref[ATT_HEAD_DIM:ATT_HEAD_DIM + 16, view] = feat
    kt2_ref[0:16, view] = feat

    q = q_ref[0].astype(F32) * (ATT_HEAD_DIM ** -0.5)
    lane = lax.broadcasted_iota(I32, (tq, LANES), 1)
    ip = lax.broadcasted_iota(I32, (tq, LANES), 0)
    ih = slope * (ip & ~255).astype(F32)
    il = slope * (ip & 255).astype(F32)

    def query_side(fl):
        return jnp.where(fl == 0, ih, jnp.where(fl == 1, il, jnp.where(fl <= 3, 1.0, 0.0)))

    lhs1 = jnp.where(lane < ATT_HEAD_DIM, q, query_side(lane - ATT_HEAD_DIM)).astype(BF16)
    lhs2 = jnp.where(lane >= ATT_HEAD_DIM, q, query_side(lane)).astype(BF16)

    di = lax.broadcasted_iota(I32, (tq, tq), 0)
    dj = lax.broadcasted_iota(I32, (tq, tq), 1)
    diag_bias = -slope * jnp.abs(di - dj).astype(F32)

    def weights(lhs, kt_ref):
        s = _dot(lhs, kt_ref[:, view])
        s = jnp.concatenate([s[:, :tq] + diag_bias, s[:, tq:]], axis=1)
        m = jnp.max(s, axis=-1, keepdims=True)
        return jnp.exp(s - m).astype(BF16)

    e = jnp.concatenate([weights(lhs1, kt1_ref), weights(lhs2, kt2_ref)], axis=0)
    oa = _dot(e, vaug_ref[view, :])
    o1 = oa[:tq, :LANES] / oa[:tq, LANES:LANES + 1]
    o2 = oa[tq:, :LANES] / oa[tq:, LANES:LANES + 1]
    lam = (jnp.exp(jnp.sum(lq1_ref[...] * lk1_ref[...], keepdims=True))
           - jnp.exp(jnp.sum(lq2_ref[...] * lk2_ref[...], keepdims=True)) + LAM_INIT)
    out = o1 - lam * o2
    ms = jnp.mean(out * out, axis=-1, keepdims=True)
    y = out * lax.rsqrt(ms + SUBLN_EPS) * subg_ref[...]
    o_ref[0] = (y * (1.0 - LAM_INIT)).astype(BF16)


def _attention(slabs, slopes, lq1, lk1, lq2, lk2, subg, batch, seq):
    n = batch * seq
    tq = 512
    nq = seq // tq
    vec = lambda: pl.BlockSpec((1, ATT_HEAD_DIM), lambda b, h, i: (0, 0))
    return pl.pallas_call(
        functools.partial(_attn_kernel, seq=seq, tq=tq),
        grid=(batch, ATT_HEADS, nq),
        in_specs=[
            pl.BlockSpec(memory_space=pltpu.SMEM),
            vec(), vec(), vec(), vec(),
            pl.BlockSpec((1, LANES), lambda b, h, i: (0, 0)),
            pl.BlockSpec((1, tq, LANES), lambda b, h, i: (SL_ATT_Q + h, b * nq + i, 0)),
            pl.BlockSpec((1, seq, LANES), lambda b, h, i: (SL_ATT_K + h, b, 0)),
            pl.BlockSpec((1, seq, LANES), lambda b, h, i: (SL_ATT_V + h, b, 0)),
        ],
        out_specs=pl.BlockSpec((1, tq, LANES), lambda b, h, i: (h, b * nq + i, 0)),
        out_shape=jax.ShapeDtypeStruct((ATT_HEADS, n, LANES), BF16),
        scratch_shapes=[
            pltpu.VMEM((LANES, 2 * seq), BF16),
            pltpu.VMEM((LANES, 2 * seq), BF16),
            pltpu.VMEM((2 * seq, 2 * LANES), BF16),
        ],
        compiler_params=_cparams(("parallel", "parallel", "arbitrary")),
        name="diff_attn",
    )(slopes, lq1, lk1, lq2, lk2, subg, slabs, slabs, slabs)


def _head_segsum(x, bd):
    hi, mid, lo = _split3(x)
    return _dot(hi, bd) + _dot(mid, bd) + _dot(lo, bd)


def _block_ones():
    ri = lax.broadcasted_iota(I32, (LANES, LANES), 0)
    ci = lax.broadcasted_iota(I32, (LANES, LANES), 1)
    return jnp.where((ri >> 6) == (ci >> 6), 1.0, 0.0).astype(BF16)


def _rwprep_kernel(main_ref, prev_ref, next_ref, mu_ref, w0_ref, a0_ref, kk_ref, ka_ref, w2_ref, a2_ref,
                   g2_ref, rva_ref, g_ref, dirp_ref, *, nt):
    i = pl.program_id(1)
    t = main_ref.shape[1]
    hb = prev_ref.shape[1]
    wide = lambda ref: jnp.concatenate([ref[c] for c in range(N_RW_SLABS)], axis=1)
    zb = wide(main_ref)
    prev = jnp.where(i > 0, wide(prev_ref), jnp.zeros((), BF16))
    nxt = jnp.where(i < nt - 1, wide(next_ref), jnp.zeros((), BF16))
    halo = jnp.concatenate([prev, zb, nxt], axis=0)
    ri = lax.broadcasted_iota(I32, (t, t + 2 * hb), 0)
    ci = lax.broadcasted_iota(I32, (t, t + 2 * hb), 1)
    zp = _dot(jnp.where(ci == ri + hb - 1, 1.0, 0.0).astype(BF16), halo)
    zn = _dot(jnp.where(ci == ri + hb + 1, 1.0, 0.0).astype(BF16), halo)
    z = zb.astype(F32)
    z = z + mu_ref[0] * (zp - z) + mu_ref[1] * (zn - z)
    slab = lambda s: z[:, s * LANES:(s + 1) * LANES]

    xw = jnp.tanh(slab(24)).astype(BF16)
    xa = slab(25).astype(BF16)
    xg = _sigmoid(z[:, 26 * LANES:28 * LANES]).astype(BF16)
    g_full = _dot(xg, g2_ref[...])
    lw = [_dot(xw, w2_ref[d]) for d in range(2)]
    la = [_dot(xa, a2_ref[d]) for d in range(2)]
    bd = _block_ones()
    for c in range(8):
        cs = slice(c * LANES, (c + 1) * LANES)
        kc = slab(8 + c)
        kk = kc * kk_ref[c]
        nrm = jnp.sqrt(_head_segsum(kk * kk, bd))
        kk = kk / jnp.maximum(nrm, 1e-12)
        rva_ref[0, c] = slab(c)
        rva_ref[1, c] = slab(16 + c)
        rva_ref[2, c] = -kk
        g_ref[c] = g_full[:, cs]
        for d in range(2):
            dirp_ref[d, 2, c] = -math.exp(-0.5) * _sigmoid(w0_ref[d, c] + lw[d][:, cs])
            asig = _sigmoid(a0_ref[d, c] + la[d][:, cs])
            dirp_ref[d, 0, c] = kc * (1.0 + (asig - 1.0) * ka_ref[c])
            dirp_ref[d, 1, c] = kk * asig


def _prep_rw_params(p):
    vec = lambda a: a.reshape(a.shape[:-1] + (8, 1, LANES))
    pad_rows = lambda a: jnp.pad(a, ((0, 0), (0, LANES - a.shape[1]), (0, 0))).astype(BF16)
    return (vec(p["rw_w0"][0]), vec(p["rw_a0"][0]), vec(p["rw_k_k"][0]), vec(p["rw_k_a"][0]),
            pad_rows(p["rw_w2"][0]), pad_rows(p["rw_a2"][0]), p["rw_g2"][0].astype(BF16),
            vec(p["rw_r_k"][0].reshape(RW_WIDTH)), vec(p["lnx_g"][0]), vec(p["lnx_b"][0]))


def _rwprep(slabs, mu_slab, w0, a0, k_k, k_a, w2, a2, g2, batch, seq):
    n = batch * seq
    t = 256
    nt = seq // t
    hb = 16
    full = lambda shape: pl.BlockSpec(shape, lambda b, i: (0,) * len(shape))
    rows = lambda b, i: b * nt + i
    out_specs = [pl.BlockSpec((3, 8, t, LANES), lambda b, i: (0, 0, rows(b, i), 0)),
                 pl.BlockSpec((8, t, LANES), lambda b, i: (0, rows(b, i), 0)),
                 pl.BlockSpec((2, 3, 8, t, LANES), lambda b, i: (0, 0, 0, rows(b, i), 0))]
    out_shape = [jax.ShapeDtypeStruct((3, 8, n, LANES), F32), jax.ShapeDtypeStruct((8, n, LANES), F32),
                 jax.ShapeDtypeStruct((2, 3, 8, n, LANES), F32)]
    rw_blk = SL_RW // N_RW_SLABS
    return pl.pallas_call(
        functools.partial(_rwprep_kernel, nt=nt),
        grid=(batch, nt),
        in_specs=[
            pl.BlockSpec((N_RW_SLABS, t, LANES), lambda b, i: (rw_blk, rows(b, i), 0)),
            pl.BlockSpec((N_RW_SLABS, hb, LANES),
                         lambda b, i: (rw_blk, jnp.maximum((b * seq + i * t) // hb - 1, 0), 0)),
            pl.BlockSpec((N_RW_SLABS, hb, LANES),
                         lambda b, i: (rw_blk, jnp.minimum((b * seq + (i + 1) * t) // hb, n // hb - 1), 0)),
            full((2, 1, N_RW_SLABS * LANES)),
            full((2, 8, 1, LANES)), full((2, 8, 1, LANES)), full((8, 1, LANES)), full((8, 1, LANES)),
            full((2, LANES, RW_WIDTH)), full((2, LANES, RW_WIDTH)), full((GATE_LORA, RW_WIDTH)),
        ],
        out_specs=out_specs,
        out_shape=out_shape,
        compiler_params=_cparams(("parallel", "parallel")),
        name="rwkv_prep",
    )(slabs, slabs, slabs, mu_slab, w0, a0, k_k, k_a, w2, a2, g2)


def _rwscan_kernel(rvaf_ref, dirf_ref, rvab_ref, dirb_ref, rk_ref, yf_ref, yb_ref, st_ref, *, nc, group):
    @pl.when(pl.program_id(2) == 0)
    def _():
        st_ref[...] = jnp.zeros_like(st_ref)

    lane = lax.broadcasted_iota(I32, (CHUNK, LANES), 1)
    head0 = lane < RW_HEAD
    ri = lax.broadcasted_iota(I32, (LANES, LANES), 0)
    ci = lax.broadcasted_iota(I32, (LANES, LANES), 1)
    same = (ri >> 6) == (ci >> 6)
    tt = ri & (CHUNK - 1)
    ss = ci & (CHUNK - 1)
    eye = jnp.where(ri == ci, 1.0, 0.0).astype(F32)
    tr = lax.broadcasted_iota(I32, (CHUNK, CHUNK), 0)
    tc = lax.broadcasted_iota(I32, (CHUNK, CHUNK), 1)
    bd = _block_ones()
    rk = rk_ref[0]

    def stack(x):
        return jnp.concatenate([jnp.where(head0, x, 0.0), jnp.where(head0, 0.0, x)], axis=0)

    dirs = ((rvaf_ref, dirf_ref, yf_ref), (rvab_ref, dirb_ref, yb_ref))
    strict = (same & (ss < tt), same & (ss > tt))
    incl = (same & (ss <= tt), same & (ss >= tt))
    tri = (jnp.where(tc <= tr, 1.0, 0.0).astype(BF16), jnp.where(tc >= tr, 1.0, 0.0).astype(BF16))
    last = (CHUNK - 1, 0)
    all_insts = [(d, k if d == 0 else nc - 1 - k) for k in range(nc) for d in range(2)]

    def load(which, j, insts):
        lead = (j, 0) if which == 0 else (0, j, 0)
        return [dirs[d][which][lead + (slice(ch * CHUNK, (ch + 1) * CHUNK), slice(None))] for d, ch in insts]

    def state_free_part(insts, out):
        every = range(len(insts))
        r, v, na = (load(0, j, insts) for j in range(3))
        kd, b, ld = (load(1, j, insts) for j in range(3))
        tri3 = [jnp.concatenate([t, t, t], axis=1) for t in tri]
        c = [_dot(tri3[insts[i][0]], jnp.concatenate(_split3(ld[i]), axis=0)) for i in every]
        total = [c[i][last[insts[i][0]]:last[insts[i][0]] + 1] for i in every]
        yield
        e_nc = [jnp.exp(-c[i]) for i in every]
        e_tc = [jnp.exp(total[i] - c[i]) for i in every]
        a_t = [stack(na[i] * jnp.exp(c[i] - ld[i])).astype(BF16) for i in every]
        r_t = [stack(r[i] * jnp.exp(c[i])).astype(BF16) for i in every]
        v_s = [stack(v[i]).astype(BF16) for i in every]
        rhs = [jnp.concatenate([stack(b[i] * e_nc[i]), stack(kd[i] * e_nc[i])], axis=0).astype(BF16) for i in every]
        bk = [jnp.concatenate([stack(b[i] * e_tc[i]), stack(kd[i] * e_tc[i])], axis=0).astype(BF16) for i in every]
        yield
        p = [_dot_nt(jnp.concatenate([a_t[i], r_t[i]], axis=0), rhs[i]) for i in every]
        yield
        n_ab = [jnp.where(strict[insts[i][0]], p[i][:LANES, :LANES], 0.0) for i in every]
        a_ak = [jnp.where(strict[insts[i][0]], p[i][:LANES, LANES:], 0.0).astype(BF16) for i in every]
        p_rb = [jnp.where(incl[insts[i][0]], p[i][LANES:, :LANES], 0.0).astype(BF16) for i in every]
        p_rk = [jnp.where(incl[insts[i][0]], p[i][LANES:, LANES:], 0.0).astype(BF16) for i in every]
        yield
        x = [eye + n_ab[i] for i in every]
        nk = [n_ab[i].astype(BF16) for i in every]
        nk = [_dot(nk[i], nk[i]) for i in every]
        for _ in range(4):
            yield
            both = [_dot(nk[i].astype(BF16), jnp.concatenate([nk[i], x[i]], axis=1).astype(BF16)) for i in every]
            nk = [both[i][:, :LANES] for i in every]
            x = [x[i] + both[i][:, LANES:] for i in every]
        yield
        x = [x[i] + _dot(nk[i].astype(BF16), x[i].astype(BF16)) for i in every]
        w = [_dot(a_ak[i], v_s[i]) for i in every]
        yield
        au = [_dot(x[i].astype(BF16), jnp.concatenate([a_t[i], w[i].astype(BF16)], axis=1)) for i in every]
        yield
        rpp = [jnp.concatenate([r_t[i], p_rb[i], p_rk[i]], axis=1) for i in every]
        bonus = [_head_segsum(r[i] * kd[i] * rk, bd) * v[i] for i in every]
        u0_t = [au[i][:, LANES:].T for i in every]
        v_t = [v_s[i].astype(F32).T.astype(BF16) for i in every]
        out.extend(dict(a_hat=au[i][:, :LANES].astype(BF16), u0_t=u0_t[i], v_t=v_t[i], rpp=rpp[i], bk=bk[i],
                        decay=jnp.exp(total[i]), bonus=bonus[i]) for i in every)

    groups = [all_insts[g:g + group] for g in range(0, len(all_insts), group)]
    pre = [[] for _ in groups]
    parts = [state_free_part(gr, pre[gi]) for gi, gr in enumerate(groups)]
    for _ in parts[0]:
        pass
    st = [st_ref[0], st_ref[1]]
    for gi, gr in enumerate(groups):
        upcoming = parts[gi + 1] if gi + 1 < len(groups) else iter(())
        for (d, ch), f in zip(gr, pre[gi]):
            st_b = st[d].astype(BF16)
            u_t = (_dot_nt(st_b, f["a_hat"]) + f["u0_t"]).astype(BF16)
            uv_t = jnp.concatenate([u_t, f["v_t"]], axis=1)
            y_t = _dot_nt(jnp.concatenate([st_b, uv_t], axis=1), f["rpp"])
            st[d] = st[d] * f["decay"] + _dot(uv_t, f["bk"])
            y = y_t.T
            dirs[d][2][0, ch * CHUNK:(ch + 1) * CHUNK, :] = y[:CHUNK] + y[CHUNK:] + f["bonus"]
            next(upcoming, None)
            next(upcoming, None)
        for _ in upcoming:
            pass
    st_ref[0] = st[0]
    st_ref[1] = st[1]


def _rwscan(rva, dirp, r_k, batch, seq):
    n = batch * seq
    nc = min(16, seq // CHUNK)
    tcs = CHUNK * nc
    nt = seq // tcs
    fwd = lambda bi, c, t: bi * nt + t
    bwd = lambda bi, c, t: bi * nt + nt - 1 - t
    s3 = lambda rows: pl.BlockSpec((1, tcs, LANES), lambda bi, c, t: (c, rows(bi, c, t), 0))
    s_rva = lambda rows: pl.BlockSpec((3, 1, tcs, LANES), lambda bi, c, t: (0, c, rows(bi, c, t), 0))
    s_dir = lambda d, rows: pl.BlockSpec((1, 3, 1, tcs, LANES), lambda bi, c, t: (d, 0, c, rows(bi, c, t), 0))
    out = jax.ShapeDtypeStruct((8, n, LANES), F32)
    return pl.pallas_call(
        functools.partial(_rwscan_kernel, nc=nc, group=2 * nc),
        grid=(batch, 8, nt),
        in_specs=[s_rva(fwd), s_dir(0, fwd), s_rva(bwd), s_dir(1, bwd),
                  pl.BlockSpec((1, 1, LANES), lambda bi, c, t: (c, 0, 0))],
        out_specs=[s3(fwd), s3(bwd)],
        out_shape=[out, out],
        scratch_shapes=[pltpu.VMEM((2, LANES, LANES), F32)],
        compiler_params=_cparams(("parallel", "parallel", "arbitrary")),
        name="rwkv_scan",
    )(rva, dirp, rva, dirp, r_k)


def _merge_kernel(x_ref, oatt_ref, yf_ref, yb_ref, g_ref, gates_ref, gb_ref, lng_ref, lnb_ref,
                  wba_ref, wbr_ref, wout_ref, h_ref):
    bd = _block_ones()
    orw = []
    for c in range(8):
        y = yf_ref[c] + yb_ref[c]
        mu = _head_segsum(y, bd) * (1.0 / RW_HEAD)
        yc = y - mu
        var = _head_segsum(yc * yc, bd) * (1.0 / RW_HEAD)
        yn = yc * lax.rsqrt(var + LNX_EPS) * lng_ref[c] + lnb_ref[c]
        orw.append((yn * g_ref[c]).astype(BF16))
    orw = jnp.concatenate(orw, axis=1)
    oatt = jnp.concatenate([oatt_ref[c] for c in range(8)], axis=1)
    ga = jnp.concatenate([gates_ref[c] for c in range(16)], axis=1).astype(F32) + gb_ref[0]
    gr = jnp.concatenate([gates_ref[16 + c] for c in range(16)], axis=1).astype(F32) + gb_ref[1]
    merged = _sigmoid(ga) * _dot(oatt, wba_ref[...]) + _sigmoid(gr) * _dot(orw, wbr_ref[...])
    h_ref[...] = x_ref[...] + _dot(merged.astype(BF16), wout_ref[...])


def _merge(x2d, oatt, yf, yb, g, slabs, gate_b, lng, lnb, wba, wbr, wout):
    n = x2d.shape[0]
    tm = min(256, n)
    const = lambda shape: pl.BlockSpec(shape, lambda i: (0,) * len(shape), pipeline_mode=pl.Buffered(1))
    s8 = pl.BlockSpec((8, tm, LANES), lambda i: (0, i, 0))
    return pl.pallas_call(
        _merge_kernel,
        grid=(n // tm,),
        in_specs=[
            pl.BlockSpec((tm, D_MODEL), lambda i: (i, 0)),
            s8, s8, s8, s8,
            pl.BlockSpec((32, tm, LANES), lambda i: (0, i, 0)),
            const((2, 1, D_MODEL)), const((8, 1, LANES)), const((8, 1, LANES)),
            const((ATT_WIDTH, D_MODEL)), const((RW_WIDTH, D_MODEL)), const((D_MODEL, D_MODEL)),
        ],
        out_specs=pl.BlockSpec((tm, D_MODEL), lambda i: (i, 0)),
        out_shape=jax.ShapeDtypeStruct((n, D_MODEL), F32),
        compiler_params=_cparams(("parallel",)),
        name="merge_outproj",
    )(x2d, oatt, yf, yb, g, slabs, gate_b, lng, lnb, wba, wbr, wout)


def _router_kernel(h_ref, g_ref, wr_ref, hn_ref, aff_ref):
    x = h_ref[...]
    ms = jnp.mean(x * x, axis=-1, keepdims=True)
    hn = x * lax.rsqrt(ms + NORM_EPS) * g_ref[...]
    hn_ref[...] = hn
    xh, xm, _ = _split3(hn)
    wh, wm, _ = _split3(wr_ref[...])
    logits = _dot(xh, wh) + _dot(xh, wm) + _dot(xm, wh)
    lt = logits.T[:N_EXPERTS]
    m = jnp.max(lt, axis=0, keepdims=True)
    e = jnp.exp(lt - m)
    aff_ref[...] = e / jnp.sum(e, axis=0, keepdims=True)


def _router(h2d, g, wr_pad):
    n = h2d.shape[0]
    tm = min(256, n)
    return pl.pallas_call(
        _router_kernel,
        grid=(n // tm,),
        in_specs=[
            pl.BlockSpec((tm, D_MODEL), lambda i: (i, 0)),
            pl.BlockSpec((1, D_MODEL), lambda i: (0, 0)),
            pl.BlockSpec((D_MODEL, LANES), lambda i: (0, 0)),
        ],
        out_specs=[
            pl.BlockSpec((tm, D_MODEL), lambda i: (i, 0)),
            pl.BlockSpec((N_EXPERTS, tm), lambda i: (0, i)),
        ],
        out_shape=[
            jax.ShapeDtypeStruct((n, D_MODEL), F32),
            jax.ShapeDtypeStruct((N_EXPERTS, n), F32),
        ],
        compiler_params=_cparams(("parallel",)),
        name="router",
    )(h2d, g, wr_pad)


def _select_kernel(aff_ref, incl_ref, tbl_ref, gval_ref, cnt_ref, slot_ref, *, cap, tt):
    bits = pltpu.bitcast(aff_ref[...], I32)
    nrow = bits.shape[1]

    def count(mask):
        c = jnp.sum(jnp.where(mask, 1, 0), axis=2, keepdims=True)
        return jnp.sum(c, axis=1, keepdims=True)

    def body(_, carry):
        lo, hi = carry
        mid = lo + ((hi - lo) >> 1)
        ok = count(bits >= mid) >= cap
        return jnp.where(ok, mid, lo), jnp.where(ok, hi, mid)

    lo0 = jnp.zeros((N_EXPERTS, 1, 1), I32)
    hi0 = jnp.full((N_EXPERTS, 1, 1), 0x7F800000, I32)
    thr, _ = lax.fori_loop(0, 31, body, (lo0, hi0))
    gt = bits > thr
    eq = bits == thr
    need = cap - count(gt)

    ri = lax.broadcasted_iota(I32, (LANES, LANES), 0)
    ci = lax.broadcasted_iota(I32, (LANES, LANES), 1)
    upper = jnp.where(ri <= ci, 1.0, 0.0).astype(BF16)
    rr = lax.broadcasted_iota(I32, (nrow, nrow), 0)
    rc = lax.broadcasted_iota(I32, (nrow, nrow), 1)
    lower_strict = jnp.where(rc < rr, 1.0, 0.0).astype(BF16)

    def incl_prefix(mask):
        x = jnp.where(mask, 1.0, 0.0).astype(BF16)
        incl = _dot(x.reshape(N_EXPERTS * nrow, LANES), upper).reshape(N_EXPERTS, nrow, LANES)
        tot = jnp.broadcast_to(incl[:, :, LANES - 1:LANES], incl.shape).astype(BF16)
        before = jnp.stack([_dot(lower_strict, tot[e]) for e in range(N_EXPERTS)], axis=0)
        return incl + before

    sel = gt | (eq & (incl_prefix(eq) - 1.0 < need.astype(F32)))
    incl_ref[...] = incl_prefix(sel).astype(I32)
    run = jnp.zeros((nrow, LANES), F32)
    for e in range(N_EXPERTS):
        slot_ref[e] = run
        run = run + jnp.where(sel[e], 1.0, 0.0)
    cnt_ref[...] = run.astype(I32)

    pf = lax.broadcasted_iota(I32, (1, cap), 1).astype(F32)
    jrow = lax.broadcasted_iota(I32, (nrow, cap), 0).astype(F32)
    lrow = lax.broadcasted_iota(I32, (LANES, cap), 0).astype(F32)

    def compact(e, carry):
        g = incl_ref[e].astype(F32)
        jsel = jnp.sum(jnp.where(g[:, LANES - 1:LANES] <= pf, 1.0, 0.0), axis=0, keepdims=True)
        onehot = jnp.where(jrow == jsel, 1.0, 0.0).astype(BF16)
        ghi = jnp.floor(g * (1.0 / 256.0))
        glo = g - 256.0 * ghi
        grow = 256.0 * _dot_tn(ghi.astype(BF16), onehot) + _dot_tn(glo.astype(BF16), onehot)
        lstar = jnp.sum(jnp.where(grow <= pf, 1.0, 0.0), axis=0, keepdims=True)
        lsel = lrow == lstar
        ah, am, al = _split3(aff_ref[e])
        arow = _dot_tn(ah, onehot) + _dot_tn(am, onehot) + _dot_tn(al, onehot)
        gval_ref[pl.ds(e, 1), :] = jnp.sum(jnp.where(lsel, arow, 0.0), axis=0, keepdims=True)
        krow = _dot_tn(slot_ref[e].astype(BF16), onehot)
        kk = jnp.sum(jnp.where(lsel, krow, 0.0), axis=0, keepdims=True)
        tok = (jsel * float(LANES) + lstar).astype(I32)
        dest = kk.astype(I32) * tt + (tok & (tt - 1))
        tbl_ref[pl.ds(e, 1), :] = tok | (dest << 16)
        return carry

    lax.fori_loop(0, N_EXPERTS, compact, 0)


def _select(aff3, cap, tt):
    nrow = aff3.shape[1]
    return pl.pallas_call(
        functools.partial(_select_kernel, cap=cap, tt=tt),
        out_shape=[
            jax.ShapeDtypeStruct(aff3.shape, I32),
            jax.ShapeDtypeStruct((N_EXPERTS, cap), I32),
            jax.ShapeDtypeStruct((N_EXPERTS, cap), F32),
            jax.ShapeDtypeStruct((nrow, LANES), I32),
        ],
        scratch_shapes=[pltpu.VMEM((N_EXPERTS, nrow, LANES), F32)],
        compiler_params=pltpu.CompilerParams(vmem_limit_bytes=VMEM_LIMIT),
        name="expert_select",
    )(aff3)


def _ffn_kernel(tbl_ref, hn_hbm, gval_ref, wg_ref, wu_ref, wd_ref, out_ref, xbuf, sem, *, tc, nt):
    step = pl.program_id(0) * nt + pl.program_id(1)
    last = N_EXPERTS * nt - 1
    slot = step % FFN_SLOTS
    ahead = FFN_SLOTS - 1

    def wait(slt):
        pltpu.make_async_copy(hn_hbm.at[pl.ds(0, tc)], xbuf.at[slt], sem.at[slt]).wait()

    @pl.when(step == 0)
    def _():
        for t in range(ahead):
            def body(i, carry, t=t):
                tok = tbl_ref[t * tc + i] & 0xFFFF
                pltpu.make_async_copy(hn_hbm.at[pl.ds(tok, 1)], xbuf.at[t, pl.ds(i, 1)], sem.at[t]).start()
                return carry
            lax.fori_loop(0, tc, body, 0, unroll=8)

    wait(slot)
    xe = xbuf[slot].astype(BF16)
    a = _dot(xe, wg_ref[0])
    u = _dot(xe, wu_ref[0])
    hmid = (a * _sigmoid(a) * u).astype(BF16)
    nxt = jnp.minimum(step + ahead, last)
    into = (step + ahead) % FFN_SLOTS
    for i in range(tc):
        tok = tbl_ref[nxt * tc + i] & 0xFFFF
        pltpu.make_async_copy(hn_hbm.at[pl.ds(tok, 1)], xbuf.at[into, pl.ds(i, 1)], sem.at[into]).start()
    y = _dot(hmid, wd_ref[0]) * gval_ref[0]
    for s in range(ROW_TILES):
        out_ref[:, s, :] = y[:, s * LANES:(s + 1) * LANES]

    @pl.when(step == last)
    def _():
        for t in range(1, FFN_SLOTS):
            wait((last + t) % FFN_SLOTS)


def _expert_ffn(tbl_flat, hn, gval, wg, wu, wd, cap):
    tc = min(256, cap)
    nt = cap // tc
    grid_spec = pltpu.PrefetchScalarGridSpec(
        num_scalar_prefetch=1,
        grid=(N_EXPERTS, nt),
        in_specs=[
            pl.BlockSpec(memory_space=pl.ANY),
            pl.BlockSpec((1, tc, 1), lambda e, j, idx: (e * nt + j, 0, 0)),
            pl.BlockSpec((1, D_MODEL, EXPERT_FF), lambda e, j, idx: (e, 0, 0)),
            pl.BlockSpec((1, D_MODEL, EXPERT_FF), lambda e, j, idx: (e, 0, 0)),
            pl.BlockSpec((1, EXPERT_FF, D_MODEL), lambda e, j, idx: (e, 0, 0)),
        ],
        out_specs=pl.BlockSpec((tc, ROW_TILES, LANES), lambda e, j, idx: (e * nt + j, 0, 0)),
        scratch_shapes=[pltpu.VMEM((FFN_SLOTS, tc, D_MODEL), F32), pltpu.SemaphoreType.DMA((FFN_SLOTS,))],
    )
    return pl.pallas_call(
        functools.partial(_ffn_kernel, tc=tc, nt=nt),
        grid_spec=grid_spec,
        out_shape=jax.ShapeDtypeStruct((N_EXPERTS * cap, ROW_TILES, LANES), F32),
        compiler_params=_cparams(("arbitrary", "arbitrary")),
        name="expert_ffn",
    )(tbl_flat, hn, gval.reshape(N_EXPERTS * nt, tc, 1), wg, wu, wd)


def _combine_kernel(tbl_ref, p0_ref, km_ref, h_ref, cnt_ref, g_ref, ye_hbm, out_ref, stage, sem,
                    *, tt, cap, ntile):
    tile = pl.program_id(0)
    slot = tile % COMBINE_SLOTS

    def issue(tl, slt):
        for e in range(N_EXPERTS):
            p0 = p0_ref[e * (ntile + 1) + tl]
            cnt = p0_ref[e * (ntile + 1) + tl + 1] - p0

            def fetch(q, e=e, p0=p0):
                row = e * cap + p0 + q
                pltpu.make_async_copy(ye_hbm.at[pl.ds(row, 1)], stage.at[slt, pl.ds(tbl_ref[row] >> 16, 1)],
                                      sem.at[slt]).start()

            def four(j, carry, fetch=fetch):
                for u in range(4):
                    fetch(4 * j + u)
                return carry

            def one(q, carry, fetch=fetch, cnt=cnt):
                fetch((cnt & ~3) + q)
                return carry

            lax.fori_loop(0, cnt >> 2, four, 0)
            lax.fori_loop(0, cnt & 3, one, 0)

    ahead = COMBINE_SLOTS - 1

    @pl.when(tile == 0)
    def _():
        for tl in range(min(ahead, ntile)):
            issue(tl, tl)

    @pl.when(tile + ahead < ntile)
    def _():
        issue(tile + ahead, (tile + ahead) % COMBINE_SLOTS)

    total = km_ref[2 * tile + 1]

    @pl.when(total > 0)
    def _():
        pltpu.make_async_copy(ye_hbm.at[pl.ds(0, total)], stage.at[slot, pl.ds(0, total)], sem.at[slot]).wait()

    cnt = cnt_ref[...]
    parts = []
    for s in range(ROW_TILES):
        def add(k, acc, s=s):
            rows = stage[slot, pl.ds(pl.multiple_of(k * tt, tt), tt), s, :]
            return acc + jnp.where(cnt > k, rows, 0.0)
        parts.append(lax.fori_loop(0, km_ref[2 * tile], add, h_ref[:, s * LANES:(s + 1) * LANES]))
    acc = jnp.concatenate(parts, axis=1)
    ms = jnp.mean(acc * acc, axis=-1, keepdims=True)
    out_ref[...] = acc * lax.rsqrt(ms + NORM_EPS) * g_ref[...]


def _combine(tbl_flat, p0_flat, km_flat, h2d, cnt_tok, g, yexp, cap, tt):
    n = h2d.shape[0]
    ntile = n // tt
    grid_spec = pltpu.PrefetchScalarGridSpec(
        num_scalar_prefetch=3,
        grid=(ntile,),
        in_specs=[
            pl.BlockSpec((tt, D_MODEL), lambda i, a, b, c: (i, 0)),
            pl.BlockSpec((tt, 1), lambda i, a, b, c: (i, 0)),
            pl.BlockSpec((1, D_MODEL), lambda i, a, b, c: (0, 0)),
            pl.BlockSpec(memory_space=pl.ANY),
        ],
        out_specs=pl.BlockSpec((tt, D_MODEL), lambda i, a, b, c: (i, 0)),
        scratch_shapes=[pltpu.VMEM((COMBINE_SLOTS, N_EXPERTS * tt, ROW_TILES, LANES), F32),
                        pltpu.SemaphoreType.DMA((COMBINE_SLOTS,))],
    )
    return pl.pallas_call(
        functools.partial(_combine_kernel, tt=tt, cap=cap, ntile=ntile),
        grid_spec=grid_spec,
        out_shape=jax.ShapeDtypeStruct((n, D_MODEL), F32),
        compiler_params=_cparams(("arbitrary",)),
        name="moe_combine",
    )(tbl_flat, p0_flat, km_flat, h2d, cnt_tok, g, yexp)


def _trunk(x, p, w_slab, mu_slab, rwp, moe_w, slopes):
    batch, seq, _ = x.shape
    n = batch * seq
    x2d = x.reshape(n, D_MODEL)
    slabs = _inproj(x2d, p["norm_mix_g"], w_slab)
    oatt = _attention(slabs, slopes, p["lambda_q1"], p["lambda_k1"], p["lambda_q2"], p["lambda_k2"],
                      p["subln_g"], batch, seq)
    rva, g, dirp = _rwprep(slabs, mu_slab, *rwp[:7], batch, seq)
    yf, yb = _rwscan(rva, dirp, rwp[7], batch, seq)
    h = _merge(x2d, oatt, yf, yb, g, slabs, p["gate_b"][0].reshape(2, 1, D_MODEL), rwp[8], rwp[9],
               moe_w["wba"], moe_w["wbr"], moe_w["wout"])

    cap = max(1, CAPACITY_FACTOR * n // N_EXPERTS)
    hn, aff = _router(h, p["norm_ffn_g"], moe_w["wr"])
    aff3 = aff.reshape(N_EXPERTS, n // LANES, LANES)
    assert n <= 1 << 16
    tt = LANES
    incl, tbl, gval, cnt = _select(aff3, cap, tt)
    ntile = n // tt
    ends = incl.reshape(N_EXPERTS, n)[:, tt - 1::tt]
    p0 = jnp.concatenate([jnp.zeros((N_EXPERTS, 1), I32), ends], axis=1)
    ct = cnt.reshape(ntile, tt)
    km = jnp.stack([jnp.max(ct, axis=1), jnp.sum(ct, axis=1)], axis=1)
    tbl_flat = tbl.reshape(-1)
    yexp = _expert_ffn(tbl_flat, hn, gval, moe_w["wg"], moe_w["wu"], moe_w["wd"], cap)
    y = _combine(tbl_flat, p0.reshape(-1), km.reshape(-1), h, cnt.reshape(n, 1),
                 p["norm_final_g"].reshape(1, D_MODEL), yexp, cap, tt)
    return y.reshape(batch, seq, D_MODEL)


def kernel(x_prompt, x_sample, norm_mix_g, w_in, shift_mu, lambda_q1, lambda_k1, lambda_q2, lambda_k2, subln_g, rw_w0, rw_w2, rw_a0, rw_a2, rw_g2, rw_k_k, rw_k_a, rw_r_k, lnx_g, lnx_b, gate_b, w_br_att, w_br_rw, w_out, norm_ffn_g, w_router, w_gate_e, w_up_e, w_down_e, norm_final_g):
    p = dict(norm_mix_g=norm_mix_g, lambda_q1=lambda_q1, lambda_k1=lambda_k1, lambda_q2=lambda_q2,
             lambda_k2=lambda_k2, subln_g=subln_g, rw_w0=rw_w0, rw_w2=rw_w2, rw_a0=rw_a0, rw_a2=rw_a2,
             rw_g2=rw_g2, rw_k_k=rw_k_k, rw_k_a=rw_k_a, rw_r_k=rw_r_k, lnx_g=lnx_g, lnx_b=lnx_b,
             gate_b=gate_b, norm_ffn_g=norm_ffn_g, norm_final_g=norm_final_g)
    w_slab, mu_slab = _prep_in_weights(w_in[0], shift_mu[0])
    rwp = _prep_rw_params(p)
    moe_w = dict(
        wba=w_br_att[0].astype(BF16), wbr=w_br_rw[0].astype(BF16), wout=w_out[0].astype(BF16),
        wr=jnp.pad(w_router[0], ((0, 0), (0, LANES - N_EXPERTS))),
        wg=w_gate_e[0].astype(BF16), wu=w_up_e[0].astype(BF16), wd=w_down_e[0].astype(BF16))
    slopes = jnp.asarray([2.0 ** (-8.0 * (i + 1) / ATT_HEADS) for i in range(ATT_HEADS)], F32)
    return (_trunk(x_prompt, p, w_slab, mu_slab, rwp, moe_w, slopes),
            _trunk(x_sample, p, w_slab, mu_slab, rwp, moe_w, slopes))
```

```python
import functools
import math

import jax
import jax.numpy as jnp
from jax import lax
from jax.experimental import pallas as pl
from jax.experimental.pallas import tpu as pltpu

F32 = jnp.float32
BF16 = jnp.bfloat16
I32 = jnp.int32

D_MODEL = 2048
ATT_HEADS = 8
ATT_HEAD_DIM = 64
ATT_WIDTH = ATT_HEADS * 2 * ATT_HEAD_DIM
RW_HEAD = 64
RW_WIDTH = 1024
DECAY_LORA = 96
ICLR_LORA = 96
GATE_LORA = 256
SHIFT_WIDTH = 3 * RW_WIDTH + DECAY_LORA + ICLR_LORA + GATE_LORA
N_EXPERTS = 16
CAPACITY_FACTOR = 2
EXPERT_FF = 1024
NORM_EPS = 1e-6
SUBLN_EPS = 1e-5
LNX_EPS = 64e-5
LAM_INIT = 0.8 - 0.6 * math.exp(-0.3 * 0)

LANES = 128
VMEM_LIMIT = 56 * 1024 * 1024

SL_GATE_ATT, SL_GATE_RW = 0, 16
SL_ATT_Q, SL_ATT_K, SL_ATT_V = 32, 40, 48
SL_RW = 56
N_RW_SLABS = 28
N_SLABS = 84
CHUNK = 64
COMBINE_SLOTS = 2
FFN_SLOTS = 3
HALF_D = D_MODEL // 2


def _cparams(sem):
    return pltpu.CompilerParams(dimension_semantics=sem, vmem_limit_bytes=VMEM_LIMIT)


def _sigmoid(x):
    return 1.0 / (1.0 + jnp.exp(-x))


def _split3(x):
    hi = x.astype(BF16)
    r1 = x - hi.astype(F32)
    mid = r1.astype(BF16)
    lo = (r1 - mid.astype(F32)).astype(BF16)
    return hi, mid, lo


def _dot(a, b):
    return jnp.dot(a, b, preferred_element_type=F32)


def _dot_nt(a, b):
    return lax.dot_general(a, b, (((1,), (1,)), ((), ())), preferred_element_type=F32)


def _dot_tn(a, b):
    return lax.dot_general(a, b, (((0,), (0,)), ((), ())), preferred_element_type=F32)


def _dot_f32(a_bf16_exact, x):
    hi, mid, lo = _split3(x)
    return _dot(a_bf16_exact, hi) + _dot(a_bf16_exact, mid) + _dot(a_bf16_exact, lo)


def _rearrange_in_cols(a):
    att = a[..., :3 * ATT_WIDTH]
    zr = a[..., 3 * ATT_WIDTH:3 * ATT_WIDTH + SHIFT_WIDTH]
    gates = a[..., 3 * ATT_WIDTH + SHIFT_WIDTH:]
    o3 = 3 * RW_WIDTH
    o4 = o3 + DECAY_LORA
    o5 = o4 + ICLR_LORA
    pad = [(0, 0)] * (a.ndim - 1)
    lw = jnp.pad(zr[..., o3:o4], pad + [(0, LANES - DECAY_LORA)])
    la = jnp.pad(zr[..., o4:o5], pad + [(0, LANES - ICLR_LORA)])
    return jnp.concatenate([gates, att, zr[..., :o3], lw, la, zr[..., o5:]], axis=-1)


def _prep_in_weights(w_in, shift_mu):
    w_slab = _rearrange_in_cols(w_in).astype(BF16)
    mu_full = jnp.pad(shift_mu, ((0, 0), (3 * ATT_WIDTH, 2 * D_MODEL)))
    mu_slab = _rearrange_in_cols(mu_full)[:, SL_RW * LANES:]
    return w_slab, mu_slab.reshape(2, 1, N_RW_SLABS * LANES)


def _inproj_kernel(x_ref, g_ref, w_ref, o_ref, xn_ref, *, n_out_slabs):
    @pl.when(pl.program_id(1) == 0)
    def _():
        x = x_ref[...]
        ms = jnp.mean(x * x, axis=-1, keepdims=True)
        xn_ref[...] = (x * lax.rsqrt(ms + NORM_EPS) * g_ref[...]).astype(BF16)

    acc = _dot(xn_ref[...], w_ref[...])
    for c in range(n_out_slabs):
        o_ref[c] = acc[:, c * LANES:(c + 1) * LANES].astype(BF16)


def _inproj(x2d, g, w_slab):
    n = x2d.shape[0]
    tm = min(1024, n)
    tn = 1536
    n_out_slabs = tn // LANES
    grid = (n // tm, (N_SLABS * LANES) // tn)
    return pl.pallas_call(
        functools.partial(_inproj_kernel, n_out_slabs=n_out_slabs),
        grid=grid,
        in_specs=[
            pl.BlockSpec((tm, D_MODEL), lambda i, j: (i, 0)),
            pl.BlockSpec((1, D_MODEL), lambda i, j: (0, 0)),
            pl.BlockSpec((D_MODEL, tn), lambda i, j: (0, j)),
        ],
        out_specs=pl.BlockSpec((n_out_slabs, tm, LANES), lambda i, j: (j, i, 0)),
        out_shape=jax.ShapeDtypeStruct((N_SLABS, n, LANES), BF16),
        scratch_shapes=[pltpu.VMEM((tm, D_MODEL), BF16)],
        compiler_params=_cparams(("parallel", "arbitrary")),
        name="inproj",
    )(x2d, g, w_slab)


def _attn_kernel(slopes_ref, lq1_ref, lk1_ref, lq2_ref, lk2_ref, subg_ref, q_ref, k_ref, v_ref, o_ref,
                 kt1_ref, kt2_ref, vaug_ref, *, seq, tq):
    h = pl.program_id(1)
    qi = pl.program_id(2)
    slope = slopes_ref[h]
    q0 = pl.multiple_of(qi * tq, tq)
    view = pl.ds(q0, seq)

    @pl.when(qi == 0)
    def _():
        kt = k_ref[0].astype(F32).T
        row = lax.broadcasted_iota(I32, kt.shape, 0)
        k1 = jnp.where(row < ATT_HEAD_DIM, kt, 0.0).astype(BF16)
        k2 = jnp.where(row >= ATT_HEAD_DIM, kt, 0.0).astype(BF16)
        lane = lax.broadcasted_iota(I32, (seq, LANES), 1)
        va = jnp.concatenate([v_ref[0], jnp.where(lane == 0, 1.0, 0.0).astype(BF16)], axis=1)
        for half in (slice(0, seq), slice(seq, 2 * seq)):
            kt1_ref[:, half] = k1
            kt2_ref[:, half] = k2
            vaug_ref[half, :] = va

    col = lax.broadcasted_iota(I32, (16, seq), 1)
    r16 = lax.broadcasted_iota(I32, (16, seq), 0)
    wrapped = col + q0 >= seq
    jp = jnp.where(wrapped, col - seq, col)
    sigma = jnp.where(col < tq, 0.0, jnp.where(wrapped, -1.0, 1.0)).astype(F32)
    jh = (jp >> 8).astype(F32)
    jl = (jp & 255).astype(F32)
    feat = jnp.where(r16 <= 1, sigma,
                     jnp.where(r16 == 2, -sigma * (slope * 256.0) * jh,
                               jnp.where(r16 == 3, -sigma * slope * jl, 0.0))).astype(BF16)
    kt1_ref[ATT_HEAD_DIM:ATT_HEAD_DIM + 16, view] = feat
    kt2_ref[0:16, view] = feat

    q = q_ref[0].astype(F32) * (ATT_HEAD_DIM ** -0.5)
    lane = lax.broadcasted_iota(I32, (tq, LANES), 1)
    ip = lax.broadcasted_iota(I32, (tq, LANES), 0)
    ih = slope * (ip & ~255).astype(F32)
    il = slope * (ip & 255).astype(F32)

    def query_side(fl):
        return jnp.where(fl == 0, ih, jnp.where(fl == 1, il, jnp.where(fl <= 3, 1.0, 0.0)))

    lhs1 = jnp.where(lane < ATT_HEAD_DIM, q, query_side(lane - ATT_HEAD_DIM)).astype(BF16)
    lhs2 = jnp.where(lane >= ATT_HEAD_DIM, q, query_side(lane)).astype(BF16)

    di = lax.broadcasted_iota(I32, (tq, tq), 0)
    dj = lax.broadcasted_iota(I32, (tq, tq), 1)
    diag_bias = -slope * jnp.abs(di - dj).astype(F32)

    def weights(lhs, kt_ref):
        s = _dot(lhs, kt_ref[:, view])
        s = jnp.concatenate([s[:, :tq] + diag_bias, s[:, tq:]], axis=1)
        m = jnp.max(s, axis=-1, keepdims=True)
        return jnp.exp(s - m).astype(BF16)

    e = jnp.concatenate([weights(lhs1, kt1_ref), weights(lhs2, kt2_ref)], axis=0)
    oa = _dot(e, vaug_ref[view, :])
    o1 = oa[:tq, :LANES] / oa[:tq, LANES:LANES + 1]
    o2 = oa[tq:, :LANES] / oa[tq:, LANES:LANES + 1]
    lam = (jnp.exp(jnp.sum(lq1_ref[...] * lk1_ref[...], keepdims=True))
           - jnp.exp(jnp.sum(lq2_ref[...] * lk2_ref[...], keepdims=True)) + LAM_INIT)
    out = o1 - lam * o2
    ms = jnp.mean(out * out, axis=-1, keepdims=True)
    y = out * lax.rsqrt(ms + SUBLN_EPS) * subg_ref[...]
    o_ref[0] = (y * (1.0 - LAM_INIT)).astype(BF16)


def _attention(slabs, slopes, lq1, lk1, lq2, lk2, subg, batch, seq):
    n = batch * seq
    tq = 512
    nq = seq // tq
    vec = lambda: pl.BlockSpec((1, ATT_HEAD_DIM), lambda b, h, i: (0, 0))
    return pl.pallas_call(
        functools.partial(_attn_kernel, seq=seq, tq=tq),
        grid=(batch, ATT_HEADS, nq),
        in_specs=[
            pl.BlockSpec(memory_space=pltpu.SMEM),
            vec(), vec(), vec(), vec(),
            pl.BlockSpec((1, LANES), lambda b, h, i: (0, 0)),
            pl.BlockSpec((1, tq, LANES), lambda b, h, i: (SL_ATT_Q + h, b * nq + i, 0)),
            pl.BlockSpec((1, seq, LANES), lambda b, h, i: (SL_ATT_K + h, b, 0)),
            pl.BlockSpec((1, seq, LANES), lambda b, h, i: (SL_ATT_V + h, b, 0)),
        ],
        out_specs=pl.BlockSpec((1, tq, LANES), lambda b, h, i: (h, b * nq + i, 0)),
        out_shape=jax.ShapeDtypeStruct((ATT_HEADS, n, LANES), BF16),
        scratch_shapes=[
            pltpu.VMEM((LANES, 2 * seq), BF16),
            pltpu.VMEM((LANES, 2 * seq), BF16),
            pltpu.VMEM((2 * seq, 2 * LANES), BF16),
        ],
        compiler_params=_cparams(("parallel", "parallel", "arbitrary")),
        name="diff_attn",
    )(slopes, lq1, lk1, lq2, lk2, subg, slabs, slabs, slabs)


def _head_segsum(x, bd):
    hi, mid, lo = _split3(x)
    return _dot(hi, bd) + _dot(mid, bd) + _dot(lo, bd)


def _block_ones():
    ri = lax.broadcasted_iota(I32, (LANES, LANES), 0)
    ci = lax.broadcasted_iota(I32, (LANES, LANES), 1)
    return jnp.where((ri >> 6) == (ci >> 6), 1.0, 0.0).astype(BF16)


def _rwprep_kernel(main_ref, prev_ref, next_ref, mu_ref, w0_ref, a0_ref, kk_ref, ka_ref, w2_ref, a2_ref,
                   g2_ref, rva_ref, g_ref, dirp_ref, *, nt):
    i = pl.program_id(1)
    t = main_ref.shape[1]
    hb = prev_ref.shape[1]
    wide = lambda ref: jnp.concatenate([ref[c] for c in range(N_RW_SLABS)], axis=1)
    zb = wide(main_ref)
    prev = jnp.where(i > 0, wide(prev_ref), jnp.zeros((), BF16))
    nxt = jnp.where(i < nt - 1, wide(next_ref), jnp.zeros((), BF16))
    halo = jnp.concatenate([prev, zb, nxt], axis=0)
    ri = lax.broadcasted_iota(I32, (t, t + 2 * hb), 0)
    ci = lax.broadcasted_iota(I32, (t, t + 2 * hb), 1)
    zp = _dot(jnp.where(ci == ri + hb - 1, 1.0, 0.0).astype(BF16), halo)
    zn = _dot(jnp.where(ci == ri + hb + 1, 1.0, 0.0).astype(BF16), halo)
    z = zb.astype(F32)
    z = z + mu_ref[0] * (zp - z) + mu_ref[1] * (zn - z)
    slab = lambda s: z[:, s * LANES:(s + 1) * LANES]

    xw = jnp.tanh(slab(24)).astype(BF16)
    xa = slab(25).astype(BF16)
    xg = _sigmoid(z[:, 26 * LANES:28 * LANES]).astype(BF16)
    g_full = _dot(xg, g2_ref[...])
    lw = [_dot(xw, w2_ref[d]) for d in range(2)]
    la = [_dot(xa, a2_ref[d]) for d in range(2)]
    bd = _block_ones()
    for c in range(8):
        cs = slice(c * LANES, (c + 1) * LANES)
        kc = slab(8 + c)
        kk = kc * kk_ref[c]
        nrm = jnp.sqrt(_head_segsum(kk * kk, bd))
        kk = kk / jnp.maximum(nrm, 1e-12)
        rva_ref[0, c] = slab(c)
        rva_ref[1, c] = slab(16 + c)
        rva_ref[2, c] = -kk
        g_ref[c] = g_full[:, cs]
        for d in range(2):
            dirp_ref[d, 2, c] = -math.exp(-0.5) * _sigmoid(w0_ref[d, c] + lw[d][:, cs])
            asig = _sigmoid(a0_ref[d, c] + la[d][:, cs])
            dirp_ref[d, 0, c] = kc * (1.0 + (asig - 1.0) * ka_ref[c])
            dirp_ref[d, 1, c] = kk * asig


def _prep_rw_params(p):
    vec = lambda a: a.reshape(a.shape[:-1] + (8, 1, LANES))
    pad_rows = lambda a: jnp.pad(a, ((0, 0), (0, LANES - a.shape[1]), (0, 0))).astype(BF16)
    return (vec(p["rw_w0"][0]), vec(p["rw_a0"][0]), vec(p["rw_k_k"][0]), vec(p["rw_k_a"][0]),
            pad_rows(p["rw_w2"][0]), pad_rows(p["rw_a2"][0]), p["rw_g2"][0].astype(BF16),
            vec(p["rw_r_k"][0].reshape(RW_WIDTH)), vec(p["lnx_g"][0]), vec(p["lnx_b"][0]))


def _rwprep(slabs, mu_slab, w0, a0, k_k, k_a, w2, a2, g2, batch, seq):
    n = batch * seq
    t = 256
    nt = seq // t
    hb = 16
    full = lambda shape: pl.BlockSpec(shape, lambda b, i: (0,) * len(shape))
    rows = lambda b, i: b * nt + i
    out_specs = [pl.BlockSpec((3, 8, t, LANES), lambda b, i: (0, 0, rows(b, i), 0)),
                 pl.BlockSpec((8, t, LANES), lambda b, i: (0, rows(b, i), 0)),
                 pl.BlockSpec((2, 3, 8, t, LANES), lambda b, i: (0, 0, 0, rows(b, i), 0))]
    out_shape = [jax.ShapeDtypeStruct((3, 8, n, LANES), F32), jax.ShapeDtypeStruct((8, n, LANES), F32),
                 jax.ShapeDtypeStruct((2, 3, 8, n, LANES), F32)]
    rw_blk = SL_RW // N_RW_SLABS
    return pl.pallas_call(
        functools.partial(_rwprep_kernel, nt=nt),
        grid=(batch, nt),
        in_specs=[
            pl.BlockSpec((N_RW_SLABS, t, LANES), lambda b, i: (rw_blk, rows(b, i), 0)),
            pl.BlockSpec((N_RW_SLABS, hb, LANES),
                         lambda b, i: (rw_blk, jnp.maximum((b * seq + i * t) // hb - 1, 0), 0)),
            pl.BlockSpec((N_RW_SLABS, hb, LANES),
                         lambda b, i: (rw_blk, jnp.minimum((b * seq + (i + 1) * t) // hb, n // hb - 1), 0)),
            full((2, 1, N_RW_SLABS * LANES)),
            full((2, 8, 1, LANES)), full((2, 8, 1, LANES)), full((8, 1, LANES)), full((8, 1, LANES)),
            full((2, LANES, RW_WIDTH)), full((2, LANES, RW_WIDTH)), full((GATE_LORA, RW_WIDTH)),
        ],
        out_specs=out_specs,
        out_shape=out_shape,
        compiler_params=_cparams(("parallel", "parallel")),
        name="rwkv_prep",
    )(slabs, slabs, slabs, mu_slab, w0, a0, k_k, k_a, w2, a2, g2)


def _rwscan_kernel(rvaf_ref, dirf_ref, rvab_ref, dirb_ref, rk_ref, yf_ref, yb_ref, st_ref, *, nc, group):
    @pl.when(pl.program_id(2) == 0)
    def _():
        st_ref[...] = jnp.zeros_like(st_ref)

    lane = lax.broadcasted_iota(I32, (CHUNK, LANES), 1)
    head0 = lane < RW_HEAD
    ri = lax.broadcasted_iota(I32, (LANES, LANES), 0)
    ci = lax.broadcasted_iota(I32, (LANES, LANES), 1)
    same = (ri >> 6) == (ci >> 6)
    tt = ri & (CHUNK - 1)
    ss = ci & (CHUNK - 1)
    eye = jnp.where(ri == ci, 1.0, 0.0).astype(F32)
    tr = lax.broadcasted_iota(I32, (CHUNK, CHUNK), 0)
    tc = lax.broadcasted_iota(I32, (CHUNK, CHUNK), 1)
    bd = _block_ones()
    rk = rk_ref[0]

    def stack(x):
        return jnp.concatenate([jnp.where(head0, x, 0.0), jnp.where(head0, 0.0, x)], axis=0)

    dirs = ((rvaf_ref, dirf_ref, yf_ref), (rvab_ref, dirb_ref, yb_ref))
    strict = (same & (ss < tt), same & (ss > tt))
    incl = (same & (ss <= tt), same & (ss >= tt))
    tri = (jnp.where(tc <= tr, 1.0, 0.0).astype(BF16), jnp.where(tc >= tr, 1.0, 0.0).astype(BF16))
    last = (CHUNK - 1, 0)
    all_insts = [(d, k if d == 0 else nc - 1 - k) for k in range(nc) for d in range(2)]

    def load(which, j, insts):
        lead = (j, 0) if which == 0 else (0, j, 0)
        return [dirs[d][which][lead + (slice(ch * CHUNK, (ch + 1) * CHUNK), slice(None))] for d, ch in insts]

    def state_free_part(insts, out):
        every = range(len(insts))
        r, v, na = (load(0, j, insts) for j in range(3))
        kd, b, ld = (load(1, j, insts) for j in range(3))
        tri3 = [jnp.concatenate([t, t, t], axis=1) for t in tri]
        c = [_dot(tri3[insts[i][0]], jnp.concatenate(_split3(ld[i]), axis=0)) for i in every]
        total = [c[i][last[insts[i][0]]:last[insts[i][0]] + 1] for i in every]
        yield
        e_nc = [jnp.exp(-c[i]) for i in every]
        e_tc = [jnp.exp(total[i] - c[i]) for i in every]
        a_t = [stack(na[i] * jnp.exp(c[i] - ld[i])).astype(BF16) for i in every]
        r_t = [stack(r[i] * jnp.exp(c[i])).astype(BF16) for i in every]
        v_s = [stack(v[i]).astype(BF16) for i in every]
        rhs = [jnp.concatenate([stack(b[i] * e_nc[i]), stack(kd[i] * e_nc[i])], axis=0).astype(BF16) for i in every]
        bk = [jnp.concatenate([stack(b[i] * e_tc[i]), stack(kd[i] * e_tc[i])], axis=0).astype(BF16) for i in every]
        yield
        p = [_dot_nt(jnp.concatenate([a_t[i], r_t[i]], axis=0), rhs[i]) for i in every]
        yield
        n_ab = [jnp.where(strict[insts[i][0]], p[i][:LANES, :LANES], 0.0) for i in every]
        a_ak = [jnp.where(strict[insts[i][0]], p[i][:LANES, LANES:], 0.0).astype(BF16) for i in every]
        p_rb = [jnp.where(incl[insts[i][0]], p[i][LANES:, :LANES], 0.0).astype(BF16) for i in every]
        p_rk = [jnp.where(incl[insts[i][0]], p[i][LANES:, LANES:], 0.0).astype(BF16) for i in every]
        yield
        x = [eye + n_ab[i] for i in every]
        nk = [n_ab[i].astype(BF16) for i in every]
        nk = [_dot(nk[i], nk[i]) for i in every]
        for _ in range(4):
            yield
            both = [_dot(nk[i].astype(BF16), jnp.concatenate([nk[i], x[i]], axis=1).astype(BF16)) for i in every]
            nk = [both[i][:, :LANES] for i in every]
            x = [x[i] + both[i][:, LANES:] for i in every]
        yield
        x = [x[i] + _dot(nk[i].astype(BF16), x[i].astype(BF16)) for i in every]
        w = [_dot(a_ak[i], v_s[i]) for i in every]
        yield
        au = [_dot(x[i].astype(BF16), jnp.concatenate([a_t[i], w[i].astype(BF16)], axis=1)) for i in every]
        yield
        rpp = [jnp.concatenate([r_t[i], p_rb[i], p_rk[i]], axis=1) for i in every]
        bonus = [_head_segsum(r[i] * kd[i] * rk, bd) * v[i] for i in every]
        u0_t = [au[i][:, LANES:].T for i in every]
        v_t = [v_s[i].astype(F32).T.astype(BF16) for i in every]
        out.extend(dict(a_hat=au[i][:, :LANES].astype(BF16), u0_t=u0_t[i], v_t=v_t[i], rpp=rpp[i], bk=bk[i],
                        decay=jnp.exp(total[i]), bonus=bonus[i]) for i in every)

    groups = [all_insts[g:g + group] for g in range(0, len(all_insts), group)]
    pre = [[] for _ in groups]
    parts = [state_free_part(gr, pre[gi]) for gi, gr in enumerate(groups)]
    for _ in parts[0]:
        pass
    st = [st_ref[0], st_ref[1]]
    for gi, gr in enumerate(groups):
        upcoming = parts[gi + 1] if gi + 1 < len(groups) else iter(())
        for (d, ch), f in zip(gr, pre[gi]):
            st_b = st[d].astype(BF16)
            u_t = (_dot_nt(st_b, f["a_hat"]) + f["u0_t"]).astype(BF16)
            uv_t = jnp.concatenate([u_t, f["v_t"]], axis=1)
            y_t = _dot_nt(jnp.concatenate([st_b, uv_t], axis=1), f["rpp"])
            st[d] = st[d] * f["decay"] + _dot(uv_t, f["bk"])
            y = y_t.T
            dirs[d][2][0, ch * CHUNK:(ch + 1) * CHUNK, :] = y[:CHUNK] + y[CHUNK:] + f["bonus"]
            next(upcoming, None)
            next(upcoming, None)
        for _ in upcoming:
            pass
    st_ref[0] = st[0]
    st_ref[1] = st[1]


def _rwscan(rva, dirp, r_k, batch, seq):
    n = batch * seq
    nc = min(16, seq // CHUNK)
    tcs = CHUNK * nc
    nt = seq // tcs
    fwd = lambda bi, c, t: bi * nt + t
    bwd = lambda bi, c, t: bi * nt + nt - 1 - t
    s3 = lambda rows: pl.BlockSpec((1, tcs, LANES), lambda bi, c, t: (c, rows(bi, c, t), 0))
    s_rva = lambda rows: pl.BlockSpec((3, 1, tcs, LANES), lambda bi, c, t: (0, c, rows(bi, c, t), 0))
    s_dir = lambda d, rows: pl.BlockSpec((1, 3, 1, tcs, LANES), lambda bi, c, t: (d, 0, c, rows(bi, c, t), 0))
    out = jax.ShapeDtypeStruct((8, n, LANES), F32)
    return pl.pallas_call(
        functools.partial(_rwscan_kernel, nc=nc, group=2 * nc),
        grid=(batch, 8, nt),
        in_specs=[s_rva(fwd), s_dir(0, fwd), s_rva(bwd), s_dir(1, bwd),
                  pl.BlockSpec((1, 1, LANES), lambda bi, c, t: (c, 0, 0))],
        out_specs=[s3(fwd), s3(bwd)],
        out_shape=[out, out],
        scratch_shapes=[pltpu.VMEM((2, LANES, LANES), F32)],
        compiler_params=_cparams(("parallel", "parallel", "arbitrary")),
        name="rwkv_scan",
    )(rva, dirp, rva, dirp, r_k)


def _merge_kernel(x_ref, oatt_ref, yf_ref, yb_ref, g_ref, gates_ref, gb_ref, lng_ref, lnb_ref,
                  wba_ref, wbr_ref, wout_ref, h_ref):
    bd = _block_ones()
    orw = []
    for c in range(8):
        y = yf_ref[c] + yb_ref[c]
        mu = _head_segsum(y, bd) * (1.0 / RW_HEAD)
        yc = y - mu
        var = _head_segsum(yc * yc, bd) * (1.0 / RW_HEAD)
        yn = yc * lax.rsqrt(var + LNX_EPS) * lng_ref[c] + lnb_ref[c]
        orw.append((yn * g_ref[c]).astype(BF16))
    orw = jnp.concatenate(orw, axis=1)
    oatt = jnp.concatenate([oatt_ref[c] for c in range(8)], axis=1)
    ga = jnp.concatenate([gates_ref[c] for c in range(16)], axis=1).astype(F32) + gb_ref[0]
    gr = jnp.concatenate([gates_ref[16 + c] for c in range(16)], axis=1).astype(F32) + gb_ref[1]
    merged = _sigmoid(ga) * _dot(oatt, wba_ref[...]) + _sigmoid(gr) * _dot(orw, wbr_ref[...])
    h_ref[...] = x_ref[...] + _dot(merged.astype(BF16), wout_ref[...])


def _merge(x2d, oatt, yf, yb, g, slabs, gate_b, lng, lnb, wba, wbr, wout):
    n = x2d.shape[0]
    tm = min(256, n)
    const = lambda shape: pl.BlockSpec(shape, lambda i: (0,) * len(shape), pipeline_mode=pl.Buffered(1))
    s8 = pl.BlockSpec((8, tm, LANES), lambda i: (0, i, 0))
    return pl.pallas_call(
        _merge_kernel,
        grid=(n // tm,),
        in_specs=[
            pl.BlockSpec((tm, D_MODEL), lambda i: (i, 0)),
            s8, s8, s8, s8,
            pl.BlockSpec((32, tm, LANES), lambda i: (0, i, 0)),
            const((2, 1, D_MODEL)), const((8, 1, LANES)), const((8, 1, LANES)),
            const((ATT_WIDTH, D_MODEL)), const((RW_WIDTH, D_MODEL)), const((D_MODEL, D_MODEL)),
        ],
        out_specs=pl.BlockSpec((tm, D_MODEL), lambda i: (i, 0)),
        out_shape=jax.ShapeDtypeStruct((n, D_MODEL), F32),
        compiler_params=_cparams(("parallel",)),
        name="merge_outproj",
    )(x2d, oatt, yf, yb, g, slabs, gate_b, lng, lnb, wba, wbr, wout)


def _router_kernel(h_ref, g_ref, wr_ref, hn_ref, aff_ref):
    x = h_ref[...]
    ms = jnp.mean(x * x, axis=-1, keepdims=True)
    hn = x * lax.rsqrt(ms + NORM_EPS) * g_ref[...]
    hn_ref[...] = hn
    xh, xm, _ = _split3(hn)
    wh, wm, _ = _split3(wr_ref[...])
    logits = _dot(xh, wh) + _dot(xh, wm) + _dot(xm, wh)
    lt = logits.T[:N_EXPERTS]
    m = jnp.max(lt, axis=0, keepdims=True)
    e = jnp.exp(lt - m)
    aff_ref[...] = e / jnp.sum(e, axis=0, keepdims=True)


def _router(h2d, g, wr_pad):
    n = h2d.shape[0]
    tm = min(256, n)
    return pl.pallas_call(
        _router_kernel,
        grid=(n // tm,),
        in_specs=[
            pl.BlockSpec((tm, D_MODEL), lambda i: (i, 0)),
            pl.BlockSpec((1, D_MODEL), lambda i: (0, 0)),
            pl.BlockSpec((D_MODEL, LANES), lambda i: (0, 0)),
        ],
        out_specs=[
            pl.BlockSpec((tm, D_MODEL), lambda i: (i, 0)),
            pl.BlockSpec((N_EXPERTS, tm), lambda i: (0, i)),
        ],
        out_shape=[
            jax.ShapeDtypeStruct((n, D_MODEL), F32),
            jax.ShapeDtypeStruct((N_EXPERTS, n), F32),
        ],
        compiler_params=_cparams(("parallel",)),
        name="router",
    )(h2d, g, wr_pad)


def _select_kernel(aff_ref, incl_ref, tbl_ref, gval_ref, cnt_ref, slot_ref, *, cap, tt):
    bits = pltpu.bitcast(aff_ref[...], I32)
    nrow = bits.shape[1]

    def count(mask):
        c = jnp.sum(jnp.where(mask, 1, 0), axis=2, keepdims=True)
        return jnp.sum(c, axis=1, keepdims=True)

    def body(_, carry):
        lo, hi = carry
        mid = lo + ((hi - lo) >> 1)
        ok = count(bits >= mid) >= cap
        return jnp.where(ok, mid, lo), jnp.where(ok, hi, mid)

    lo0 = jnp.zeros((N_EXPERTS, 1, 1), I32)
    hi0 = jnp.full((N_EXPERTS, 1, 1), 0x7F800000, I32)
    thr, _ = lax.fori_loop(0, 31, body, (lo0, hi0))
    gt = bits > thr
    eq = bits == thr
    need = cap - count(gt)

    ri = lax.broadcasted_iota(I32, (LANES, LANES), 0)
    ci = lax.broadcasted_iota(I32, (LANES, LANES), 1)
    upper = jnp.where(ri <= ci, 1.0, 0.0).astype(BF16)
    rr = lax.broadcasted_iota(I32, (nrow, nrow), 0)
    rc = lax.broadcasted_iota(I32, (nrow, nrow), 1)
    lower_strict = jnp.where(rc < rr, 1.0, 0.0).astype(BF16)

    def incl_prefix(mask):
        x = jnp.where(mask, 1.0, 0.0).astype(BF16)
        incl = _dot(x.reshape(N_EXPERTS * nrow, LANES), upper).reshape(N_EXPERTS, nrow, LANES)
        tot = jnp.broadcast_to(incl[:, :, LANES - 1:LANES], incl.shape).astype(BF16)
        before = jnp.stack([_dot(lower_strict, tot[e]) for e in range(N_EXPERTS)], axis=0)
        return incl + before

    sel = gt | (eq & (incl_prefix(eq) - 1.0 < need.astype(F32)))
    incl_ref[...] = incl_prefix(sel).astype(I32)
    run = jnp.zeros((nrow, LANES), F32)
    for e in range(N_EXPERTS):
        slot_ref[e] = run
        run = run + jnp.where(sel[e], 1.0, 0.0)
    cnt_ref[...] = run.astype(I32)

    pf = lax.broadcasted_iota(I32, (1, cap), 1).astype(F32)
    jrow = lax.broadcasted_iota(I32, (nrow, cap), 0).astype(F32)
    lrow = lax.broadcasted_iota(I32, (LANES, cap), 0).astype(F32)

    def compact(e, carry):
        g = incl_ref[e].astype(F32)
        jsel = jnp.sum(jnp.where(g[:, LANES - 1:LANES] <= pf, 1.0, 0.0), axis=0, keepdims=True)
        onehot = jnp.where(jrow == jsel, 1.0, 0.0).astype(BF16)
        ghi = jnp.floor(g * (1.0 / 256.0))
        glo = g - 256.0 * ghi
        grow = 256.0 * _dot_tn(ghi.astype(BF16), onehot) + _dot_tn(glo.astype(BF16), onehot)
        lstar = jnp.sum(jnp.where(grow <= pf, 1.0, 0.0), axis=0, keepdims=True)
        lsel = lrow == lstar
        ah, am, al = _split3(aff_ref[e])
        arow = _dot_tn(ah, onehot) + _dot_tn(am, onehot) + _dot_tn(al, onehot)
        gval_ref[pl.ds(e, 1), :] = jnp.sum(jnp.where(lsel, arow, 0.0), axis=0, keepdims=True)
        krow = _dot_tn(slot_ref[e].astype(BF16), onehot)
        kk = jnp.sum(jnp.where(lsel, krow, 0.0), axis=0, keepdims=True)
        tok = (jsel * float(LANES) + lstar).astype(I32)
        dest = kk.astype(I32) * tt + (tok & (tt - 1))
        tbl_ref[pl.ds(e, 1), :] = tok | (dest << 16)
        return carry

    lax.fori_loop(0, N_EXPERTS, compact, 0)


def _select(aff3, cap, tt):
    nrow = aff3.shape[1]
    return pl.pallas_call(
        functools.partial(_select_kernel, cap=cap, tt=tt),
        out_shape=[
            jax.ShapeDtypeStruct(aff3.shape, I32),
            jax.ShapeDtypeStruct((N_EXPERTS, cap), I32),
            jax.ShapeDtypeStruct((N_EXPERTS, cap), F32),
            jax.ShapeDtypeStruct((nrow, LANES), I32),
        ],
        scratch_shapes=[pltpu.VMEM((N_EXPERTS, nrow, LANES), F32)],
        compiler_params=pltpu.CompilerParams(vmem_limit_bytes=VMEM_LIMIT),
        name="expert_select",
    )(aff3)


def _ffn_kernel(tbl_ref, hn_hbm, gval_ref, wg_ref, wu_ref, wd_ref, out_ref, xbuf, sem, *, tc, nt):
    step = pl.program_id(0) * nt + pl.program_id(1)
    last = N_EXPERTS * nt - 1
    slot = step % FFN_SLOTS
    ahead = FFN_SLOTS - 1

    def wait(slt):
        pltpu.make_async_copy(hn_hbm.at[pl.ds(0, tc)], xbuf.at[slt], sem.at[slt]).wait()

    @pl.when(step == 0)
    def _():
        for t in range(ahead):
            def body(i, carry, t=t):
                tok = tbl_ref[t * tc + i] & 0xFFFF
                pltpu.make_async_copy(hn_hbm.at[pl.ds(tok, 1)], xbuf.at[t, pl.ds(i, 1)], sem.at[t]).start()
                return carry
            lax.fori_loop(0, tc, body, 0, unroll=8)

    wait(slot)
    xe = xbuf[slot].astype(BF16)
    a = _dot(xe, wg_ref[0])
    u = _dot(xe, wu_ref[0])
    hmid = (a * _sigmoid(a) * u).astype(BF16)
    nxt = jnp.minimum(step + ahead, last)
    into = (step + ahead) % FFN_SLOTS
    for i in range(tc):
        tok = tbl_ref[nxt * tc + i] & 0xFFFF
        pltpu.make_async_copy(hn_hbm.at[pl.ds(tok, 1)], xbuf.at[into, pl.ds(i, 1)], sem.at[into]).start()
    y = (_dot(hmid, wd_ref[0]) * gval_ref[0]).astype(BF16).astype(F32)
    lo = pltpu.bitcast(y[:, :HALF_D], jnp.uint32) >> 16
    hi = pltpu.bitcast(y[:, HALF_D:], jnp.uint32) & jnp.uint32(0xFFFF0000)
    out_ref[...] = hi | lo

    @pl.when(step == last)
    def _():
        for t in range(1, FFN_SLOTS):
            wait((last + t) % FFN_SLOTS)


def _expert_ffn(tbl_flat, hn, gval, wg, wu, wd, cap):
    tc = min(256, cap)
    nt = cap // tc
    grid_spec = pltpu.PrefetchScalarGridSpec(
        num_scalar_prefetch=1,
        grid=(N_EXPERTS, nt),
        in_specs=[
            pl.BlockSpec(memory_space=pl.ANY),
            pl.BlockSpec((1, tc, 1), lambda e, j, idx: (e * nt + j, 0, 0)),
            pl.BlockSpec((1, D_MODEL, EXPERT_FF), lambda e, j, idx: (e, 0, 0)),
            pl.BlockSpec((1, D_MODEL, EXPERT_FF), lambda e, j, idx: (e, 0, 0)),
            pl.BlockSpec((1, EXPERT_FF, D_MODEL), lambda e, j, idx: (e, 0, 0)),
        ],
        out_specs=pl.BlockSpec((tc, HALF_D), lambda e, j, idx: (e * nt + j, 0)),
        scratch_shapes=[pltpu.VMEM((FFN_SLOTS, tc, D_MODEL), F32), pltpu.SemaphoreType.DMA((FFN_SLOTS,))],
    )
    return pl.pallas_call(
        functools.partial(_ffn_kernel, tc=tc, nt=nt),
        grid_spec=grid_spec,
        out_shape=jax.ShapeDtypeStruct((N_EXPERTS * cap, HALF_D), jnp.uint32),
        compiler_params=_cparams(("arbitrary", "arbitrary")),
        name="expert_ffn",
    )(tbl_flat, hn, gval.reshape(N_EXPERTS * nt, tc, 1), wg, wu, wd)


def _combine_kernel(tbl_ref, p0_ref, km_ref, h_ref, cnt_ref, g_ref, ye_hbm, out_ref, stage, sem,
                    *, tt, cap, ntile):
    tile = pl.program_id(0)
    slot = tile % COMBINE_SLOTS

    def issue(tl, slt):
        for e in range(N_EXPERTS):
            p0 = p0_ref[e * (ntile + 1) + tl]
            cnt = p0_ref[e * (ntile + 1) + tl + 1] - p0

            def fetch(q, e=e, p0=p0):
                row = e * cap + p0 + q
                pltpu.make_async_copy(ye_hbm.at[pl.ds(row, 1)], stage.at[slt, pl.ds(tbl_ref[row] >> 16, 1)],
                                      sem.at[slt]).start()

            def four(j, carry, fetch=fetch):
                for u in range(4):
                    fetch(4 * j + u)
                return carry

            def one(q, carry, fetch=fetch, cnt=cnt):
                fetch((cnt & ~3) + q)
                return carry

            lax.fori_loop(0, cnt >> 2, four, 0)
            lax.fori_loop(0, cnt & 3, one, 0)

    ahead = COMBINE_SLOTS - 1

    @pl.when(tile == 0)
    def _():
        for tl in range(min(ahead, ntile)):
            issue(tl, tl)

    @pl.when(tile + ahead < ntile)
    def _():
        issue(tile + ahead, (tile + ahead) % COMBINE_SLOTS)

    def wait_rows(nrows):
        def body(q, carry):
            pltpu.make_async_copy(ye_hbm.at[pl.ds(0, nrows)], stage.at[slot, pl.ds(0, nrows)], sem.at[slot]).wait()
            return carry
        return body

    total = km_ref[2 * tile + 1]
    lax.fori_loop(0, total >> 3, wait_rows(8), 0)
    lax.fori_loop(0, total & 7, wait_rows(1), 0)

    cnt = cnt_ref[...]
    lo_parts, hi_parts = [], []
    for c in range(HALF_D // LANES):
        cs = slice(c * LANES, (c + 1) * LANES)
        hs = slice(HALF_D + c * LANES, HALF_D + (c + 1) * LANES)

        def add(k, acc, cs=cs):
            w = stage[slot, pl.ds(pl.multiple_of(k * tt, tt), tt), cs]
            pick = cnt > k
            lo = pltpu.bitcast(w << 16, F32)
            hi = pltpu.bitcast(w & jnp.uint32(0xFFFF0000), F32)
            return acc[0] + jnp.where(pick, lo, 0.0), acc[1] + jnp.where(pick, hi, 0.0)

        lo_acc, hi_acc = lax.fori_loop(0, km_ref[2 * tile], add, (h_ref[:, cs], h_ref[:, hs]))
        lo_parts.append(lo_acc)
        hi_parts.append(hi_acc)
    acc = jnp.concatenate(lo_parts + hi_parts, axis=1)
    ms = jnp.mean(acc * acc, axis=-1, keepdims=True)
    out_ref[...] = acc * lax.rsqrt(ms + NORM_EPS) * g_ref[...]


def _combine(tbl_flat, p0_flat, km_flat, h2d, cnt_tok, g, yexp, cap, tt):
    n = h2d.shape[0]
    ntile = n // tt
    grid_spec = pltpu.PrefetchScalarGridSpec(
        num_scalar_prefetch=3,
        grid=(ntile,),
        in_specs=[
            pl.BlockSpec((tt, D_MODEL), lambda i, a, b, c: (i, 0)),
            pl.BlockSpec((tt, 1), lambda i, a, b, c: (i, 0)),
            pl.BlockSpec((1, D_MODEL), lambda i, a, b, c: (0, 0)),
            pl.BlockSpec(memory_space=pl.ANY),
        ],
        out_specs=pl.BlockSpec((tt, D_MODEL), lambda i, a, b, c: (i, 0)),
        scratch_shapes=[pltpu.VMEM((COMBINE_SLOTS, N_EXPERTS * tt, HALF_D), jnp.uint32),
                        pltpu.SemaphoreType.DMA((COMBINE_SLOTS,))],
    )
    return pl.pallas_call(
        functools.partial(_combine_kernel, tt=tt, cap=cap, ntile=ntile),
        grid_spec=grid_spec,
        out_shape=jax.ShapeDtypeStruct((n, D_MODEL), F32),
        compiler_params=_cparams(("arbitrary",)),
        name="moe_combine",
    )(tbl_flat, p0_flat, km_flat, h2d, cnt_tok, g, yexp)


def _trunk(x, p, w_slab, mu_slab, rwp, moe_w, slopes):
    batch, seq, _ = x.shape
    n = batch * seq
    x2d = x.reshape(n, D_MODEL)
    slabs = _inproj(x2d, p["norm_mix_g"], w_slab)
    oatt = _attention(slabs, slopes, p["lambda_q1"], p["lambda_k1"], p["lambda_q2"], p["lambda_k2"],
                      p["subln_g"], batch, seq)
    rva, g, dirp = _rwprep(slabs, mu_slab, *rwp[:7], batch, seq)
    yf, yb = _rwscan(rva, dirp, rwp[7], batch, seq)
    h = _merge(x2d, oatt, yf, yb, g, slabs, p["gate_b"][0].reshape(2, 1, D_MODEL), rwp[8], rwp[9],
               moe_w["wba"], moe_w["wbr"], moe_w["wout"])

    cap = max(1, CAPACITY_FACTOR * n // N_EXPERTS)
    hn, aff = _router(h, p["norm_ffn_g"], moe_w["wr"])
    aff3 = aff.reshape(N_EXPERTS, n // LANES, LANES)
    assert n <= 1 << 16
    tt = LANES
    incl, tbl, gval, cnt = _select(aff3, cap, tt)
    ntile = n // tt
    ends = incl.reshape(N_EXPERTS, n)[:, tt - 1::tt]
    p0 = jnp.concatenate([jnp.zeros((N_EXPERTS, 1), I32), ends], axis=1)
    ct = cnt.reshape(ntile, tt)
    km = jnp.stack([jnp.max(ct, axis=1), jnp.sum(ct, axis=1)], axis=1)
    tbl_flat = tbl.reshape(-1)
    yexp = _expert_ffn(tbl_flat, hn, gval, moe_w["wg"], moe_w["wu"], moe_w["wd"], cap)
    y = _combine(tbl_flat, p0.reshape(-1), km.reshape(-1), h, cnt.reshape(n, 1),
                 p["norm_final_g"].reshape(1, D_MODEL), yexp, cap, tt)
    return y.reshape(batch, seq, D_MODEL)


def kernel(x_prompt, x_sample, norm_mix_g, w_in, shift_mu, lambda_q1, lambda_k1, lambda_q2, lambda_k2, subln_g, rw_w0, rw_w2, rw_a0, rw_a2, rw_g2, rw_k_k, rw_k_a, rw_r_k, lnx_g, lnx_b, gate_b, w_br_att, w_br_rw, w_out, norm_ffn_g, w_router, w_gate_e, w_up_e, w_down_e, norm_final_g):
    p = dict(norm_mix_g=norm_mix_g, lambda_q1=lambda_q1, lambda_k1=lambda_k1, lambda_q2=lambda_q2,
             lambda_k2=lambda_k2, subln_g=subln_g, rw_w0=rw_w0, rw_w2=rw_w2, rw_a0=rw_a0, rw_a2=rw_a2,
             rw_g2=rw_g2, rw_k_k=rw_k_k, rw_k_a=rw_k_a, rw_r_k=rw_r_k, lnx_g=lnx_g, lnx_b=lnx_b,
             gate_b=gate_b, norm_ffn_g=norm_ffn_g, norm_final_g=norm_final_g)
    w_slab, mu_slab = _prep_in_weights(w_in[0], shift_mu[0])
    rwp = _prep_rw_params(p)
    moe_w = dict(
        wba=w_br_att[0].astype(BF16), wbr=w_br_rw[0].astype(BF16), wout=w_out[0].astype(BF16),
        wr=jnp.pad(w_router[0], ((0, 0), (0, LANES - N_EXPERTS))),
        wg=w_gate_e[0].astype(BF16), wu=w_up_e[0].astype(BF16), wd=w_down_e[0].astype(BF16))
    slopes = jnp.asarray([2.0 ** (-8.0 * (i + 1) / ATT_HEADS) for i in range(ATT_HEADS)], F32)
    return (_trunk(x_prompt, p, w_slab, mu_slab, rwp, moe_w, slopes),
            _trunk(x_sample, p, w_slab, mu_slab, rwp, moe_w, slopes))
```

```python
import functools
import math

import jax
import jax.numpy as jnp
from jax import lax
from jax.experimental import pallas as pl
from jax.experimental.pallas import tpu as pltpu

F32 = jnp.float32
BF16 = jnp.bfloat16
I32 = jnp.int32

D_MODEL = 2048
ATT_HEADS = 8
ATT_HEAD_DIM = 64
ATT_WIDTH = ATT_HEADS * 2 * ATT_HEAD_DIM
RW_HEAD = 64
RW_WIDTH = 1024
DECAY_LORA = 96
ICLR_LORA = 96
GATE_LORA = 256
SHIFT_WIDTH = 3 * RW_WIDTH + DECAY_LORA + ICLR_LORA + GATE_LORA
N_EXPERTS = 16
CAPACITY_FACTOR = 2
EXPERT_FF = 1024
NORM_EPS = 1e-6
SUBLN_EPS = 1e-5
LNX_EPS = 64e-5
LAM_INIT = 0.8 - 0.6 * math.exp(-0.3 * 0)

LANES = 128
VMEM_LIMIT = 56 * 1024 * 1024

SL_GATE_ATT, SL_GATE_RW = 0, 16
SL_ATT_Q, SL_ATT_K, SL_ATT_V = 32, 40, 48
SL_RW = 56
N_RW_SLABS = 28
N_SLABS = 84
CHUNK = 64
COMBINE_SLOTS = 2
FFN_SLOTS = 3
HALF_D = D_MODEL // 2


def _cparams(sem):
    return pltpu.CompilerParams(dimension_semantics=sem, vmem_limit_bytes=VMEM_LIMIT)


def _sigmoid(x):
    return 1.0 / (1.0 + jnp.exp(-x))


def _split3(x):
    hi = x.astype(BF16)
    r1 = x - hi.astype(F32)
    mid = r1.astype(BF16)
    lo = (r1 - mid.astype(F32)).astype(BF16)
    return hi, mid, lo


def _dot(a, b):
    return jnp.dot(a, b, preferred_element_type=F32)


def _dot_nt(a, b):
    return lax.dot_general(a, b, (((1,), (1,)), ((), ())), preferred_element_type=F32)


def _dot_tn(a, b):
    return lax.dot_general(a, b, (((0,), (0,)), ((), ())), preferred_element_type=F32)


def _dot_f32(a_bf16_exact, x):
    hi, mid, lo = _split3(x)
    return _dot(a_bf16_exact, hi) + _dot(a_bf16_exact, mid) + _dot(a_bf16_exact, lo)


def _rearrange_in_cols(a):
    att = a[..., :3 * ATT_WIDTH]
    zr = a[..., 3 * ATT_WIDTH:3 * ATT_WIDTH + SHIFT_WIDTH]
    gates = a[..., 3 * ATT_WIDTH + SHIFT_WIDTH:]
    o3 = 3 * RW_WIDTH
    o4 = o3 + DECAY_LORA
    o5 = o4 + ICLR_LORA
    pad = [(0, 0)] * (a.ndim - 1)
    lw = jnp.pad(zr[..., o3:o4], pad + [(0, LANES - DECAY_LORA)])
    la = jnp.pad(zr[..., o4:o5], pad + [(0, LANES - ICLR_LORA)])
    return jnp.concatenate([gates, att, zr[..., :o3], lw, la, zr[..., o5:]], axis=-1)


def _prep_in_weights(w_in, shift_mu):
    w_slab = _rearrange_in_cols(w_in).astype(BF16)
    mu_full = jnp.pad(shift_mu, ((0, 0), (3 * ATT_WIDTH, 2 * D_MODEL)))
    mu_slab = _rearrange_in_cols(mu_full)[:, SL_RW * LANES:]
    return w_slab, mu_slab.reshape(2, 1, N_RW_SLABS * LANES)


def _inproj_kernel(x_ref, g_ref, w_ref, o_ref, xn_ref, *, n_out_slabs):
    @pl.when(pl.program_id(1) == 0)
    def _():
        x = x_ref[...]
        ms = jnp.mean(x * x, axis=-1, keepdims=True)
        xn_ref[...] = (x * lax.rsqrt(ms + NORM_EPS) * g_ref[...]).astype(BF16)

    acc = _dot(xn_ref[...], w_ref[...])
    for c in range(n_out_slabs):
        o_ref[c] = acc[:, c * LANES:(c + 1) * LANES].astype(BF16)


def _inproj(x2d, g, w_slab):
    n = x2d.shape[0]
    tm = min(1024, n)
    tn = 1536
    n_out_slabs = tn // LANES
    grid = (n // tm, (N_SLABS * LANES) // tn)
    return pl.pallas_call(
        functools.partial(_inproj_kernel, n_out_slabs=n_out_slabs),
        grid=grid,
        in_specs=[
            pl.BlockSpec((tm, D_MODEL), lambda i, j: (i, 0)),
            pl.BlockSpec((1, D_MODEL), lambda i, j: (0, 0)),
            pl.BlockSpec((D_MODEL, tn), lambda i, j: (0, j)),
        ],
        out_specs=pl.BlockSpec((n_out_slabs, tm, LANES), lambda i, j: (j, i, 0)),
        out_shape=jax.ShapeDtypeStruct((N_SLABS, n, LANES), BF16),
        scratch_shapes=[pltpu.VMEM((tm, D_MODEL), BF16)],
        compiler_params=_cparams(("parallel", "arbitrary")),
        name="inproj",
    )(x2d, g, w_slab)


def _attn_kernel(slopes_ref, lq1_ref, lk1_ref, lq2_ref, lk2_ref, subg_ref, q_ref, k_ref, v_ref, o_ref,
                 kt1_ref, kt2_ref, vaug_ref, *, seq, tq):
    h = pl.program_id(1)
    qi = pl.program_id(2)
    slope = slopes_ref[h]
    q0 = pl.multiple_of(qi * tq, tq)
    view = pl.ds(q0, seq)

    @pl.when(qi == 0)
    def _():
        kt = k_ref[0].astype(F32).T
        row = lax.broadcasted_iota(I32, kt.shape, 0)
        k1 = jnp.where(row < ATT_HEAD_DIM, kt, 0.0).astype(BF16)
        k2 = jnp.where(row >= ATT_HEAD_DIM, kt, 0.0).astype(BF16)
        lane = lax.broadcasted_iota(I32, (seq, LANES), 1)
        va = jnp.concatenate([v_ref[0], jnp.where(lane == 0, 1.0, 0.0).astype(BF16)], axis=1)
        for half in (slice(0, seq), slice(seq, 2 * seq)):
            kt1_ref[:, half] = k1
            kt2_ref[:, half] = k2
            vaug_ref[half, :] = va

    col = lax.broadcasted_iota(I32, (16, seq), 1)
    r16 = lax.broadcasted_iota(I32, (16, seq), 0)
    wrapped = col + q0 >= seq
    jp = jnp.where(wrapped, col - seq, col)
    sigma = jnp.where(col < tq, 0.0, jnp.where(wrapped, -1.0, 1.0)).astype(F32)
    jh = (jp >> 8).astype(F32)
    jl = (jp & 255).astype(F32)
    feat = jnp.where(r16 <= 1, sigma,
                     jnp.where(r16 == 2, -sigma * (slope * 256.0) * jh,
                               jnp.where(r16 == 3, -sigma * slope * jl, 0.0))).astype(BF16)
    kt1_ref[ATT_HEAD_DIM:ATT_HEAD_DIM + 16, view] = feat
    kt2_ref[0:16, view] = feat

    q = q_ref[0].astype(F32) * (ATT_HEAD_DIM ** -0.5)
    lane = lax.broadcasted_iota(I32, (tq, LANES), 1)
    ip = lax.broadcasted_iota(I32, (tq, LANES), 0)
    ih = slope * (ip & ~255).astype(F32)
    il = slope * (ip & 255).astype(F32)

    def query_side(fl):
        return jnp.where(fl == 0, ih, jnp.where(fl == 1, il, jnp.where(fl <= 3, 1.0, 0.0)))

    lhs1 = jnp.where(lane < ATT_HEAD_DIM, q, query_side(lane - ATT_HEAD_DIM)).astype(BF16)
    lhs2 = jnp.where(lane >= ATT_HEAD_DIM, q, query_side(lane)).astype(BF16)

    di = lax.broadcasted_iota(I32, (tq, tq), 0)
    dj = lax.broadcasted_iota(I32, (tq, tq), 1)
    diag_bias = -slope * jnp.abs(di - dj).astype(F32)

    def weights(lhs, kt_ref):
        s = _dot(lhs, kt_ref[:, view])
        s = jnp.concatenate([s[:, :tq] + diag_bias, s[:, tq:]], axis=1)
        m = jnp.max(s, axis=-1, keepdims=True)
        return jnp.exp(s - m).astype(BF16)

    e = jnp.concatenate([weights(lhs1, kt1_ref), weights(lhs2, kt2_ref)], axis=0)
    oa = _dot(e, vaug_ref[view, :])
    o1 = oa[:tq, :LANES] / oa[:tq, LANES:LANES + 1]
    o2 = oa[tq:, :LANES] / oa[tq:, LANES:LANES + 1]
    lam = (jnp.exp(jnp.sum(lq1_ref[...] * lk1_ref[...], keepdims=True))
           - jnp.exp(jnp.sum(lq2_ref[...] * lk2_ref[...], keepdims=True)) + LAM_INIT)
    out = o1 - lam * o2
    ms = jnp.mean(out * out, axis=-1, keepdims=True)
    y = out * lax.rsqrt(ms + SUBLN_EPS) * subg_ref[...]
    o_ref[0] = (y * (1.0 - LAM_INIT)).astype(BF16)


def _attention(slabs, slopes, lq1, lk1, lq2, lk2, subg, batch, seq):
    n = batch * seq
    tq = 512
    nq = seq // tq
    vec = lambda: pl.BlockSpec((1, ATT_HEAD_DIM), lambda b, h, i: (0, 0))
    return pl.pallas_call(
        functools.partial(_attn_kernel, seq=seq, tq=tq),
        grid=(batch, ATT_HEADS, nq),
        in_specs=[
            pl.BlockSpec(memory_space=pltpu.SMEM),
            vec(), vec(), vec(), vec(),
            pl.BlockSpec((1, LANES), lambda b, h, i: (0, 0)),
            pl.BlockSpec((1, tq, LANES), lambda b, h, i: (SL_ATT_Q + h, b * nq + i, 0)),
            pl.BlockSpec((1, seq, LANES), lambda b, h, i: (SL_ATT_K + h, b, 0)),
            pl.BlockSpec((1, seq, LANES), lambda b, h, i: (SL_ATT_V + h, b, 0)),
        ],
        out_specs=pl.BlockSpec((1, tq, LANES), lambda b, h, i: (h, b * nq + i, 0)),
        out_shape=jax.ShapeDtypeStruct((ATT_HEADS, n, LANES), BF16),
        scratch_shapes=[
            pltpu.VMEM((LANES, 2 * seq), BF16),
            pltpu.VMEM((LANES, 2 * seq), BF16),
            pltpu.VMEM((2 * seq, 2 * LANES), BF16),
        ],
        compiler_params=_cparams(("parallel", "parallel", "arbitrary")),
        name="diff_attn",
    )(slopes, lq1, lk1, lq2, lk2, subg, slabs, slabs, slabs)


def _head_segsum(x, bd):
    hi, mid, lo = _split3(x)
    return _dot(hi, bd) + _dot(mid, bd) + _dot(lo, bd)


def _block_ones():
    ri = lax.broadcasted_iota(I32, (LANES, LANES), 0)
    ci = lax.broadcasted_iota(I32, (LANES, LANES), 1)
    return jnp.where((ri >> 6) == (ci >> 6), 1.0, 0.0).astype(BF16)


def _rwprep_kernel(main_ref, prev_ref, next_ref, mu_ref, w0_ref, a0_ref, kk_ref, ka_ref, w2_ref, a2_ref,
                   g2_ref, rva_ref, g_ref, dirp_ref, *, nt):
    i = pl.program_id(1)
    t = main_ref.shape[1]
    hb = prev_ref.shape[1]
    wide = lambda ref: jnp.concatenate([ref[c] for c in range(N_RW_SLABS)], axis=1)
    zb = wide(main_ref)
    prev = jnp.where(i > 0, wide(prev_ref), jnp.zeros((), BF16))
    nxt = jnp.where(i < nt - 1, wide(next_ref), jnp.zeros((), BF16))
    halo = jnp.concatenate([prev, zb, nxt], axis=0)
    ri = lax.broadcasted_iota(I32, (t, t + 2 * hb), 0)
    ci = lax.broadcasted_iota(I32, (t, t + 2 * hb), 1)
    zp = _dot(jnp.where(ci == ri + hb - 1, 1.0, 0.0).astype(BF16), halo)
    zn = _dot(jnp.where(ci == ri + hb + 1, 1.0, 0.0).astype(BF16), halo)
    z = zb.astype(F32)
    z = z + mu_ref[0] * (zp - z) + mu_ref[1] * (zn - z)
    slab = lambda s: z[:, s * LANES:(s + 1) * LANES]

    xw = jnp.tanh(slab(24)).astype(BF16)
    xa = slab(25).astype(BF16)
    xg = _sigmoid(z[:, 26 * LANES:28 * LANES]).astype(BF16)
    g_full = _dot(xg, g2_ref[...])
    lw = [_dot(xw, w2_ref[d]) for d in range(2)]
    la = [_dot(xa, a2_ref[d]) for d in range(2)]
    bd = _block_ones()
    for c in range(8):
        cs = slice(c * LANES, (c + 1) * LANES)
        kc = slab(8 + c)
        kk = kc * kk_ref[c]
        nrm = jnp.sqrt(_head_segsum(kk * kk, bd))
        kk = kk / jnp.maximum(nrm, 1e-12)
        rva_ref[0, c] = slab(c)
        rva_ref[1, c] = slab(16 + c)
        rva_ref[2, c] = -kk
        g_ref[c] = g_full[:, cs]
        for d in range(2):
            dirp_ref[d, 2, c] = -math.exp(-0.5) * _sigmoid(w0_ref[d, c] + lw[d][:, cs])
            asig = _sigmoid(a0_ref[d, c] + la[d][:, cs])
            dirp_ref[d, 0, c] = kc * (1.0 + (asig - 1.0) * ka_ref[c])
            dirp_ref[d, 1, c] = kk * asig


def _prep_rw_params(p):
    vec = lambda a: a.reshape(a.shape[:-1] + (8, 1, LANES))
    pad_rows = lambda a: jnp.pad(a, ((0, 0), (0, LANES - a.shape[1]), (0, 0))).astype(BF16)
    return (vec(p["rw_w0"][0]), vec(p["rw_a0"][0]), vec(p["rw_k_k"][0]), vec(p["rw_k_a"][0]),
            pad_rows(p["rw_w2"][0]), pad_rows(p["rw_a2"][0]), p["rw_g2"][0].astype(BF16),
            vec(p["rw_r_k"][0].reshape(RW_WIDTH)), vec(p["lnx_g"][0]), vec(p["lnx_b"][0]))


def _rwprep(slabs, mu_slab, w0, a0, k_k, k_a, w2, a2, g2, batch, seq):
    n = batch * seq
    t = 256
    nt = seq // t
    hb = 16
    full = lambda shape: pl.BlockSpec(shape, lambda b, i: (0,) * len(shape))
    rows = lambda b, i: b * nt + i
    out_specs = [pl.BlockSpec((3, 8, t, LANES), lambda b, i: (0, 0, rows(b, i), 0)),
                 pl.BlockSpec((8, t, LANES), lambda b, i: (0, rows(b, i), 0)),
                 pl.BlockSpec((2, 3, 8, t, LANES), lambda b, i: (0, 0, 0, rows(b, i), 0))]
    out_shape = [jax.ShapeDtypeStruct((3, 8, n, LANES), F32), jax.ShapeDtypeStruct((8, n, LANES), F32),
                 jax.ShapeDtypeStruct((2, 3, 8, n, LANES), F32)]
    rw_blk = SL_RW // N_RW_SLABS
    return pl.pallas_call(
        functools.partial(_rwprep_kernel, nt=nt),
        grid=(batch, nt),
        in_specs=[
            pl.BlockSpec((N_RW_SLABS, t, LANES), lambda b, i: (rw_blk, rows(b, i), 0)),
            pl.BlockSpec((N_RW_SLABS, hb, LANES),
                         lambda b, i: (rw_blk, jnp.maximum((b * seq + i * t) // hb - 1, 0), 0)),
            pl.BlockSpec((N_RW_SLABS, hb, LANES),
                         lambda b, i: (rw_blk, jnp.minimum((b * seq + (i + 1) * t) // hb, n // hb - 1), 0)),
            full((2, 1, N_RW_SLABS * LANES)),
            full((2, 8, 1, LANES)), full((2, 8, 1, LANES)), full((8, 1, LANES)), full((8, 1, LANES)),
            full((2, LANES, RW_WIDTH)), full((2, LANES, RW_WIDTH)), full((GATE_LORA, RW_WIDTH)),
        ],
        out_specs=out_specs,
        out_shape=out_shape,
        compiler_params=_cparams(("parallel", "parallel")),
        name="rwkv_prep",
    )(slabs, slabs, slabs, mu_slab, w0, a0, k_k, k_a, w2, a2, g2)


def _rwscan_kernel(rvaf_ref, dirf_ref, rvab_ref, dirb_ref, rk_ref, yf_ref, yb_ref, st_ref, *, nc, group):
    @pl.when(pl.program_id(2) == 0)
    def _():
        st_ref[...] = jnp.zeros_like(st_ref)

    lane = lax.broadcasted_iota(I32, (CHUNK, LANES), 1)
    head0 = lane < RW_HEAD
    ri = lax.broadcasted_iota(I32, (LANES, LANES), 0)
    ci = lax.broadcasted_iota(I32, (LANES, LANES), 1)
    same = (ri >> 6) == (ci >> 6)
    tt = ri & (CHUNK - 1)
    ss = ci & (CHUNK - 1)
    eye = jnp.where(ri == ci, 1.0, 0.0).astype(F32)
    tr = lax.broadcasted_iota(I32, (CHUNK, CHUNK), 0)
    tc = lax.broadcasted_iota(I32, (CHUNK, CHUNK), 1)
    bd = _block_ones()
    rk = rk_ref[0]

    def stack(x):
        return jnp.concatenate([jnp.where(head0, x, 0.0), jnp.where(head0, 0.0, x)], axis=0)

    dirs = ((rvaf_ref, dirf_ref, yf_ref), (rvab_ref, dirb_ref, yb_ref))
    strict = (same & (ss < tt), same & (ss > tt))
    incl = (same & (ss <= tt), same & (ss >= tt))
    tri = (jnp.where(tc <= tr, 1.0, 0.0).astype(BF16), jnp.where(tc >= tr, 1.0, 0.0).astype(BF16))
    last = (CHUNK - 1, 0)
    all_insts = [(d, k if d == 0 else nc - 1 - k) for k in range(nc) for d in range(2)]

    def load(which, j, insts):
        lead = (j, 0) if which == 0 else (0, j, 0)
        return [dirs[d][which][lead + (slice(ch * CHUNK, (ch + 1) * CHUNK), slice(None))] for d, ch in insts]

    def state_free_part(insts, out):
        every = range(len(insts))
        r, v, na = (load(0, j, insts) for j in range(3))
        kd, b, ld = (load(1, j, insts) for j in range(3))
        tri3 = [jnp.concatenate([t, t, t], axis=1) for t in tri]
        c = [_dot(tri3[insts[i][0]], jnp.concatenate(_split3(ld[i]), axis=0)) for i in every]
        total = [c[i][last[insts[i][0]]:last[insts[i][0]] + 1] for i in every]
        yield
        e_nc = [jnp.exp(-c[i]) for i in every]
        e_tc = [jnp.exp(total[i] - c[i]) for i in every]
        a_t = [stack(na[i] * jnp.exp(c[i] - ld[i])).astype(BF16) for i in every]
        r_t = [stack(r[i] * jnp.exp(c[i])).astype(BF16) for i in every]
        v_s = [stack(v[i]).astype(BF16) for i in every]
        rhs = [jnp.concatenate([stack(b[i] * e_nc[i]), stack(kd[i] * e_nc[i])], axis=0).astype(BF16) for i in every]
        bk = [jnp.concatenate([stack(b[i] * e_tc[i]), stack(kd[i] * e_tc[i])], axis=0).astype(BF16) for i in every]
        yield
        p = [_dot_nt(jnp.concatenate([a_t[i], r_t[i]], axis=0), rhs[i]) for i in every]
        yield
        n_ab = [jnp.where(strict[insts[i][0]], p[i][:LANES, :LANES], 0.0) for i in every]
        a_ak = [jnp.where(strict[insts[i][0]], p[i][:LANES, LANES:], 0.0).astype(BF16) for i in every]
        p_rb = [jnp.where(incl[insts[i][0]], p[i][LANES:, :LANES], 0.0).astype(BF16) for i in every]
        p_rk = [jnp.where(incl[insts[i][0]], p[i][LANES:, LANES:], 0.0).astype(BF16) for i in every]
        yield
        x = [eye + n_ab[i] for i in every]
        nk = [n_ab[i].astype(BF16) for i in every]
        nk = [_dot(nk[i], nk[i]) for i in every]
        for _ in range(4):
            yield
            both = [_dot(nk[i].astype(BF16), jnp.concatenate([nk[i], x[i]], axis=1).astype(BF16)) for i in every]
            nk = [both[i][:, :LANES] for i in every]
            x = [x[i] + both[i][:, LANES:] for i in every]
        yield
        x = [x[i] + _dot(nk[i].astype(BF16), x[i].astype(BF16)) for i in every]
        w = [_dot(a_ak[i], v_s[i]) for i in every]
        yield
        au = [_dot(x[i].astype(BF16), jnp.concatenate([a_t[i], w[i].astype(BF16)], axis=1)) for i in every]
        yield
        rpp = [jnp.concatenate([r_t[i], p_rb[i], p_rk[i]], axis=1) for i in every]
        bonus = [_head_segsum(r[i] * kd[i] * rk, bd) * v[i] for i in every]
        u0_t = [au[i][:, LANES:].T for i in every]
        v_t = [v_s[i].astype(F32).T.astype(BF16) for i in every]
        out.extend(dict(a_hat=au[i][:, :LANES].astype(BF16), u0_t=u0_t[i], v_t=v_t[i], rpp=rpp[i], bk=bk[i],
                        decay=jnp.exp(total[i]), bonus=bonus[i]) for i in every)

    groups = [all_insts[g:g + group] for g in range(0, len(all_insts), group)]
    pre = [[] for _ in groups]
    parts = [state_free_part(gr, pre[gi]) for gi, gr in enumerate(groups)]
    for _ in parts[0]:
        pass
    st = [st_ref[0], st_ref[1]]
    for gi, gr in enumerate(groups):
        upcoming = parts[gi + 1] if gi + 1 < len(groups) else iter(())
        for (d, ch), f in zip(gr, pre[gi]):
            st_b = st[d].astype(BF16)
            u_t = (_dot_nt(st_b, f["a_hat"]) + f["u0_t"]).astype(BF16)
            uv_t = jnp.concatenate([u_t, f["v_t"]], axis=1)
            y_t = _dot_nt(jnp.concatenate([st_b, uv_t], axis=1), f["rpp"])
            st[d] = st[d] * f["decay"] + _dot(uv_t, f["bk"])
            y = y_t.T
            dirs[d][2][0, ch * CHUNK:(ch + 1) * CHUNK, :] = y[:CHUNK] + y[CHUNK:] + f["bonus"]
            next(upcoming, None)
            next(upcoming, None)
        for _ in upcoming:
            pass
    st_ref[0] = st[0]
    st_ref[1] = st[1]


def _rwscan(rva, dirp, r_k, batch, seq):
    n = batch * seq
    nc = min(16, seq // CHUNK)
    tcs = CHUNK * nc
    nt = seq // tcs
    fwd = lambda bi, c, t: bi * nt + t
    bwd = lambda bi, c, t: bi * nt + nt - 1 - t
    s3 = lambda rows: pl.BlockSpec((1, tcs, LANES), lambda bi, c, t: (c, rows(bi, c, t), 0))
    s_rva = lambda rows: pl.BlockSpec((3, 1, tcs, LANES), lambda bi, c, t: (0, c, rows(bi, c, t), 0))
    s_dir = lambda d, rows: pl.BlockSpec((1, 3, 1, tcs, LANES), lambda bi, c, t: (d, 0, c, rows(bi, c, t), 0))
    out = jax.ShapeDtypeStruct((8, n, LANES), F32)
    return pl.pallas_call(
        functools.partial(_rwscan_kernel, nc=nc, group=2 * nc),
        grid=(batch, 8, nt),
        in_specs=[s_rva(fwd), s_dir(0, fwd), s_rva(bwd), s_dir(1, bwd),
                  pl.BlockSpec((1, 1, LANES), lambda bi, c, t: (c, 0, 0))],
        out_specs=[s3(fwd), s3(bwd)],
        out_shape=[out, out],
        scratch_shapes=[pltpu.VMEM((2, LANES, LANES), F32)],
        compiler_params=_cparams(("parallel", "parallel", "arbitrary")),
        name="rwkv_scan",
    )(rva, dirp, rva, dirp, r_k)


def _merge_kernel(x_ref, oatt_ref, yf_ref, yb_ref, g_ref, gates_ref, gb_ref, lng_ref, lnb_ref,
                  wba_ref, wbr_ref, wout_ref, h_ref):
    bd = _block_ones()
    orw = []
    for c in range(8):
        y = yf_ref[c] + yb_ref[c]
        mu = _head_segsum(y, bd) * (1.0 / RW_HEAD)
        yc = y - mu
        var = _head_segsum(yc * yc, bd) * (1.0 / RW_HEAD)
        yn = yc * lax.rsqrt(var + LNX_EPS) * lng_ref[c] + lnb_ref[c]
        orw.append((yn * g_ref[c]).astype(BF16))
    orw = jnp.concatenate(orw, axis=1)
    oatt = jnp.concatenate([oatt_ref[c] for c in range(8)], axis=1)
    ga = jnp.concatenate([gates_ref[c] for c in range(16)], axis=1).astype(F32) + gb_ref[0]
    gr = jnp.concatenate([gates_ref[16 + c] for c in range(16)], axis=1).astype(F32) + gb_ref[1]
    merged = _sigmoid(ga) * _dot(oatt, wba_ref[...]) + _sigmoid(gr) * _dot(orw, wbr_ref[...])
    h_ref[...] = x_ref[...] + _dot(merged.astype(BF16), wout_ref[...])


def _merge(x2d, oatt, yf, yb, g, slabs, gate_b, lng, lnb, wba, wbr, wout):
    n = x2d.shape[0]
    tm = min(256, n)
    const = lambda shape: pl.BlockSpec(shape, lambda i: (0,) * len(shape), pipeline_mode=pl.Buffered(1))
    s8 = pl.BlockSpec((8, tm, LANES), lambda i: (0, i, 0))
    return pl.pallas_call(
        _merge_kernel,
        grid=(n // tm,),
        in_specs=[
            pl.BlockSpec((tm, D_MODEL), lambda i: (i, 0)),
            s8, s8, s8, s8,
            pl.BlockSpec((32, tm, LANES), lambda i: (0, i, 0)),
            const((2, 1, D_MODEL)), const((8, 1, LANES)), const((8, 1, LANES)),
            const((ATT_WIDTH, D_MODEL)), const((RW_WIDTH, D_MODEL)), const((D_MODEL, D_MODEL)),
        ],
        out_specs=pl.BlockSpec((tm, D_MODEL), lambda i: (i, 0)),
        out_shape=jax.ShapeDtypeStruct((n, D_MODEL), F32),
        compiler_params=_cparams(("parallel",)),
        name="merge_outproj",
    )(x2d, oatt, yf, yb, g, slabs, gate_b, lng, lnb, wba, wbr, wout)


def _router_kernel(h_ref, g_ref, wr_ref, hn_ref, aff_ref):
    x = h_ref[...]
    ms = jnp.mean(x * x, axis=-1, keepdims=True)
    hn = x * lax.rsqrt(ms + NORM_EPS) * g_ref[...]
    hn_ref[...] = hn
    xh, xm, _ = _split3(hn)
    wh, wm, _ = _split3(wr_ref[...])
    logits = _dot(xh, wh) + _dot(xh, wm) + _dot(xm, wh)
    lt = logits.T[:N_EXPERTS]
    m = jnp.max(lt, axis=0, keepdims=True)
    e = jnp.exp(lt - m)
    aff_ref[...] = e / jnp.sum(e, axis=0, keepdims=True)


def _router(h2d, g, wr_pad):
    n = h2d.shape[0]
    tm = min(256, n)
    return pl.pallas_call(
        _router_kernel,
        grid=(n // tm,),
        in_specs=[
            pl.BlockSpec((tm, D_MODEL), lambda i: (i, 0)),
            pl.BlockSpec((1, D_MODEL), lambda i: (0, 0)),
            pl.BlockSpec((D_MODEL, LANES), lambda i: (0, 0)),
        ],
        out_specs=[
            pl.BlockSpec((tm, D_MODEL), lambda i: (i, 0)),
            pl.BlockSpec((N_EXPERTS, tm), lambda i: (0, i)),
        ],
        out_shape=[
            jax.ShapeDtypeStruct((n, D_MODEL), F32),
            jax.ShapeDtypeStruct((N_EXPERTS, n), F32),
        ],
        compiler_params=_cparams(("parallel",)),
        name="router",
    )(h2d, g, wr_pad)


def _select_kernel(aff_ref, incl_ref, tbl_ref, gval_ref, cnt_ref, slot_ref, *, cap, tt):
    bits = pltpu.bitcast(aff_ref[...], I32)
    nrow = bits.shape[1]

    def count(mask):
        c = jnp.sum(jnp.where(mask, 1, 0), axis=2, keepdims=True)
        return jnp.sum(c, axis=1, keepdims=True)

    def body(_, carry):
        lo, hi = carry
        mid = lo + ((hi - lo) >> 1)
        ok = count(bits >= mid) >= cap
        return jnp.where(ok, mid, lo), jnp.where(ok, hi, mid)

    lo0 = jnp.zeros((N_EXPERTS, 1, 1), I32)
    hi0 = jnp.full((N_EXPERTS, 1, 1), 0x7F800000, I32)
    thr, _ = lax.fori_loop(0, 31, body, (lo0, hi0))
    gt = bits > thr
    eq = bits == thr
    need = cap - count(gt)

    ri = lax.broadcasted_iota(I32, (LANES, LANES), 0)
    ci = lax.broadcasted_iota(I32, (LANES, LANES), 1)
    upper = jnp.where(ri <= ci, 1.0, 0.0).astype(BF16)
    rr = lax.broadcasted_iota(I32, (nrow, nrow), 0)
    rc = lax.broadcasted_iota(I32, (nrow, nrow), 1)
    lower_strict = jnp.where(rc < rr, 1.0, 0.0).astype(BF16)

    def incl_prefix(mask):
        x = jnp.where(mask, 1.0, 0.0).astype(BF16)
        incl = _dot(x.reshape(N_EXPERTS * nrow, LANES), upper).reshape(N_EXPERTS, nrow, LANES)
        tot = jnp.broadcast_to(incl[:, :, LANES - 1:LANES], incl.shape).astype(BF16)
        before = jnp.stack([_dot(lower_strict, tot[e]) for e in range(N_EXPERTS)], axis=0)
        return incl + before

    sel = gt | (eq & (incl_prefix(eq) - 1.0 < need.astype(F32)))
    incl_ref[...] = incl_prefix(sel).astype(I32)
    run = jnp.zeros((nrow, LANES), F32)
    for e in range(N_EXPERTS):
        slot_ref[e] = run
        run = run + jnp.where(sel[e], 1.0, 0.0)
    cnt_ref[...] = run.astype(I32)

    pf = lax.broadcasted_iota(I32, (1, cap), 1).astype(F32)
    jrow = lax.broadcasted_iota(I32, (nrow, cap), 0).astype(F32)
    lrow = lax.broadcasted_iota(I32, (LANES, cap), 0).astype(F32)

    def compact(e, carry):
        g = incl_ref[e].astype(F32)
        jsel = jnp.sum(jnp.where(g[:, LANES - 1:LANES] <= pf, 1.0, 0.0), axis=0, keepdims=True)
        onehot = jnp.where(jrow == jsel, 1.0, 0.0).astype(BF16)
        ghi = jnp.floor(g * (1.0 / 256.0))
        glo = g - 256.0 * ghi
        grow = 256.0 * _dot_tn(ghi.astype(BF16), onehot) + _dot_tn(glo.astype(BF16), onehot)
        lstar = jnp.sum(jnp.where(grow <= pf, 1.0, 0.0), axis=0, keepdims=True)
        lsel = lrow == lstar
        ah, am, al = _split3(aff_ref[e])
        arow = _dot_tn(ah, onehot) + _dot_tn(am, onehot) + _dot_tn(al, onehot)
        gval_ref[pl.ds(e, 1), :] = jnp.sum(jnp.where(lsel, arow, 0.0), axis=0, keepdims=True)
        krow = _dot_tn(slot_ref[e].astype(BF16), onehot)
        kk = jnp.sum(jnp.where(lsel, krow, 0.0), axis=0, keepdims=True)
        tok = (jsel * float(LANES) + lstar).astype(I32)
        dest = kk.astype(I32) * tt + (tok & (tt - 1))
        tbl_ref[pl.ds(e, 1), :] = tok | (dest << 16)
        return carry

    lax.fori_loop(0, N_EXPERTS, compact, 0)


def _select(aff3, cap, tt):
    nrow = aff3.shape[1]
    return pl.pallas_call(
        functools.partial(_select_kernel, cap=cap, tt=tt),
        out_shape=[
            jax.ShapeDtypeStruct(aff3.shape, I32),
            jax.ShapeDtypeStruct((N_EXPERTS, cap), I32),
            jax.ShapeDtypeStruct((N_EXPERTS, cap), F32),
            jax.ShapeDtypeStruct((nrow, LANES), I32),
        ],
        scratch_shapes=[pltpu.VMEM((N_EXPERTS, nrow, LANES), F32)],
        compiler_params=pltpu.CompilerParams(vmem_limit_bytes=VMEM_LIMIT),
        name="expert_select",
    )(aff3)


def _ffn_kernel(tbl_ref, hn_hbm, gval_ref, wg_ref, wu_ref, wd_ref, out_ref, xbuf, sem, *, tc, nt):
    step = pl.program_id(0) * nt + pl.program_id(1)
    last = N_EXPERTS * nt - 1
    slot = step % FFN_SLOTS
    ahead = FFN_SLOTS - 1

    def wait(slt):
        pltpu.make_async_copy(hn_hbm.at[pl.ds(0, tc)], xbuf.at[slt], sem.at[slt]).wait()

    @pl.when(step == 0)
    def _():
        for t in range(ahead):
            def body(i, carry, t=t):
                tok = tbl_ref[t * tc + i] & 0xFFFF
                pltpu.make_async_copy(hn_hbm.at[pl.ds(tok, 1)], xbuf.at[t, pl.ds(i, 1)], sem.at[t]).start()
                return carry
            lax.fori_loop(0, tc, body, 0, unroll=8)

    wait(slot)
    xe = xbuf[slot].astype(BF16)
    a = _dot(xe, wg_ref[0])
    u = _dot(xe, wu_ref[0])
    hmid = (a * _sigmoid(a) * u).astype(BF16)
    nxt = jnp.minimum(step + ahead, last)
    into = (step + ahead) % FFN_SLOTS
    for i in range(tc):
        tok = tbl_ref[nxt * tc + i] & 0xFFFF
        pltpu.make_async_copy(hn_hbm.at[pl.ds(tok, 1)], xbuf.at[into, pl.ds(i, 1)],
                              sem.at[into]).start(priority=i % 2)
    y = (_dot(hmid, wd_ref[0]) * gval_ref[0]).astype(BF16).astype(F32)
    lo = pltpu.bitcast(y[:, :HALF_D], jnp.uint32) >> 16
    hi = pltpu.bitcast(y[:, HALF_D:], jnp.uint32) & jnp.uint32(0xFFFF0000)
    out_ref[...] = hi | lo

    @pl.when(step == last)
    def _():
        for t in range(1, FFN_SLOTS):
            wait((last + t) % FFN_SLOTS)


def _expert_ffn(tbl_flat, hn, gval, wg, wu, wd, cap):
    tc = min(256, cap)
    nt = cap // tc
    grid_spec = pltpu.PrefetchScalarGridSpec(
        num_scalar_prefetch=1,
        grid=(N_EXPERTS, nt),
        in_specs=[
            pl.BlockSpec(memory_space=pl.ANY),
            pl.BlockSpec((1, tc, 1), lambda e, j, idx: (e * nt + j, 0, 0)),
            pl.BlockSpec((1, D_MODEL, EXPERT_FF), lambda e, j, idx: (e, 0, 0)),
            pl.BlockSpec((1, D_MODEL, EXPERT_FF), lambda e, j, idx: (e, 0, 0)),
            pl.BlockSpec((1, EXPERT_FF, D_MODEL), lambda e, j, idx: (e, 0, 0)),
        ],
        out_specs=pl.BlockSpec((tc, HALF_D), lambda e, j, idx: (e * nt + j, 0)),
        scratch_shapes=[pltpu.VMEM((FFN_SLOTS, tc, D_MODEL), F32), pltpu.SemaphoreType.DMA((FFN_SLOTS,))],
    )
    return pl.pallas_call(
        functools.partial(_ffn_kernel, tc=tc, nt=nt),
        grid_spec=grid_spec,
        out_shape=jax.ShapeDtypeStruct((N_EXPERTS * cap, HALF_D), jnp.uint32),
        compiler_params=_cparams(("arbitrary", "arbitrary")),
        name="expert_ffn",
    )(tbl_flat, hn, gval.reshape(N_EXPERTS * nt, tc, 1), wg, wu, wd)


def _combine_kernel(tbl_ref, p0_ref, km_ref, h_ref, cnt_ref, g_ref, ye_hbm, out_ref, stage, sem,
                    *, tt, cap, ntile):
    tile = pl.program_id(0)
    slot = tile % COMBINE_SLOTS

    def issue(tl, slt):
        for e in range(N_EXPERTS):
            p0 = p0_ref[e * (ntile + 1) + tl]
            cnt = p0_ref[e * (ntile + 1) + tl + 1] - p0

            def fetch(q, queue, e=e, p0=p0):
                row = e * cap + p0 + q
                pltpu.make_async_copy(ye_hbm.at[pl.ds(row, 1)], stage.at[slt, pl.ds(tbl_ref[row] >> 16, 1)],
                                      sem.at[slt]).start(priority=queue)

            def four(j, carry, fetch=fetch):
                for u in range(4):
                    fetch(4 * j + u, u % 2)
                return carry

            def one(q, carry, fetch=fetch, cnt=cnt):
                fetch((cnt & ~3) + q, 0)
                return carry

            lax.fori_loop(0, cnt >> 2, four, 0)
            lax.fori_loop(0, cnt & 3, one, 0)

    ahead = COMBINE_SLOTS - 1

    @pl.when(tile == 0)
    def _():
        for tl in range(min(ahead, ntile)):
            issue(tl, tl)

    @pl.when(tile + ahead < ntile)
    def _():
        issue(tile + ahead, (tile + ahead) % COMBINE_SLOTS)

    def wait_rows(nrows):
        def body(q, carry):
            pltpu.make_async_copy(ye_hbm.at[pl.ds(0, nrows)], stage.at[slot, pl.ds(0, nrows)], sem.at[slot]).wait()
            return carry
        return body

    total = km_ref[2 * tile + 1]
    lax.fori_loop(0, total >> 3, wait_rows(8), 0)
    lax.fori_loop(0, total & 7, wait_rows(1), 0)

    cnt = cnt_ref[...]
    lo_parts, hi_parts = [], []
    for c in range(HALF_D // LANES):
        cs = slice(c * LANES, (c + 1) * LANES)
        hs = slice(HALF_D + c * LANES, HALF_D + (c + 1) * LANES)

        def add(k, acc, cs=cs):
            w = stage[slot, pl.ds(pl.multiple_of(k * tt, tt), tt), cs]
            pick = cnt > k
            lo = pltpu.bitcast(w << 16, F32)
            hi = pltpu.bitcast(w & jnp.uint32(0xFFFF0000), F32)
            return acc[0] + jnp.where(pick, lo, 0.0), acc[1] + jnp.where(pick, hi, 0.0)

        lo_acc, hi_acc = lax.fori_loop(0, km_ref[2 * tile], add, (h_ref[:, cs], h_ref[:, hs]))
        lo_parts.append(lo_acc)
        hi_parts.append(hi_acc)
    acc = jnp.concatenate(lo_parts + hi_parts, axis=1)
    ms = jnp.mean(acc * acc, axis=-1, keepdims=True)
    out_ref[...] = acc * lax.rsqrt(ms + NORM_EPS) * g_ref[...]


def _combine(tbl_flat, p0_flat, km_flat, h2d, cnt_tok, g, yexp, cap, tt):
    n = h2d.shape[0]
    ntile = n // tt
    grid_spec = pltpu.PrefetchScalarGridSpec(
        num_scalar_prefetch=3,
        grid=(ntile,),
        in_specs=[
            pl.BlockSpec((tt, D_MODEL), lambda i, a, b, c: (i, 0)),
            pl.BlockSpec((tt, 1), lambda i, a, b, c: (i, 0)),
            pl.BlockSpec((1, D_MODEL), lambda i, a, b, c: (0, 0)),
            pl.BlockSpec(memory_space=pl.ANY),
        ],
        out_specs=pl.BlockSpec((tt, D_MODEL), lambda i, a, b, c: (i, 0)),
        scratch_shapes=[pltpu.VMEM((COMBINE_SLOTS, N_EXPERTS * tt, HALF_D), jnp.uint32),
                        pltpu.SemaphoreType.DMA((COMBINE_SLOTS,))],
    )
    return pl.pallas_call(
        functools.partial(_combine_kernel, tt=tt, cap=cap, ntile=ntile),
        grid_spec=grid_spec,
        out_shape=jax.ShapeDtypeStruct((n, D_MODEL), F32),
        compiler_params=_cparams(("arbitrary",)),
        name="moe_combine",
    )(tbl_flat, p0_flat, km_flat, h2d, cnt_tok, g, yexp)


def _trunk(x, p, w_slab, mu_slab, rwp, moe_w, slopes):
    batch, seq, _ = x.shape
    n = batch * seq
    x2d = x.reshape(n, D_MODEL)
    slabs = _inproj(x2d, p["norm_mix_g"], w_slab)
    oatt = _attention(slabs, slopes, p["lambda_q1"], p["lambda_k1"], p["lambda_q2"], p["lambda_k2"],
                      p["subln_g"], batch, seq)
    rva, g, dirp = _rwprep(slabs, mu_slab, *rwp[:7], batch, seq)
    yf, yb = _rwscan(rva, dirp, rwp[7], batch, seq)
    h = _merge(x2d, oatt, yf, yb, g, slabs, p["gate_b"][0].reshape(2, 1, D_MODEL), rwp[8], rwp[9],
               moe_w["wba"], moe_w["wbr"], moe_w["wout"])

    cap = max(1, CAPACITY_FACTOR * n // N_EXPERTS)
    hn, aff = _router(h, p["norm_ffn_g"], moe_w["wr"])
    aff3 = aff.reshape(N_EXPERTS, n // LANES, LANES)
    assert n <= 1 << 16
    tt = LANES
    incl, tbl, gval, cnt = _select(aff3, cap, tt)
    ntile = n // tt
    ends = incl.reshape(N_EXPERTS, n)[:, tt - 1::tt]
    p0 = jnp.concatenate([jnp.zeros((N_EXPERTS, 1), I32), ends], axis=1)
    ct = cnt.reshape(ntile, tt)
    km = jnp.stack([jnp.max(ct, axis=1), jnp.sum(ct, axis=1)], axis=1)
    tbl_flat = tbl.reshape(-1)
    yexp = _expert_ffn(tbl_flat, hn, gval, moe_w["wg"], moe_w["wu"], moe_w["wd"], cap)
    y = _combine(tbl_flat, p0.reshape(-1), km.reshape(-1), h, cnt.reshape(n, 1),
                 p["norm_final_g"].reshape(1, D_MODEL), yexp, cap, tt)
    return y.reshape(batch, seq, D_MODEL)


def kernel(x_prompt, x_sample, norm_mix_g, w_in, shift_mu, lambda_q1, lambda_k1, lambda_q2, lambda_k2, subln_g, rw_w0, rw_w2, rw_a0, rw_a2, rw_g2, rw_k_k, rw_k_a, rw_r_k, lnx_g, lnx_b, gate_b, w_br_att, w_br_rw, w_out, norm_ffn_g, w_router, w_gate_e, w_up_e, w_down_e, norm_final_g):
    p = dict(norm_mix_g=norm_mix_g, lambda_q1=lambda_q1, lambda_k1=lambda_k1, lambda_q2=lambda_q2,
             lambda_k2=lambda_k2, subln_g=subln_g, rw_w0=rw_w0, rw_w2=rw_w2, rw_a0=rw_a0, rw_a2=rw_a2,
             rw_g2=rw_g2, rw_k_k=rw_k_k, rw_k_a=rw_k_a, rw_r_k=rw_r_k, lnx_g=lnx_g, lnx_b=lnx_b,
             gate_b=gate_b, norm_ffn_g=norm_ffn_g, norm_final_g=norm_final_g)
    w_slab, mu_slab = _prep_in_weights(w_in[0], shift_mu[0])
    rwp = _prep_rw_params(p)
    moe_w = dict(
        wba=w_br_att[0].astype(BF16), wbr=w_br_rw[0].astype(BF16), wout=w_out[0].astype(BF16),
        wr=jnp.pad(w_router[0], ((0, 0), (0, LANES - N_EXPERTS))),
        wg=w_gate_e[0].astype(BF16), wu=w_up_e[0].astype(BF16), wd=w_down_e[0].astype(BF16))
    slopes = jnp.asarray([2.0 ** (-8.0 * (i + 1) / ATT_HEADS) for i in range(ATT_HEADS)], F32)
    return (_trunk(x_prompt, p, w_slab, mu_slab, rwp, moe_w, slopes),
            _trunk(x_sample, p, w_slab, mu_slab, rwp, moe_w, slopes))
```

```python
import functools
import math

import jax
import jax.numpy as jnp
from jax import lax
from jax.experimental import pallas as pl
from jax.experimental.pallas import tpu as pltpu

F32 = jnp.float32
BF16 = jnp.bfloat16
I32 = jnp.int32

D_MODEL = 2048
ATT_HEADS = 8
ATT_HEAD_DIM = 64
ATT_WIDTH = ATT_HEADS * 2 * ATT_HEAD_DIM
RW_HEAD = 64
RW_WIDTH = 1024
DECAY_LORA = 96
ICLR_LORA = 96
GATE_LORA = 256
SHIFT_WIDTH = 3 * RW_WIDTH + DECAY_LORA + ICLR_LORA + GATE_LORA
N_EXPERTS = 16
CAPACITY_FACTOR = 2
EXPERT_FF = 1024
NORM_EPS = 1e-6
SUBLN_EPS = 1e-5
LNX_EPS = 64e-5
LAM_INIT = 0.8 - 0.6 * math.exp(-0.3 * 0)

LANES = 128
VMEM_LIMIT = 56 * 1024 * 1024

SL_GATE_ATT, SL_GATE_RW = 0, 16
SL_ATT_Q, SL_ATT_K, SL_ATT_V = 32, 40, 48
SL_RW = 56
N_RW_SLABS = 28
N_SLABS = 84
CHUNK = 64
COMBINE_SLOTS = 2
FFN_SLOTS = 3
HALF_D = D_MODEL // 2


def _cparams(sem):
    return pltpu.CompilerParams(dimension_semantics=sem, vmem_limit_bytes=VMEM_LIMIT)


def _sigmoid(x):
    return 1.0 / (1.0 + jnp.exp(-x))


def _split3(x):
    hi = x.astype(BF16)
    r1 = x - hi.astype(F32)
    mid = r1.astype(BF16)
    lo = (r1 - mid.astype(F32)).astype(BF16)
    return hi, mid, lo


def _dot(a, b):
    return jnp.dot(a, b, preferred_element_type=F32)


def _dot_nt(a, b):
    return lax.dot_general(a, b, (((1,), (1,)), ((), ())), preferred_element_type=F32)


def _dot_tn(a, b):
    return lax.dot_general(a, b, (((0,), (0,)), ((), ())), preferred_element_type=F32)


def _dot_f32(a_bf16_exact, x):
    hi, mid, lo = _split3(x)
    return _dot(a_bf16_exact, hi) + _dot(a_bf16_exact, mid) + _dot(a_bf16_exact, lo)


def _rearrange_in_cols(a):
    att = a[..., :3 * ATT_WIDTH]
    zr = a[..., 3 * ATT_WIDTH:3 * ATT_WIDTH + SHIFT_WIDTH]
    gates = a[..., 3 * ATT_WIDTH + SHIFT_WIDTH:]
    o3 = 3 * RW_WIDTH
    o4 = o3 + DECAY_LORA
    o5 = o4 + ICLR_LORA
    pad = [(0, 0)] * (a.ndim - 1)
    lw = jnp.pad(zr[..., o3:o4], pad + [(0, LANES - DECAY_LORA)])
    la = jnp.pad(zr[..., o4:o5], pad + [(0, LANES - ICLR_LORA)])
    return jnp.concatenate([gates, att, zr[..., :o3], lw, la, zr[..., o5:]], axis=-1)


def _prep_in_weights(w_in, shift_mu):
    w_slab = _rearrange_in_cols(w_in).astype(BF16)
    mu_full = jnp.pad(shift_mu, ((0, 0), (3 * ATT_WIDTH, 2 * D_MODEL)))
    mu_slab = _rearrange_in_cols(mu_full)[:, SL_RW * LANES:]
    return w_slab, mu_slab.reshape(2, 1, N_RW_SLABS * LANES)


def _inproj_kernel(x_ref, g_ref, w_ref, o_ref, xn_ref, *, n_out_slabs):
    @pl.when(pl.program_id(1) == 0)
    def _():
        x = x_ref[...]
        ms = jnp.mean(x * x, axis=-1, keepdims=True)
        xn_ref[...] = (x * lax.rsqrt(ms + NORM_EPS) * g_ref[...]).astype(BF16)

    acc = _dot(xn_ref[...], w_ref[...])
    for c in range(n_out_slabs):
        o_ref[c] = acc[:, c * LANES:(c + 1) * LANES].astype(BF16)


def _inproj(x2d, g, w_slab):
    n = x2d.shape[0]
    tm = min(1024, n)
    tn = 1536
    n_out_slabs = tn // LANES
    grid = (n // tm, (N_SLABS * LANES) // tn)
    return pl.pallas_call(
        functools.partial(_inproj_kernel, n_out_slabs=n_out_slabs),
        grid=grid,
        in_specs=[
            pl.BlockSpec((tm, D_MODEL), lambda i, j: (i, 0)),
            pl.BlockSpec((1, D_MODEL), lambda i, j: (0, 0)),
            pl.BlockSpec((D_MODEL, tn), lambda i, j: (0, j)),
        ],
        out_specs=pl.BlockSpec((n_out_slabs, tm, LANES), lambda i, j: (j, i, 0)),
        out_shape=jax.ShapeDtypeStruct((N_SLABS, n, LANES), BF16),
        scratch_shapes=[pltpu.VMEM((tm, D_MODEL), BF16)],
        compiler_params=_cparams(("parallel", "arbitrary")),
        name="inproj",
    )(x2d, g, w_slab)


def _attn_kernel(slopes_ref, lq1_ref, lk1_ref, lq2_ref, lk2_ref, subg_ref, q_ref, k_ref, v_ref, o_ref,
                 kt1_ref, kt2_ref, vaug_ref, *, seq, tq):
    h = pl.program_id(1)
    qi = pl.program_id(2)
    slope = slopes_ref[h]
    q0 = pl.multiple_of(qi * tq, tq)
    view = pl.ds(q0, seq)

    @pl.when(qi == 0)
    def _():
        kt = k_ref[0].astype(F32).T
        row = lax.broadcasted_iota(I32, kt.shape, 0)
        k1 = jnp.where(row < ATT_HEAD_DIM, kt, 0.0).astype(BF16)
        k2 = jnp.where(row >= ATT_HEAD_DIM, kt, 0.0).astype(BF16)
        lane = lax.broadcasted_iota(I32, (seq, LANES), 1)
        va = jnp.concatenate([v_ref[0], jnp.where(lane == 0, 1.0, 0.0).astype(BF16)], axis=1)
        j = lax.broadcasted_iota(I32, (16, seq), 1)
        r16 = lax.broadcasted_iota(I32, (16, seq), 0)
        jh = (slope * 256.0) * (j >> 8).astype(F32)
        jl = slope * (j & 255).astype(F32)
        for half, sigma in ((slice(0, seq), 1.0), (slice(seq, 2 * seq), -1.0)):
            kt1_ref[:, half] = k1
            kt2_ref[:, half] = k2
            vaug_ref[half, :] = va
            feat = jnp.where(r16 <= 1, sigma,
                             jnp.where(r16 == 2, -sigma * jh, jnp.where(r16 == 3, -sigma * jl, 0.0))).astype(BF16)
            kt1_ref[ATT_HEAD_DIM:ATT_HEAD_DIM + 16, half] = feat
            kt2_ref[0:16, half] = feat

    q = q_ref[0].astype(F32) * (ATT_HEAD_DIM ** -0.5)
    lane = lax.broadcasted_iota(I32, (tq, LANES), 1)
    ip = lax.broadcasted_iota(I32, (tq, LANES), 0) + q0
    ih = slope * (ip & ~255).astype(F32)
    il = slope * (ip & 255).astype(F32)

    def query_side(fl):
        return jnp.where(fl == 0, ih, jnp.where(fl == 1, il, jnp.where(fl <= 3, 1.0, 0.0)))

    lhs1 = jnp.where(lane < ATT_HEAD_DIM, q, query_side(lane - ATT_HEAD_DIM)).astype(BF16)
    lhs2 = jnp.where(lane >= ATT_HEAD_DIM, q, query_side(lane)).astype(BF16)

    di = lax.broadcasted_iota(I32, (tq, tq), 0)
    dj = lax.broadcasted_iota(I32, (tq, tq), 1)
    diag_bias = jnp.where(dj < di, -2.0 * slope * (di - dj).astype(F32), 0.0)

    def weights(lhs, kt_ref):
        s = _dot(lhs, kt_ref[:, view])
        s = jnp.concatenate([s[:, :tq] + diag_bias, s[:, tq:]], axis=1)
        m = jnp.max(s, axis=-1, keepdims=True)
        return jnp.exp(s - m).astype(BF16)

    e = jnp.concatenate([weights(lhs1, kt1_ref), weights(lhs2, kt2_ref)], axis=0)
    oa = _dot(e, vaug_ref[view, :])
    o1 = oa[:tq, :LANES] / oa[:tq, LANES:LANES + 1]
    o2 = oa[tq:, :LANES] / oa[tq:, LANES:LANES + 1]
    lam = (jnp.exp(jnp.sum(lq1_ref[...] * lk1_ref[...], keepdims=True))
           - jnp.exp(jnp.sum(lq2_ref[...] * lk2_ref[...], keepdims=True)) + LAM_INIT)
    out = o1 - lam * o2
    ms = jnp.mean(out * out, axis=-1, keepdims=True)
    y = out * lax.rsqrt(ms + SUBLN_EPS) * subg_ref[...]
    o_ref[0] = (y * (1.0 - LAM_INIT)).astype(BF16)


def _attention(slabs, slopes, lq1, lk1, lq2, lk2, subg, batch, seq):
    n = batch * seq
    tq = min(1024, seq, (1 << 21) // seq)
    nq = seq // tq
    vec = lambda: pl.BlockSpec((1, ATT_HEAD_DIM), lambda b, h, i: (0, 0))
    return pl.pallas_call(
        functools.partial(_attn_kernel, seq=seq, tq=tq),
        grid=(batch, ATT_HEADS, nq),
        in_specs=[
            pl.BlockSpec(memory_space=pltpu.SMEM),
            vec(), vec(), vec(), vec(),
            pl.BlockSpec((1, LANES), lambda b, h, i: (0, 0)),
            pl.BlockSpec((1, tq, LANES), lambda b, h, i: (SL_ATT_Q + h, b * nq + i, 0)),
            pl.BlockSpec((1, seq, LANES), lambda b, h, i: (SL_ATT_K + h, b, 0)),
            pl.BlockSpec((1, seq, LANES), lambda b, h, i: (SL_ATT_V + h, b, 0)),
        ],
        out_specs=pl.BlockSpec((1, tq, LANES), lambda b, h, i: (h, b * nq + i, 0)),
        out_shape=jax.ShapeDtypeStruct((ATT_HEADS, n, LANES), BF16),
        scratch_shapes=[
            pltpu.VMEM((LANES, 2 * seq), BF16),
            pltpu.VMEM((LANES, 2 * seq), BF16),
            pltpu.VMEM((2 * seq, 2 * LANES), BF16),
        ],
        compiler_params=_cparams(("parallel", "parallel", "arbitrary")),
        name="diff_attn",
    )(slopes, lq1, lk1, lq2, lk2, subg, slabs, slabs, slabs)


def _head_segsum(x, bd):
    hi, mid, lo = _split3(x)
    return _dot(hi, bd) + _dot(mid, bd) + _dot(lo, bd)


def _block_ones():
    ri = lax.broadcasted_iota(I32, (LANES, LANES), 0)
    ci = lax.broadcasted_iota(I32, (LANES, LANES), 1)
    return jnp.where((ri >> 6) == (ci >> 6), 1.0, 0.0).astype(BF16)


def _rwprep_kernel(main_ref, prev_ref, next_ref, mu_ref, w0_ref, a0_ref, kk_ref, ka_ref, w2_ref, a2_ref,
                   g2_ref, rva_ref, g_ref, dirp_ref, *, nt):
    i = pl.program_id(1)
    t = main_ref.shape[1]
    hb = prev_ref.shape[1]
    wide = lambda ref: jnp.concatenate([ref[c] for c in range(N_RW_SLABS)], axis=1)
    zb = wide(main_ref)
    prev = jnp.where(i > 0, wide(prev_ref), jnp.zeros((), BF16))
    nxt = jnp.where(i < nt - 1, wide(next_ref), jnp.zeros((), BF16))
    halo = jnp.concatenate([prev, zb, nxt], axis=0)
    ri = lax.broadcasted_iota(I32, (t, t + 2 * hb), 0)
    ci = lax.broadcasted_iota(I32, (t, t + 2 * hb), 1)
    zp = _dot(jnp.where(ci == ri + hb - 1, 1.0, 0.0).astype(BF16), halo)
    zn = _dot(jnp.where(ci == ri + hb + 1, 1.0, 0.0).astype(BF16), halo)
    z = zb.astype(F32)
    z = z + mu_ref[0] * (zp - z) + mu_ref[1] * (zn - z)
    slab = lambda s: z[:, s * LANES:(s + 1) * LANES]

    xw = jnp.tanh(slab(24)).astype(BF16)
    xa = slab(25).astype(BF16)
    xg = _sigmoid(z[:, 26 * LANES:28 * LANES]).astype(BF16)
    g_full = _dot(xg, g2_ref[...])
    lw = [_dot(xw, w2_ref[d]) for d in range(2)]
    la = [_dot(xa, a2_ref[d]) for d in range(2)]
    bd = _block_ones()
    for c in range(8):
        cs = slice(c * LANES, (c + 1) * LANES)
        kc = slab(8 + c)
        kk = kc * kk_ref[c]
        nrm = jnp.sqrt(_head_segsum(kk * kk, bd))
        kk = kk / jnp.maximum(nrm, 1e-12)
        rva_ref[0, c] = slab(c)
        rva_ref[1, c] = slab(16 + c)
        rva_ref[2, c] = -kk
        g_ref[c] = g_full[:, cs]
        for d in range(2):
            dirp_ref[d, 2, c] = -math.exp(-0.5) * _sigmoid(w0_ref[d, c] + lw[d][:, cs])
            asig = _sigmoid(a0_ref[d, c] + la[d][:, cs])
            dirp_ref[d, 0, c] = kc * (1.0 + (asig - 1.0) * ka_ref[c])
            dirp_ref[d, 1, c] = kk * asig


def _prep_rw_params(p):
    vec = lambda a: a.reshape(a.shape[:-1] + (8, 1, LANES))
    pad_rows = lambda a: jnp.pad(a, ((0, 0), (0, LANES - a.shape[1]), (0, 0))).astype(BF16)
    return (vec(p["rw_w0"][0]), vec(p["rw_a0"][0]), vec(p["rw_k_k"][0]), vec(p["rw_k_a"][0]),
            pad_rows(p["rw_w2"][0]), pad_rows(p["rw_a2"][0]), p["rw_g2"][0].astype(BF16),
            vec(p["rw_r_k"][0].reshape(RW_WIDTH)), vec(p["lnx_g"][0]), vec(p["lnx_b"][0]))


def _rwprep(slabs, mu_slab, w0, a0, k_k, k_a, w2, a2, g2, batch, seq):
    n = batch * seq
    t = 256
    nt = seq // t
    hb = 16
    full = lambda shape: pl.BlockSpec(shape, lambda b, i: (0,) * len(shape))
    rows = lambda b, i: b * nt + i
    out_specs = [pl.BlockSpec((3, 8, t, LANES), lambda b, i: (0, 0, rows(b, i), 0)),
                 pl.BlockSpec((8, t, LANES), lambda b, i: (0, rows(b, i), 0)),
                 pl.BlockSpec((2, 3, 8, t, LANES), lambda b, i: (0, 0, 0, rows(b, i), 0))]
    out_shape = [jax.ShapeDtypeStruct((3, 8, n, LANES), F32), jax.ShapeDtypeStruct((8, n, LANES), F32),
                 jax.ShapeDtypeStruct((2, 3, 8, n, LANES), F32)]
    rw_blk = SL_RW // N_RW_SLABS
    return pl.pallas_call(
        functools.partial(_rwprep_kernel, nt=nt),
        grid=(batch, nt),
        in_specs=[
            pl.BlockSpec((N_RW_SLABS, t, LANES), lambda b, i: (rw_blk, rows(b, i), 0)),
            pl.BlockSpec((N_RW_SLABS, hb, LANES),
                         lambda b, i: (rw_blk, jnp.maximum((b * seq + i * t) // hb - 1, 0), 0)),
            pl.BlockSpec((N_RW_SLABS, hb, LANES),
                         lambda b, i: (rw_blk, jnp.minimum((b * seq + (i + 1) * t) // hb, n // hb - 1), 0)),
            full((2, 1, N_RW_SLABS * LANES)),
            full((2, 8, 1, LANES)), full((2, 8, 1, LANES)), full((8, 1, LANES)), full((8, 1, LANES)),
            full((2, LANES, RW_WIDTH)), full((2, LANES, RW_WIDTH)), full((GATE_LORA, RW_WIDTH)),
        ],
        out_specs=out_specs,
        out_shape=out_shape,
        compiler_params=_cparams(("parallel", "parallel")),
        name="rwkv_prep",
    )(slabs, slabs, slabs, mu_slab, w0, a0, k_k, k_a, w2, a2, g2)


def _rwscan_kernel(rvaf_ref, dirf_ref, rvab_ref, dirb_ref, rk_ref, yf_ref, yb_ref, st_ref, *, nc, group):
    @pl.when(pl.program_id(2) == 0)
    def _():
        st_ref[...] = jnp.zeros_like(st_ref)

    lane = lax.broadcasted_iota(I32, (CHUNK, LANES), 1)
    head0 = lane < RW_HEAD
    ri = lax.broadcasted_iota(I32, (LANES, LANES), 0)
    ci = lax.broadcasted_iota(I32, (LANES, LANES), 1)
    same = (ri >> 6) == (ci >> 6)
    tt = ri & (CHUNK - 1)
    ss = ci & (CHUNK - 1)
    eye = jnp.where(ri == ci, 1.0, 0.0).astype(F32)
    tr = lax.broadcasted_iota(I32, (CHUNK, CHUNK), 0)
    tc = lax.broadcasted_iota(I32, (CHUNK, CHUNK), 1)
    bd = _block_ones()
    rk = rk_ref[0]

    def stack(x):
        return jnp.concatenate([jnp.where(head0, x, 0.0), jnp.where(head0, 0.0, x)], axis=0)

    dirs = ((rvaf_ref, dirf_ref, yf_ref), (rvab_ref, dirb_ref, yb_ref))
    strict = (same & (ss < tt), same & (ss > tt))
    incl = (same & (ss <= tt), same & (ss >= tt))
    tri = (jnp.where(tc <= tr, 1.0, 0.0).astype(BF16), jnp.where(tc >= tr, 1.0, 0.0).astype(BF16))
    last = (CHUNK - 1, 0)
    all_insts = [(d, k if d == 0 else nc - 1 - k) for k in range(nc) for d in range(2)]

    def load(which, j, insts):
        lead = (j, 0) if which == 0 else (0, j, 0)
        return [dirs[d][which][lead + (slice(ch * CHUNK, (ch + 1) * CHUNK), slice(None))] for d, ch in insts]

    def state_free_part(insts, out):
        every = range(len(insts))
        r, v, na = (load(0, j, insts) for j in range(3))
        kd, b, ld = (load(1, j, insts) for j in range(3))
        tri3 = [jnp.concatenate([t, t, t], axis=1) for t in tri]
        c = [_dot(tri3[insts[i][0]], jnp.concatenate(_split3(ld[i]), axis=0)) for i in every]
        total = [c[i][last[insts[i][0]]:last[insts[i][0]] + 1] for i in every]
        yield
        e_nc = [jnp.exp(-c[i]) for i in every]
        e_tc = [jnp.exp(total[i] - c[i]) for i in every]
        a_t = [stack(na[i] * jnp.exp(c[i] - ld[i])).astype(BF16) for i in every]
        r_t = [stack(r[i] * jnp.exp(c[i])).astype(BF16) for i in every]
        v_s = [stack(v[i]).astype(BF16) for i in every]
        rhs = [jnp.concatenate([stack(b[i] * e_nc[i]), stack(kd[i] * e_nc[i])], axis=0).astype(BF16) for i in every]
        bk = [jnp.concatenate([stack(b[i] * e_tc[i]), stack(kd[i] * e_tc[i])], axis=0).astype(BF16) for i in every]
        yield
        p = [_dot_nt(jnp.concatenate([a_t[i], r_t[i]], axis=0), rhs[i]) for i in every]
        yield
        n_ab = [jnp.where(strict[insts[i][0]], p[i][:LANES, :LANES], 0.0) for i in every]
        a_ak = [jnp.where(strict[insts[i][0]], p[i][:LANES, LANES:], 0.0).astype(BF16) for i in every]
        p_rb = [jnp.where(incl[insts[i][0]], p[i][LANES:, :LANES], 0.0).astype(BF16) for i in every]
        p_rk = [jnp.where(incl[insts[i][0]], p[i][LANES:, LANES:], 0.0).astype(BF16) for i in every]
        yield
        x = [eye + n_ab[i] for i in every]
        nk = [n_ab[i].astype(BF16) for i in every]
        nk = [_dot(nk[i], nk[i]) for i in every]
        for _ in range(4):
            yield
            both = [_dot(nk[i].astype(BF16), jnp.concatenate([nk[i], x[i]], axis=1).astype(BF16)) for i in every]
            nk = [both[i][:, :LANES] for i in every]
            x = [x[i] + both[i][:, LANES:] for i in every]
        yield
        x = [x[i] + _dot(nk[i].astype(BF16), x[i].astype(BF16)) for i in every]
        w = [_dot(a_ak[i], v_s[i]) for i in every]
        yield
        au = [_dot(x[i].astype(BF16), jnp.concatenate([a_t[i], w[i].astype(BF16)], axis=1)) for i in every]
        yield
        rpp = [jnp.concatenate([r_t[i], p_rb[i], p_rk[i]], axis=1) for i in every]
        bonus = [_head_segsum(r[i] * kd[i] * rk, bd) * v[i] for i in every]
        u0_t = [au[i][:, LANES:].T for i in every]
        v_t = [v_s[i].astype(F32).T.astype(BF16) for i in every]
        out.extend(dict(a_hat=au[i][:, :LANES].astype(BF16), u0_t=u0_t[i], v_t=v_t[i], rpp=rpp[i], bk=bk[i],
                        decay=jnp.exp(total[i]), bonus=bonus[i]) for i in every)

    groups = [all_insts[g:g + group] for g in range(0, len(all_insts), group)]
    pre = [[] for _ in groups]
    parts = [state_free_part(gr, pre[gi]) for gi, gr in enumerate(groups)]
    for _ in parts[0]:
        pass
    st = [st_ref[0], st_ref[1]]
    for gi, gr in enumerate(groups):
        upcoming = parts[gi + 1] if gi + 1 < len(groups) else iter(())
        for (d, ch), f in zip(gr, pre[gi]):
            st_b = st[d].astype(BF16)
            u_t = (_dot_nt(st_b, f["a_hat"]) + f["u0_t"]).astype(BF16)
            uv_t = jnp.concatenate([u_t, f["v_t"]], axis=1)
            y_t = _dot_nt(jnp.concatenate([st_b, uv_t], axis=1), f["rpp"])
            st[d] = st[d] * f["decay"] + _dot(uv_t, f["bk"])
            y = y_t.T
            dirs[d][2][0, ch * CHUNK:(ch + 1) * CHUNK, :] = y[:CHUNK] + y[CHUNK:] + f["bonus"]
            next(upcoming, None)
            next(upcoming, None)
        for _ in upcoming:
            pass
    st_ref[0] = st[0]
    st_ref[1] = st[1]


def _rwscan(rva, dirp, r_k, batch, seq):
    n = batch * seq
    nc = min(16, seq // CHUNK)
    tcs = CHUNK * nc
    nt = seq // tcs
    fwd = lambda bi, c, t: bi * nt + t
    bwd = lambda bi, c, t: bi * nt + nt - 1 - t
    s3 = lambda rows: pl.BlockSpec((1, tcs, LANES), lambda bi, c, t: (c, rows(bi, c, t), 0))
    s_rva = lambda rows: pl.BlockSpec((3, 1, tcs, LANES), lambda bi, c, t: (0, c, rows(bi, c, t), 0))
    s_dir = lambda d, rows: pl.BlockSpec((1, 3, 1, tcs, LANES), lambda bi, c, t: (d, 0, c, rows(bi, c, t), 0))
    out = jax.ShapeDtypeStruct((8, n, LANES), F32)
    return pl.pallas_call(
        functools.partial(_rwscan_kernel, nc=nc, group=2 * nc),
        grid=(batch, 8, nt),
        in_specs=[s_rva(fwd), s_dir(0, fwd), s_rva(bwd), s_dir(1, bwd),
                  pl.BlockSpec((1, 1, LANES), lambda bi, c, t: (c, 0, 0))],
        out_specs=[s3(fwd), s3(bwd)],
        out_shape=[out, out],
        scratch_shapes=[pltpu.VMEM((2, LANES, LANES), F32)],
        compiler_params=_cparams(("parallel", "parallel", "arbitrary")),
        name="rwkv_scan",
    )(rva, dirp, rva, dirp, r_k)


def _merge_kernel(x_ref, oatt_ref, yf_ref, yb_ref, g_ref, gates_ref, gb_ref, lng_ref, lnb_ref,
                  wba_ref, wbr_ref, wout_ref, h_ref):
    bd = _block_ones()
    orw = []
    for c in range(8):
        y = yf_ref[c] + yb_ref[c]
        mu = _head_segsum(y, bd) * (1.0 / RW_HEAD)
        yc = y - mu
        var = _head_segsum(yc * yc, bd) * (1.0 / RW_HEAD)
        yn = yc * lax.rsqrt(var + LNX_EPS) * lng_ref[c] + lnb_ref[c]
        orw.append((yn * g_ref[c]).astype(BF16))
    orw = jnp.concatenate(orw, axis=1)
    oatt = jnp.concatenate([oatt_ref[c] for c in range(8)], axis=1)
    ga = jnp.concatenate([gates_ref[c] for c in range(16)], axis=1).astype(F32) + gb_ref[0]
    gr = jnp.concatenate([gates_ref[16 + c] for c in range(16)], axis=1).astype(F32) + gb_ref[1]
    merged = _sigmoid(ga) * _dot(oatt, wba_ref[...]) + _sigmoid(gr) * _dot(orw, wbr_ref[...])
    h_ref[...] = x_ref[...] + _dot(merged.astype(BF16), wout_ref[...])


def _merge(x2d, oatt, yf, yb, g, slabs, gate_b, lng, lnb, wba, wbr, wout):
    n = x2d.shape[0]
    tm = min(256, n)
    const = lambda shape: pl.BlockSpec(shape, lambda i: (0,) * len(shape), pipeline_mode=pl.Buffered(1))
    s8 = pl.BlockSpec((8, tm, LANES), lambda i: (0, i, 0))
    return pl.pallas_call(
        _merge_kernel,
        grid=(n // tm,),
        in_specs=[
            pl.BlockSpec((tm, D_MODEL), lambda i: (i, 0)),
            s8, s8, s8, s8,
            pl.BlockSpec((32, tm, LANES), lambda i: (0, i, 0)),
            const((2, 1, D_MODEL)), const((8, 1, LANES)), const((8, 1, LANES)),
            const((ATT_WIDTH, D_MODEL)), const((RW_WIDTH, D_MODEL)), const((D_MODEL, D_MODEL)),
        ],
        out_specs=pl.BlockSpec((tm, D_MODEL), lambda i: (i, 0)),
        out_shape=jax.ShapeDtypeStruct((n, D_MODEL), F32),
        compiler_params=_cparams(("parallel",)),
        name="merge_outproj",
    )(x2d, oatt, yf, yb, g, slabs, gate_b, lng, lnb, wba, wbr, wout)


def _router_kernel(h_ref, g_ref, wr_ref, hn_ref, aff_ref):
    x = h_ref[...]
    ms = jnp.mean(x * x, axis=-1, keepdims=True)
    hn = x * lax.rsqrt(ms + NORM_EPS) * g_ref[...]
    hn_ref[...] = hn
    xh, xm, _ = _split3(hn)
    wh, wm, _ = _split3(wr_ref[...])
    logits = _dot(xh, wh) + _dot(xh, wm) + _dot(xm, wh)
    lt = logits.T[:N_EXPERTS]
    m = jnp.max(lt, axis=0, keepdims=True)
    e = jnp.exp(lt - m)
    aff_ref[...] = e / jnp.sum(e, axis=0, keepdims=True)


def _router(h2d, g, wr_pad):
    n = h2d.shape[0]
    tm = min(256, n)
    return pl.pallas_call(
        _router_kernel,
        grid=(n // tm,),
        in_specs=[
            pl.BlockSpec((tm, D_MODEL), lambda i: (i, 0)),
            pl.BlockSpec((1, D_MODEL), lambda i: (0, 0)),
            pl.BlockSpec((D_MODEL, LANES), lambda i: (0, 0)),
        ],
        out_specs=[
            pl.BlockSpec((tm, D_MODEL), lambda i: (i, 0)),
            pl.BlockSpec((N_EXPERTS, tm), lambda i: (0, i)),
        ],
        out_shape=[
            jax.ShapeDtypeStruct((n, D_MODEL), F32),
            jax.ShapeDtypeStruct((N_EXPERTS, n), F32),
        ],
        compiler_params=_cparams(("parallel",)),
        name="router",
    )(h2d, g, wr_pad)


def _select_kernel(aff_ref, incl_ref, tbl_ref, gval_ref, cnt_ref, slot_ref, *, cap, tt):
    bits = pltpu.bitcast(aff_ref[...], I32)
    nrow = bits.shape[1]

    def count(mask):
        c = jnp.sum(jnp.where(mask, 1, 0), axis=2, keepdims=True)
        return jnp.sum(c, axis=1, keepdims=True)

    def body(_, carry):
        lo, hi = carry
        mid = lo + ((hi - lo) >> 1)
        ok = count(bits >= mid) >= cap
        return jnp.where(ok, mid, lo), jnp.where(ok, hi, mid)

    lo0 = jnp.zeros((N_EXPERTS, 1, 1), I32)
    hi0 = jnp.full((N_EXPERTS, 1, 1), 0x7F800000, I32)
    thr, _ = lax.fori_loop(0, 31, body, (lo0, hi0))
    gt = bits > thr
    eq = bits == thr
    need = cap - count(gt)

    ri = lax.broadcasted_iota(I32, (LANES, LANES), 0)
    ci = lax.broadcasted_iota(I32, (LANES, LANES), 1)
    upper = jnp.where(ri <= ci, 1.0, 0.0).astype(BF16)
    rr = lax.broadcasted_iota(I32, (nrow, nrow), 0)
    rc = lax.broadcasted_iota(I32, (nrow, nrow), 1)
    lower_strict = jnp.where(rc < rr, 1.0, 0.0).astype(BF16)

    def incl_prefix(mask):
        x = jnp.where(mask, 1.0, 0.0).astype(BF16)
        incl = _dot(x.reshape(N_EXPERTS * nrow, LANES), upper).reshape(N_EXPERTS, nrow, LANES)
        tot = jnp.broadcast_to(incl[:, :, LANES - 1:LANES], incl.shape).astype(BF16)
        before = jnp.stack([_dot(lower_strict, tot[e]) for e in range(N_EXPERTS)], axis=0)
        return incl + before

    sel = gt | (eq & (incl_prefix(eq) - 1.0 < need.astype(F32)))
    incl_ref[...] = incl_prefix(sel).astype(I32)
    run = jnp.zeros((nrow, LANES), F32)
    for e in range(N_EXPERTS):
        slot_ref[e] = run
        run = run + jnp.where(sel[e], 1.0, 0.0)
    cnt_ref[...] = run.astype(I32)

    pf = lax.broadcasted_iota(I32, (1, cap), 1).astype(F32)
    jrow = lax.broadcasted_iota(I32, (nrow, cap), 0).astype(F32)
    lrow = lax.broadcasted_iota(I32, (LANES, cap), 0).astype(F32)

    def compact(e, carry):
        g = incl_ref[e].astype(F32)
        jsel = jnp.sum(jnp.where(g[:, LANES - 1:LANES] <= pf, 1.0, 0.0), axis=0, keepdims=True)
        onehot = jnp.where(jrow == jsel, 1.0, 0.0).astype(BF16)
        ghi = jnp.floor(g * (1.0 / 256.0))
        glo = g - 256.0 * ghi
        grow = 256.0 * _dot_tn(ghi.astype(BF16), onehot) + _dot_tn(glo.astype(BF16), onehot)
        lstar = jnp.sum(jnp.where(grow <= pf, 1.0, 0.0), axis=0, keepdims=True)
        lsel = lrow == lstar
        ah, am, al = _split3(aff_ref[e])
        arow = _dot_tn(ah, onehot) + _dot_tn(am, onehot) + _dot_tn(al, onehot)
        gval_ref[pl.ds(e, 1), :] = jnp.sum(jnp.where(lsel, arow, 0.0), axis=0, keepdims=True)
        krow = _dot_tn(slot_ref[e].astype(BF16), onehot)
        kk = jnp.sum(jnp.where(lsel, krow, 0.0), axis=0, keepdims=True)
        tok = (jsel * float(LANES) + lstar).astype(I32)
        dest = kk.astype(I32) * tt + (tok & (tt - 1))
        tbl_ref[pl.ds(e, 1), :] = tok | (dest << 16)
        return carry

    lax.fori_loop(0, N_EXPERTS, compact, 0)


def _select(aff3, cap, tt):
    nrow = aff3.shape[1]
    return pl.pallas_call(
        functools.partial(_select_kernel, cap=cap, tt=tt),
        out_shape=[
            jax.ShapeDtypeStruct(aff3.shape, I32),
            jax.ShapeDtypeStruct((N_EXPERTS, cap), I32),
            jax.ShapeDtypeStruct((N_EXPERTS, cap), F32),
            jax.ShapeDtypeStruct((nrow, LANES), I32),
        ],
        scratch_shapes=[pltpu.VMEM((N_EXPERTS, nrow, LANES), F32)],
        compiler_params=pltpu.CompilerParams(vmem_limit_bytes=VMEM_LIMIT),
        name="expert_select",
    )(aff3)


def _ffn_kernel(tbl_ref, hn_hbm, gval_ref, wg_ref, wu_ref, wd_ref, out_ref, xbuf, sem, *, tc, nt):
    step = pl.program_id(0) * nt + pl.program_id(1)
    last = N_EXPERTS * nt - 1
    slot = step % FFN_SLOTS
    ahead = FFN_SLOTS - 1

    def wait(slt):
        pltpu.make_async_copy(hn_hbm.at[pl.ds(0, tc)], xbuf.at[slt], sem.at[slt]).wait()

    @pl.when(step == 0)
    def _():
        for t in range(ahead):
            def body(i, carry, t=t):
                tok = tbl_ref[t * tc + i] & 0xFFFF
                pltpu.make_async_copy(hn_hbm.at[pl.ds(tok, 1)], xbuf.at[t, pl.ds(i, 1)], sem.at[t]).start()
                return carry
            lax.fori_loop(0, tc, body, 0, unroll=8)

    wait(slot)
    xe = xbuf[slot].astype(BF16)
    a = _dot(xe, wg_ref[0])
    u = _dot(xe, wu_ref[0])
    hmid = (a * _sigmoid(a) * u).astype(BF16)
    nxt = jnp.minimum(step + ahead, last)
    into = (step + ahead) % FFN_SLOTS
    for i in range(tc):
        tok = tbl_ref[nxt * tc + i] & 0xFFFF
        pltpu.make_async_copy(hn_hbm.at[pl.ds(tok, 1)], xbuf.at[into, pl.ds(i, 1)], sem.at[into]).start()
    y = (_dot(hmid, wd_ref[0]) * gval_ref[0]).astype(BF16).astype(F32)
    lo = pltpu.bitcast(y[:, :HALF_D], jnp.uint32) >> 16
    hi = pltpu.bitcast(y[:, HALF_D:], jnp.uint32) & jnp.uint32(0xFFFF0000)
    out_ref[...] = hi | lo

    @pl.when(step == last)
    def _():
        for t in range(1, FFN_SLOTS):
            wait((last + t) % FFN_SLOTS)


def _expert_ffn(tbl_flat, hn, gval, wg, wu, wd, cap):
    tc = min(256, cap)
    nt = cap // tc
    grid_spec = pltpu.PrefetchScalarGridSpec(
        num_scalar_prefetch=1,
        grid=(N_EXPERTS, nt),
        in_specs=[
            pl.BlockSpec(memory_space=pl.ANY),
            pl.BlockSpec((1, tc, 1), lambda e, j, idx: (e * nt + j, 0, 0)),
            pl.BlockSpec((1, D_MODEL, EXPERT_FF), lambda e, j, idx: (e, 0, 0)),
            pl.BlockSpec((1, D_MODEL, EXPERT_FF), lambda e, j, idx: (e, 0, 0)),
            pl.BlockSpec((1, EXPERT_FF, D_MODEL), lambda e, j, idx: (e, 0, 0)),
        ],
        out_specs=pl.BlockSpec((tc, HALF_D), lambda e, j, idx: (e * nt + j, 0)),
        scratch_shapes=[pltpu.VMEM((FFN_SLOTS, tc, D_MODEL), F32), pltpu.SemaphoreType.DMA((FFN_SLOTS,))],
    )
    return pl.pallas_call(
        functools.partial(_ffn_kernel, tc=tc, nt=nt),
        grid_spec=grid_spec,
        out_shape=jax.ShapeDtypeStruct((N_EXPERTS * cap, HALF_D), jnp.uint32),
        compiler_params=_cparams(("arbitrary", "arbitrary")),
        name="expert_ffn",
    )(tbl_flat, hn, gval.reshape(N_EXPERTS * nt, tc, 1), wg, wu, wd)


def _combine_kernel(tbl_ref, p0_ref, km_ref, h_ref, cnt_ref, g_ref, ye_hbm, out_ref, stage, sem,
                    *, tt, cap, ntile):
    tile = pl.program_id(0)
    slot = tile % COMBINE_SLOTS

    def issue(tl, slt):
        for e in range(N_EXPERTS):
            p0 = p0_ref[e * (ntile + 1) + tl]
            cnt = p0_ref[e * (ntile + 1) + tl + 1] - p0

            def fetch(q, e=e, p0=p0):
                row = e * cap + p0 + q
                pltpu.make_async_copy(ye_hbm.at[pl.ds(row, 1)], stage.at[slt, pl.ds(tbl_ref[row] >> 16, 1)],
                                      sem.at[slt]).start()

            def four(j, carry, fetch=fetch):
                for u in range(4):
                    fetch(4 * j + u)
                return carry

            def one(q, carry, fetch=fetch, cnt=cnt):
                fetch((cnt & ~3) + q)
                return carry

            lax.fori_loop(0, cnt >> 2, four, 0)
            lax.fori_loop(0, cnt & 3, one, 0)

    ahead = COMBINE_SLOTS - 1

    @pl.when(tile == 0)
    def _():
        for tl in range(min(ahead, ntile)):
            issue(tl, tl)

    @pl.when(tile + ahead < ntile)
    def _():
        issue(tile + ahead, (tile + ahead) % COMBINE_SLOTS)

    def wait_rows(nrows):
        def body(q, carry):
            pltpu.make_async_copy(ye_hbm.at[pl.ds(0, nrows)], stage.at[slot, pl.ds(0, nrows)], sem.at[slot]).wait()
            return carry
        return body

    total = km_ref[2 * tile + 1]
    lax.fori_loop(0, total >> 3, wait_rows(8), 0)
    lax.fori_loop(0, total & 7, wait_rows(1), 0)

    cnt = cnt_ref[...]
    lo_parts, hi_parts = [], []
    for c in range(HALF_D // LANES):
        cs = slice(c * LANES, (c + 1) * LANES)
        hs = slice(HALF_D + c * LANES, HALF_D + (c + 1) * LANES)

        def add(k, acc, cs=cs):
            w = stage[slot, pl.ds(pl.multiple_of(k * tt, tt), tt), cs]
            pick = cnt > k
            lo = pltpu.bitcast(w << 16, F32)
            hi = pltpu.bitcast(w & jnp.uint32(0xFFFF0000), F32)
            return acc[0] + jnp.where(pick, lo, 0.0), acc[1] + jnp.where(pick, hi, 0.0)

        lo_acc, hi_acc = lax.fori_loop(0, km_ref[2 * tile], add, (h_ref[:, cs], h_ref[:, hs]))
        lo_parts.append(lo_acc)
        hi_parts.append(hi_acc)
    acc = jnp.concatenate(lo_parts + hi_parts, axis=1)
    ms = jnp.mean(acc * acc, axis=-1, keepdims=True)
    out_ref[...] = acc * lax.rsqrt(ms + NORM_EPS) * g_ref[...]


def _combine(tbl_flat, p0_flat, km_flat, h2d, cnt_tok, g, yexp, cap, tt):
    n = h2d.shape[0]
    ntile = n // tt
    grid_spec = pltpu.PrefetchScalarGridSpec(
        num_scalar_prefetch=3,
        grid=(ntile,),
        in_specs=[
            pl.BlockSpec((tt, D_MODEL), lambda i, a, b, c: (i, 0)),
            pl.BlockSpec((tt, 1), lambda i, a, b, c: (i, 0)),
            pl.BlockSpec((1, D_MODEL), lambda i, a, b, c: (0, 0)),
            pl.BlockSpec(memory_space=pl.ANY),
        ],
        out_specs=pl.BlockSpec((tt, D_MODEL), lambda i, a, b, c: (i, 0)),
        scratch_shapes=[pltpu.VMEM((COMBINE_SLOTS, N_EXPERTS * tt, HALF_D), jnp.uint32),
                        pltpu.SemaphoreType.DMA((COMBINE_SLOTS,))],
    )
    return pl.pallas_call(
        functools.partial(_combine_kernel, tt=tt, cap=cap, ntile=ntile),
        grid_spec=grid_spec,
        out_shape=jax.ShapeDtypeStruct((n, D_MODEL), F32),
        compiler_params=_cparams(("arbitrary",)),
        name="moe_combine",
    )(tbl_flat, p0_flat, km_flat, h2d, cnt_tok, g, yexp)


def _trunk(x, p, w_slab, mu_slab, rwp, moe_w, slopes):
    batch, seq, _ = x.shape
    n = batch * seq
    x2d = x.reshape(n, D_MODEL)
    slabs = _inproj(x2d, p["norm_mix_g"], w_slab)
    oatt = _attention(slabs, slopes, p["lambda_q1"], p["lambda_k1"], p["lambda_q2"], p["lambda_k2"],
                      p["subln_g"], batch, seq)
    rva, g, dirp = _rwprep(slabs, mu_slab, *rwp[:7], batch, seq)
    yf, yb = _rwscan(rva, dirp, rwp[7], batch, seq)
    h = _merge(x2d, oatt, yf, yb, g, slabs, p["gate_b"][0].reshape(2, 1, D_MODEL), rwp[8], rwp[9],
               moe_w["wba"], moe_w["wbr"], moe_w["wout"])

    cap = max(1, CAPACITY_FACTOR * n // N_EXPERTS)
    hn, aff = _router(h, p["norm_ffn_g"], moe_w["wr"])
    aff3 = aff.reshape(N_EXPERTS, n // LANES, LANES)
    assert n <= 1 << 16
    tt = LANES
    incl, tbl, gval, cnt = _select(aff3, cap, tt)
    ntile = n // tt
    ends = incl.reshape(N_EXPERTS, n)[:, tt - 1::tt]
    p0 = jnp.concatenate([jnp.zeros((N_EXPERTS, 1), I32), ends], axis=1)
    ct = cnt.reshape(ntile, tt)
    km = jnp.stack([jnp.max(ct, axis=1), jnp.sum(ct, axis=1)], axis=1)
    tbl_flat = tbl.reshape(-1)
    yexp = _expert_ffn(tbl_flat, hn, gval, moe_w["wg"], moe_w["wu"], moe_w["wd"], cap)
    y = _combine(tbl_flat, p0.reshape(-1), km.reshape(-1), h, cnt.reshape(n, 1),
                 p["norm_final_g"].reshape(1, D_MODEL), yexp, cap, tt)
    return y.reshape(batch, seq, D_MODEL)


def kernel(x_prompt, x_sample, norm_mix_g, w_in, shift_mu, lambda_q1, lambda_k1, lambda_q2, lambda_k2, subln_g, rw_w0, rw_w2, rw_a0, rw_a2, rw_g2, rw_k_k, rw_k_a, rw_r_k, lnx_g, lnx_b, gate_b, w_br_att, w_br_rw, w_out, norm_ffn_g, w_router, w_gate_e, w_up_e, w_down_e, norm_final_g):
    p = dict(norm_mix_g=norm_mix_g, lambda_q1=lambda_q1, lambda_k1=lambda_k1, lambda_q2=lambda_q2,
             lambda_k2=lambda_k2, subln_g=subln_g, rw_w0=rw_w0, rw_w2=rw_w2, rw_a0=rw_a0, rw_a2=rw_a2,
             rw_g2=rw_g2, rw_k_k=rw_k_k, rw_k_a=rw_k_a, rw_r_k=rw_r_k, lnx_g=lnx_g, lnx_b=lnx_b,
             gate_b=gate_b, norm_ffn_g=norm_ffn_g, norm_final_g=norm_final_g)
    w_slab, mu_slab = _prep_in_weights(w_in[0], shift_mu[0])
    rwp = _prep_rw_params(p)
    moe_w = dict(
        wba=w_br_att[0].astype(BF16), wbr=w_br_rw[0].astype(BF16), wout=w_out[0].astype(BF16),
        wr=jnp.pad(w_router[0], ((0, 0), (0, LANES - N_EXPERTS))),
        wg=w_gate_e[0].astype(BF16), wu=w_up_e[0].astype(BF16), wd=w_down_e[0].astype(BF16))
    slopes = jnp.asarray([2.0 ** (-8.0 * (i + 1) / ATT_HEADS) for i in range(ATT_HEADS)], F32)
    return (_trunk(x_prompt, p, w_slab, mu_slab, rwp, moe_w, slopes),
            _trunk(x_sample, p, w_slab, mu_slab, rwp, moe_w, slopes))
```
